```python
import jax, jax.numpy as jnp
from jax import lax
import numpy as np

D_MODEL = 2048
BATCH = 1
SEQ = 8192
DEPTH = 4
DEC_BATCH = 8
DEC_SEQ = 64
PAST_LEN = 1024

CHUNK = 64
Q_BLOCK = 128
N_MIXERS = 3
HEAD_DIM = 128
ATTN_HEADS = 12
ATTN_W = ATTN_HEADS * HEAD_DIM
N_MEM = 256
MEM_HEADS = 4
MEM_W = MEM_HEADS * HEAD_DIM
GLA_HEADS = 4
GLA_DK = 192
GLA_DV = 384
GLA_KW = GLA_HEADS * GLA_DK
GLA_VW = GLA_HEADS * GLA_DV
GLA_RANK = 16
GLA_GATE_TEMP = 16.0
D_FF = 4 * D_MODEL
RMS_EPS = 1e-6
FOX_COLS = (ATTN_W, ATTN_W, ATTN_W, ATTN_HEADS, MEM_W)
SB_COLS = (ATTN_W, ATTN_W, ATTN_W, MEM_W)
GLA_COLS = (GLA_KW, GLA_KW, GLA_VW, GLA_VW, GLA_RANK, MEM_W)
N_FOX = len(range(0, DEPTH, N_MIXERS))
N_SB = len(range(1, DEPTH, N_MIXERS))
N_GLA = len(range(2, DEPTH, N_MIXERS))

kernel_name = 'hybrid_fox_sb_gla_streaming_step'


def rmsnorm(x, g):
    xf = x.astype(jnp.float32)
    y = xf * lax.rsqrt(jnp.mean(jnp.square(xf), axis=-1, keepdims=True) + RMS_EPS)
    return (y * g.astype(jnp.float32)).astype(x.dtype)


def split_cols(z, sizes):
    return jnp.split(z, [int(c) for c in np.cumsum(sizes)[:-1]], axis=-1)


def heads(a, n, d):
    return a.reshape(a.shape[0], a.shape[1], n, d)


def query_blocks(a, nb):
    return jnp.swapaxes(a.reshape(a.shape[0], nb, Q_BLOCK, *a.shape[2:]), 0, 1)


def unblock(a):
    return jnp.swapaxes(a, 0, 1).reshape(a.shape[1], -1, *a.shape[3:])


def fox_attend(q, k, v, fq, fk, q_pos, k_pos):
    s = jnp.einsum('bqhd,bkhd->bhqk', q, k, preferred_element_type=jnp.float32) * (HEAD_DIM ** -0.5)
    bias = jnp.swapaxes(fq, 1, 2)[:, :, :, None] - jnp.swapaxes(fk, 1, 2)[:, :, None, :]
    s = jnp.where(k_pos[None, :] <= q_pos[:, None], s + bias, -jnp.inf)
    p = jax.nn.softmax(s, axis=-1)
    return jnp.einsum('bhqk,bkhd->bqhd', p.astype(v.dtype), v)


def sb_attend(q, k, v, q_pos, k_pos):
    z = jnp.einsum('bqhd,bkhd->bhqk', q, k, preferred_element_type=jnp.float32) * (HEAD_DIM ** -0.5)
    mask = k_pos[None, :] < q_pos[:, None]
    sp = jnp.where(mask, jax.nn.softplus(z), 0.0)
    after = lax.cumsum(sp, axis=3, reverse=True) - sp
    a = jnp.where(mask, jnp.exp(jax.nn.log_sigmoid(z) - after), 0.0)
    return jnp.einsum('bhqk,bkhd->bqhd', a.astype(v.dtype), v)


def fox_mixer(h, w_in, b_f, past):
    B, T, _ = h.shape
    q, k, v, f_logit, q_mem = split_cols(h @ w_in, FOX_COLS)
    q, k, v = (heads(a, ATTN_HEADS, HEAD_DIM) for a in (q, k, v))
    logf = jax.nn.log_sigmoid((f_logit + b_f).astype(jnp.float32))
    if past is None:
        F = jnp.cumsum(logf, axis=1)
        pos = jnp.arange(T)
        nb = T // Q_BLOCK

        def block(args):
            qb, fb, pb = args
            return fox_attend(qb, k, v, fb, F, pb, pos)

        o = unblock(lax.map(block, (query_blocks(q, nb), query_blocks(F, nb), pos.reshape(nb, Q_BLOCK))))
    else:
        pk, pv, plogf = past
        P = pk.shape[1]
        F = jnp.cumsum(jnp.concatenate([plogf.astype(jnp.float32), logf], axis=1), axis=1)
        o = fox_attend(q, jnp.concatenate([pk, k], axis=1), jnp.concatenate([pv, v], axis=1),
                       F[:, P:], F, P + jnp.arange(T), jnp.arange(P + T))
    return o.reshape(B, T, ATTN_W), q_mem, (k, v, logf.astype(h.dtype))


def sb_mixer(h, w_in, past):
    B, T, _ = h.shape
    q, k, v, q_mem = split_cols(h @ w_in, SB_COLS)
    q, k, v = (heads(a, ATTN_HEADS, HEAD_DIM) for a in (q, k, v))
    if past is None:
        pos = jnp.arange(T)
        nb = T // Q_BLOCK

        def block(args):
            qb, pb = args
            return sb_attend(qb, k, v, pb, pos)

        o = unblock(lax.map(block, (query_blocks(q, nb), pos.reshape(nb, Q_BLOCK))))
    else:
        pk, pv = past
        P = pk.shape[1]
        o = sb_attend(q, jnp.concatenate([pk, k], axis=1), jnp.concatenate([pv, v], axis=1),
                      P + jnp.arange(T), jnp.arange(P + T))
    return o.reshape(B, T, ATTN_W), q_mem, (k, v)


def gla_recurrence(q, k, v, log_a, S0):
    B, T, H, dk = q.shape
    C = min(CHUNK, T)
    n = T // C
    causal = jnp.tril(jnp.ones((C, C), dtype=bool))[None, :, :, None, None]

    def chunks(a):
        return jnp.swapaxes(a.astype(jnp.float32).reshape(B, n, C, *a.shape[2:]), 0, 1)

    def step(S, xs):
        qc, kc, vc, gc = xs
        G = jnp.cumsum(gc, axis=1)
        o_inter = jnp.einsum('bthk,bhkv->bthv', qc * jnp.exp(G), S)
        decay = jnp.exp(jnp.where(causal, G[:, :, None] - G[:, None, :], -jnp.inf))
        A = jnp.einsum('bthk,bshk,btshk->bhts', qc, kc, decay)
        o_intra = jnp.einsum('bhts,bshv->bthv', A, vc)
        G_last = G[:, -1]
        S_new = jnp.exp(G_last)[..., None] * S + jnp.einsum(
            'bshk,bshv->bhkv', kc * jnp.exp(G_last[:, None] - G), vc)
        return S_new, o_inter + o_intra

    S, o = lax.scan(step, S0.astype(jnp.float32), (chunks(q), chunks(k), chunks(v), chunks(log_a)))
    return unblock(o), S.astype(S0.dtype)


def gla_mixer(h, w_in, w_gate2, b_gate, b_r, norm_g, S0):
    B, T, _ = h.shape
    q, k, v, r, g_low, q_mem = split_cols(h @ w_in, GLA_COLS)
    log_a = jax.nn.log_sigmoid((g_low @ w_gate2 + b_gate).astype(jnp.float32)) / GLA_GATE_TEMP
    o, S = gla_recurrence(heads(q, GLA_HEADS, GLA_DK) * (GLA_DK ** -0.5), heads(k, GLA_HEADS, GLA_DK),
                          heads(v, GLA_HEADS, GLA_DV), heads(log_a, GLA_HEADS, GLA_DK), S0)
    o = rmsnorm(o.astype(h.dtype), norm_g).reshape(B, T, GLA_VW) * jax.nn.silu(r + b_r)
    return o, q_mem, S


def memory_kv(mem, g, w_kv):
    B, M, _ = mem.shape
    mk, mv = jnp.split(rmsnorm(mem, g) @ w_kv, 2, axis=-1)
    return mk.reshape(B, M, MEM_HEADS, HEAD_DIM), mv.reshape(B, M, MEM_HEADS, HEAD_DIM)


def memory_attend(q_mem, mk, mv):
    B, T, _ = q_mem.shape
    q = heads(q_mem, MEM_HEADS, HEAD_DIM)
    s = jnp.einsum('bqhd,bmhd->bhqm', q, mk, preferred_element_type=jnp.float32) * (HEAD_DIM ** -0.5)
    p = jax.nn.softmax(s, axis=-1)
    return jnp.einsum('bhqm,bmhd->bqhd', p.astype(mv.dtype), mv).reshape(B, T, MEM_W)


def squared_relu_mlp(h, w_up, w_down):
    return jnp.square(jax.nn.relu(h @ w_up)) @ w_down


def setup_inputs(seed: int = 0) -> dict:
    key = jax.random.key(seed)
    ks = jax.random.split(key, 32)
    D = D_MODEL

    def nrm(i, shape, scale):
        return jax.random.normal(ks[i], shape, jnp.float32) * scale

    return {
        'x_prompt': nrm(0, (BATCH, SEQ, D), 1.0),
        'x_sample': nrm(1, (DEC_BATCH, DEC_SEQ, D), 1.0),
        'cache_fox_k': nrm(2, (N_FOX, DEC_BATCH, PAST_LEN, ATTN_HEADS, HEAD_DIM), 1.0),
        'cache_fox_v': nrm(3, (N_FOX, DEC_BATCH, PAST_LEN, ATTN_HEADS, HEAD_DIM), 1.0),
        'cache_fox_logf': jax.nn.log_sigmoid(2.0 + nrm(4, (N_FOX, DEC_BATCH, PAST_LEN, ATTN_HEADS), 1.0)),
        'cache_sb_k': nrm(5, (N_SB, DEC_BATCH, PAST_LEN, ATTN_HEADS, HEAD_DIM), 1.0),
        'cache_sb_v': nrm(6, (N_SB, DEC_BATCH, PAST_LEN, ATTN_HEADS, HEAD_DIM), 1.0),
        'state_gla': nrm(7, (N_GLA, DEC_BATCH, GLA_HEADS, GLA_DK, GLA_DV), 0.5),
        'cache_mem_k': nrm(8, (DEPTH, DEC_BATCH, N_MEM, MEM_HEADS, HEAD_DIM), 1.0),
        'cache_mem_v': nrm(9, (DEPTH, DEC_BATCH, N_MEM, MEM_HEADS, HEAD_DIM), 1.0),
        'mem_prompt': nrm(10, (BATCH, N_MEM, D), 1.0),
        'norm_mix_g': 1.0 + nrm(11, (DEPTH, D), 0.02),
        'norm_mlp_g': 1.0 + nrm(12, (DEPTH, D), 0.02),
        'norm_mem_g': 1.0 + nrm(13, (DEPTH, D), 0.02),
        'norm_final_g': 1.0 + nrm(14, (D,), 0.02),
        'w_mem_kv': nrm(15, (DEPTH, D, 2 * MEM_W), D ** -0.5),
        'w_in_fox': nrm(16, (N_FOX, D, sum(FOX_COLS)), D ** -0.5),
        'b_forget': jax.random.uniform(ks[17], (N_FOX, ATTN_HEADS), jnp.float32, 0.0, 4.0),
        'w_out_fox': nrm(18, (N_FOX, ATTN_W + MEM_W, D), (ATTN_W + MEM_W) ** -0.5),
        'w_in_sb': nrm(19, (N_SB, D, sum(SB_COLS)), D ** -0.5),
        'w_out_sb': nrm(20, (N_SB, ATTN_W + MEM_W, D), (ATTN_W + MEM_W) ** -0.5),
        'w_in_gla': nrm(21, (N_GLA, D, sum(GLA_COLS)), D ** -0.5),
        'w_gate2_gla': nrm(22, (N_GLA, GLA_RANK, GLA_KW), GLA_RANK ** -0.5),
        'b_gate_gla': nrm(23, (N_GLA, GLA_KW), 0.1),
        'b_outgate_gla': nrm(24, (N_GLA, GLA_VW), 0.02),
        'norm_gla_g': 1.0 + nrm(25, (N_GLA, GLA_HEADS, GLA_DV), 0.02),
        'w_out_gla': nrm(26, (N_GLA, GLA_VW + MEM_W, D), (GLA_VW + MEM_W) ** -0.5),
        'w_up': nrm(27, (DEPTH, D, D_FF), D ** -0.5),
        'w_down': nrm(28, (DEPTH, D_FF, D), D_FF ** -0.5),
    }


def reference(x_prompt, x_sample, cache_fox_k, cache_fox_v, cache_fox_logf, cache_sb_k, cache_sb_v,
              state_gla, cache_mem_k, cache_mem_v, mem_prompt, norm_mix_g, norm_mlp_g, norm_mem_g,
              norm_final_g, w_mem_kv, w_in_fox, b_forget, w_out_fox, w_in_sb, w_out_sb, w_in_gla,
              w_gate2_gla, b_gate_gla, b_outgate_gla, norm_gla_g, w_out_gla, w_up, w_down):

    def trunk(x, sample):
        fox_st, sb_st, gla_st, mem_st = [], [], [], []
        for i in range(DEPTH):
            kind, j = i % N_MIXERS, i // N_MIXERS
            h = rmsnorm(x, norm_mix_g[i])
            if sample:
                mk, mv = cache_mem_k[i], cache_mem_v[i]
            else:
                mk, mv = memory_kv(mem_prompt, norm_mem_g[i], w_mem_kv[i])
                mem_st.append((mk, mv))
            if kind == 0:
                past = (cache_fox_k[j], cache_fox_v[j], cache_fox_logf[j]) if sample else None
                o, q_mem, st = fox_mixer(h, w_in_fox[j], b_forget[j], past)
                fox_st.append(st)
                w_out = w_out_fox[j]
            elif kind == 1:
                past = (cache_sb_k[j], cache_sb_v[j]) if sample else None
                o, q_mem, st = sb_mixer(h, w_in_sb[j], past)
                sb_st.append(st)
                w_out = w_out_sb[j]
            else:
                S0 = state_gla[j] if sample else jnp.zeros((x.shape[0], GLA_HEADS, GLA_DK, GLA_DV), x.dtype)
                o, q_mem, st = gla_mixer(h, w_in_gla[j], w_gate2_gla[j], b_gate_gla[j], b_outgate_gla[j],
                                         norm_gla_g[j], S0)
                gla_st.append(st)
                w_out = w_out_gla[j]
            x = x + jnp.concatenate([o, memory_attend(q_mem, mk, mv)], axis=-1) @ w_out
            x = x + squared_relu_mlp(rmsnorm(x, norm_mlp_g[i]), w_up[i], w_down[i])
        return rmsnorm(x, norm_final_g), fox_st, sb_st, gla_st, mem_st

    y_prompt, fox_p, sb_p, gla_p, mem_p = trunk(x_prompt, False)
    y_sample, fox_s, sb_s, gla_s, _ = trunk(x_sample, True)

    def stack(sts, n):
        return jnp.stack([st[n] for st in sts])

    return (y_prompt, y_sample,
            stack(fox_p, 0), stack(fox_p, 1), stack(fox_p, 2),
            stack(sb_p, 0), stack(sb_p, 1), jnp.stack(gla_p),
            stack(mem_p, 0), stack(mem_p, 1),
            stack(fox_s, 0), stack(fox_s, 1), stack(fox_s, 2),
            stack(sb_s, 0), stack(sb_s, 1), jnp.stack(gla_s))
```

```python
import functools

import jax
import jax.numpy as jnp
from jax import lax
from jax.experimental import pallas as pl
from jax.experimental.pallas import tpu as pltpu

f32 = jnp.float32
bf16 = jnp.bfloat16

RMS_EPS = 1e-6
GLA_GATE_TEMP = 16.0
GLA_CHUNK = 64
GLA_SUB = 16
LANES = 128
FOX_GATE_ROWS = 16
VMEM_LIMIT_BYTES = 56 * 1024 * 1024


def _params(n_axes):
    return pltpu.CompilerParams(dimension_semantics=("arbitrary",) * n_axes,
                                vmem_limit_bytes=VMEM_LIMIT_BYTES)


def _rms(x, g):
    return x * lax.rsqrt(jnp.mean(x * x, axis=-1, keepdims=True) + RMS_EPS) * g


def _softplus(z):
    return jnp.maximum(z, 0.0) + jnp.log1p(jnp.exp(-jnp.abs(z)))


def _log_sigmoid(z):
    return jnp.minimum(z, 0.0) - jnp.log1p(jnp.exp(-jnp.abs(z)))


def _split3(x):
    hi = x.astype(bf16)
    r = x - hi.astype(f32)
    mid = r.astype(bf16)
    lo = (r - mid.astype(f32)).astype(bf16)
    return hi, mid, lo


def _split2(x):
    hi = x.astype(bf16)
    lo = (x - hi.astype(f32)).astype(bf16)
    return hi, lo


def _tri(n, m, fn):
    r = lax.broadcasted_iota(jnp.int32, (n, m), 0)
    c = lax.broadcasted_iota(jnp.int32, (n, m), 1)
    return jnp.where(fn(r, c), 1.0, 0.0).astype(bf16)


def _dot(a, b):
    return jnp.dot(a, b, preferred_element_type=f32)


def _dot_nt(a, b):
    return lax.dot_general(a, b, (((1,), (1,)), ((), ())), preferred_element_type=f32)


def _dot_tn(a, b):
    return lax.dot_general(a, b, (((0,), (0,)), ((), ())), preferred_element_type=f32)


def _cumsum_lanes(x, carry, tri):
    n = x.shape[1]
    outs = []
    for b0 in range(0, n, LANES):
        w = min(LANES, n - b0)
        hi, mid, lo = _split3(x[:, b0:b0 + w])
        t = tri[:w, :w]
        c = _dot(hi, t) + _dot(mid, t) + _dot(lo, t) + carry
        carry = c[:, w - 1:w]
        outs.append(c)
    return outs, carry


def _proj_body(x_ref, g_ref, w_ref, *o_refs, cols, scales):
    h = _rms(x_ref[...], g_ref[...]).astype(bf16)
    y = _dot(h, w_ref[...])
    for o_ref, (c0, c1), sc in zip(o_refs, cols, scales):
        part = y[:, c0:c1]
        if sc != 1.0:
            part = part * sc
        o_ref[...] = part.astype(o_ref.dtype)


def _proj(x, g, w, outs, tm):
    M, D = x.shape
    N = w.shape[1]
    cols = tuple((c0, c1) for c0, c1, _, _ in outs)
    scales = tuple(float(s) for _, _, _, s in outs)
    return pl.pallas_call(
        functools.partial(_proj_body, cols=cols, scales=scales),
        grid=(M // tm,),
        in_specs=[pl.BlockSpec((tm, D), lambda i: (i, 0)),
                  pl.BlockSpec((1, D), lambda i: (0, 0)),
                  pl.BlockSpec((D, N), lambda i: (0, 0))],
        out_specs=[pl.BlockSpec((tm, c1 - c0), lambda i: (i, 0)) for c0, c1 in cols],
        out_shape=[jax.ShapeDtypeStruct((M, c1 - c0), dt) for c0, c1, dt, _ in outs],
        compiler_params=_params(1),
        name="norm_proj",
    )(x, g.reshape(1, D), w)


def _memkv_body(m_ref, g_ref, w_ref, k_ref, v_ref, *, mw):
    h = _rms(m_ref[...], g_ref[...]).astype(bf16)
    w = w_ref[...].astype(bf16)
    k_ref[...] = _dot(h, w[:, :mw])
    v_ref[...] = _dot(h, w[:, mw:])


def _memkv(mem, g, w):
    L, D, two_mw = w.shape
    Mm = mem.shape[0]
    mw = two_mw // 2
    return pl.pallas_call(
        functools.partial(_memkv_body, mw=mw),
        grid=(L,),
        in_specs=[pl.BlockSpec((Mm, D), lambda l: (0, 0)),
                  pl.BlockSpec((None, 1, D), lambda l: (l, 0, 0)),
                  pl.BlockSpec((None, D, two_mw), lambda l: (l, 0, 0))],
        out_specs=[pl.BlockSpec((None, Mm, mw), lambda l: (l, 0, 0))] * 2,
        out_shape=[jax.ShapeDtypeStruct((L, Mm, mw), f32)] * 2,
        compiler_params=_params(1),
        name="mem_kv",
    )(mem, g.reshape(L, 1, D), w)


def _gate_logits(x_ref, g_ref, wf_ref, bf_ref):
    h = _rms(x_ref[...], g_ref[...]).astype(bf16)
    return _log_sigmoid(_dot_nt(wf_ref[...], h) + bf_ref[...])


def _fox_gate_prompt_body(x_ref, g_ref, wf_ref, bf_ref, lf_ref, cf_ref, carry_ref):
    @pl.when(pl.program_id(0) == 0)
    def _():
        carry_ref[...] = jnp.zeros_like(carry_ref)

    logf = _gate_logits(x_ref, g_ref, wf_ref, bf_ref)
    lf_ref[...] = logf
    tri = _tri(LANES, LANES, lambda r, c: r <= c)
    outs, carry = _cumsum_lanes(logf, carry_ref[:, 0:1], tri)
    for b, c in enumerate(outs):
        cf_ref[:, b * LANES:(b + 1) * LANES] = c
    carry_ref[...] = jnp.broadcast_to(carry, carry_ref.shape)


def _fox_gate_prompt(x, g, wf_t, bf, t_prompt, tg):
    D = x.shape[1]
    R = wf_t.shape[0]
    return pl.pallas_call(
        _fox_gate_prompt_body,
        grid=(t_prompt // tg,),
        in_specs=[pl.BlockSpec((tg, D), lambda i: (i, 0)),
                  pl.BlockSpec((1, D), lambda i: (0, 0)),
                  pl.BlockSpec((R, D), lambda i: (0, 0)),
                  pl.BlockSpec((R, 1), lambda i: (0, 0))],
        out_specs=[pl.BlockSpec((R, tg), lambda i: (0, i))] * 2,
        out_shape=[jax.ShapeDtypeStruct((R, t_prompt), f32)] * 2,
        scratch_shapes=[pltpu.VMEM((R, LANES), f32)],
        compiler_params=_params(1),
        name="fox_gate_prompt",
    )(x, g.reshape(1, D), wf_t, bf)


def _fox_gate_sample_body(x_ref, g_ref, wf_ref, bf_ref, pl_ref, lf_ref, cf_ref, *, past, ts):
    logf = _gate_logits(x_ref, g_ref, wf_ref, bf_ref)
    lf_ref[...] = logf
    tri = _tri(LANES, LANES, lambda r, c: r <= c)
    zero = jnp.zeros((logf.shape[0], 1), f32)
    outs, carry = _cumsum_lanes(pl_ref[...], zero, tri)
    for b, c in enumerate(outs):
        cf_ref[:, b * LANES:(b + 1) * LANES] = c
    (new,), _ = _cumsum_lanes(logf, carry, tri)
    cf_ref[:, past:past + ts] = new
    cf_ref[:, past + ts:] = jnp.zeros((logf.shape[0], LANES - ts), f32)


def _fox_gate_sample(x, g, wf_t, bf, plogf_t, row0, ts):
    D = x.shape[1]
    R = wf_t.shape[0]
    B, _, past = plogf_t.shape
    blk0 = row0 // ts
    return pl.pallas_call(
        functools.partial(_fox_gate_sample_body, past=past, ts=ts),
        grid=(B,),
        in_specs=[pl.BlockSpec((ts, D), lambda b: (blk0 + b, 0)),
                  pl.BlockSpec((1, D), lambda b: (0, 0)),
                  pl.BlockSpec((R, D), lambda b: (0, 0)),
                  pl.BlockSpec((R, 1), lambda b: (0, 0)),
                  pl.BlockSpec((None, R, past), lambda b: (b, 0, 0))],
        out_specs=[pl.BlockSpec((None, R, ts), lambda b: (b, 0, 0)),
                   pl.BlockSpec((None, R, past + LANES), lambda b: (b, 0, 0))],
        out_shape=[jax.ShapeDtypeStruct((B, R, ts), f32),
                   jax.ShapeDtypeStruct((B, R, past + LANES), f32)],
        compiler_params=_params(1),
        name="fox_gate_sample",
    )(x, g.reshape(1, D), wf_t, bf, plogf_t)


def _softmax_tile(s, m, l, acc, v):
    m_new = jnp.maximum(m, jnp.max(s, axis=-1, keepdims=True))
    alpha = jnp.exp(m - m_new)
    p = jnp.exp(s - m_new)
    l = alpha * l + jnp.sum(p, axis=-1, keepdims=True)
    acc = alpha * acc + _dot(p.astype(bf16), v)
    return m_new, l, acc


def _causal(n, strict):
    r = lax.broadcasted_iota(jnp.int32, (n, n), 0)
    c = lax.broadcasted_iota(jnp.int32, (n, n), 1)
    return c < r if strict else c <= r


def _fox_prompt_body(q_ref, k_ref, v_ref, f_ref, o_ref, k16, v16, *, t):
    qi = pl.program_id(1)

    @pl.when(qi == 0)
    def _():
        k16[...] = k_ref[...].astype(bf16)
        v16[...] = v_ref[...].astype(bf16)

    q = q_ref[...]

    def scores(kb):
        rows = pl.ds(pl.multiple_of(kb * t, t), t)
        return _dot_nt(q, k16[rows, :]) - f_ref[pl.ds(kb, 1), :], v16[rows, :]

    def below(kb, carry):
        s, v = scores(kb)
        return _softmax_tile(s, *carry, v)

    init = (jnp.full((t, 1), -jnp.inf, f32), jnp.zeros((t, 1), f32), jnp.zeros((t, q.shape[1]), f32))
    carry = lax.fori_loop(0, qi, below, init)
    s, v = scores(qi)
    s = jnp.where(_causal(t, strict=False), s, -jnp.inf)
    _, l, acc = _softmax_tile(s, *carry, v)
    o_ref[...] = (acc / l).astype(o_ref.dtype)


def _fox_prompt(q, k, v, cf, t_prompt, n_heads, hd, t):
    M = q.shape[0]
    nq = t_prompt // t
    return pl.pallas_call(
        functools.partial(_fox_prompt_body, t=t),
        grid=(n_heads, nq),
        in_specs=[pl.BlockSpec((t, hd), lambda h, i: (i, h)),
                  pl.BlockSpec((t_prompt, hd), lambda h, i: (0, h)),
                  pl.BlockSpec((t_prompt, hd), lambda h, i: (0, h)),
                  pl.BlockSpec((None, nq, t), lambda h, i: (h, 0, 0))],
        out_specs=pl.BlockSpec((t, hd), lambda h, i: (i, h)),
        out_shape=jax.ShapeDtypeStruct((M, n_heads * hd), bf16),
        scratch_shapes=[pltpu.VMEM((t_prompt, hd), bf16)] * 2,
        compiler_params=_params(2),
        name="fox_prompt",
    )(q, k, v, cf.reshape(cf.shape[0], nq, t))


def _fox_sample_body(q_ref, k_ref, v_ref, pk_ref, pv_ref, f_ref, _, o_ref, *, past, ts):
    h = pl.program_id(1)
    q = q_ref[...]
    f = f_ref[pl.ds(h, 1), :]
    s_past = _dot_nt(q, pk_ref[...].astype(bf16)) - f[:, :past]
    s_new = _dot_nt(q, k_ref[...].astype(bf16)) - f[:, past:past + ts]
    s_new = jnp.where(_causal(ts, strict=False), s_new, -jnp.inf)
    m = jnp.maximum(jnp.max(s_past, axis=-1, keepdims=True), jnp.max(s_new, axis=-1, keepdims=True))
    p_past = jnp.exp(s_past - m)
    p_new = jnp.exp(s_new - m)
    l = jnp.sum(p_past, axis=-1, keepdims=True) + jnp.sum(p_new, axis=-1, keepdims=True)
    acc = _dot(p_past.astype(bf16), pv_ref[...].astype(bf16)) + _dot(p_new.astype(bf16), v_ref[...].astype(bf16))
    o_ref[...] = (acc / l).astype(o_ref.dtype)


def _sample_attn_call(body, name, q, k, v, pk, pv, extra, extra_spec, o_buf, row0, ts, n_heads, hd):
    B, past, _ = pk.shape
    blk0 = row0 // ts
    new_spec = pl.BlockSpec((ts, hd), lambda b, h: (blk0 + b, h))
    past_spec = pl.BlockSpec((None, past, hd), lambda b, h: (b, 0, h))
    in_specs = [new_spec, new_spec, new_spec, past_spec, past_spec]
    args = [q, k, v, pk, pv]
    if extra is not None:
        in_specs.append(extra_spec)
        args.append(extra)
    in_specs.append(pl.BlockSpec(memory_space=pl.ANY))
    args.append(o_buf)
    return pl.pallas_call(
        body,
        grid=(B, n_heads),
        in_specs=in_specs,
        out_specs=new_spec,
        out_shape=jax.ShapeDtypeStruct(o_buf.shape, o_buf.dtype),
        input_output_aliases={len(args) - 1: 0},
        compiler_params=_params(2),
        name=name,
    )(*args)


def _fox_sample(q, k, v, pk, pv, cf, o_buf, row0, ts, n_heads, hd):
    B, R, wf = cf.shape
    past = pk.shape[1]
    return _sample_attn_call(
        functools.partial(_fox_sample_body, past=past, ts=ts), "fox_sample",
        q, k, v, pk, pv, cf, pl.BlockSpec((None, R, wf), lambda b, h: (b, 0, 0)),
        o_buf, row0, ts, n_heads, hd)


def _sb_tile(q, k, v, tri, run, acc, valid):
    z = _dot_nt(q, k)
    sp = _softplus(z)
    if valid is not None:
        sp = jnp.where(valid, sp, 0.0)
    hi, lo = _split2(sp)
    cum = _dot(hi, tri) + _dot(lo, tri)
    a = jnp.exp(z - cum - run)
    if valid is not None:
        a = jnp.where(valid, a, 0.0)
    return run + cum[:, 0:1], acc + _dot(a.astype(bf16), v)


def _sb_prompt_body(q_ref, k_ref, v_ref, o_ref, k16, v16, *, t):
    qi = pl.program_id(1)

    @pl.when(qi == 0)
    def _():
        k16[...] = k_ref[...].astype(bf16)
        v16[...] = v_ref[...].astype(bf16)

    q = q_ref[...]
    tri = _tri(t, t, lambda r, c: r >= c)

    def tile(kb, carry, valid):
        rows = pl.ds(pl.multiple_of(kb * t, t), t)
        return _sb_tile(q, k16[rows, :], v16[rows, :], tri, *carry, valid)

    init = (jnp.zeros((t, 1), f32), jnp.zeros((t, q.shape[1]), f32))
    carry = tile(qi, init, _causal(t, strict=True))
    _, acc = lax.fori_loop(0, qi, lambda i, c: tile(qi - 1 - i, c, None), carry)
    o_ref[...] = acc.astype(o_ref.dtype)


def _sb_prompt(q, k, v, t_prompt, n_heads, hd, t):
    M = q.shape[0]
    nq = t_prompt // t
    return pl.pallas_call(
        functools.partial(_sb_prompt_body, t=t),
        grid=(n_heads, nq),
        in_specs=[pl.BlockSpec((t, hd), lambda h, i: (i, h)),
                  pl.BlockSpec((t_prompt, hd), lambda h, i: (0, h)),
                  pl.BlockSpec((t_prompt, hd), lambda h, i: (0, h))],
        out_specs=pl.BlockSpec((t, hd), lambda h, i: (i, h)),
        out_shape=jax.ShapeDtypeStruct((M, n_heads * hd), bf16),
        scratch_shapes=[pltpu.VMEM((t_prompt, hd), bf16)] * 2,
        compiler_params=_params(2),
        name="sb_prompt",
    )(q, k, v)


def _sb_sample_body(q_ref, k_ref, v_ref, pk_ref, pv_ref, _, o_ref, *, past, ts, tp):
    q = q_ref[...]
    init = (jnp.zeros((ts, 1), f32), jnp.zeros((ts, q.shape[1]), f32))
    carry = _sb_tile(q, k_ref[...].astype(bf16), v_ref[...].astype(bf16),
                     _tri(ts, ts, lambda r, c: r >= c), *init, _causal(ts, strict=True))
    tri = _tri(tp, tp, lambda r, c: r >= c)
    for p0 in range(past - tp, -1, -tp):
        carry = _sb_tile(q, pk_ref[p0:p0 + tp, :].astype(bf16), pv_ref[p0:p0 + tp, :].astype(bf16),
                         tri, *carry, None)
    o_ref[...] = carry[1].astype(o_ref.dtype)


def _sb_sample(q, k, v, pk, pv, o_buf, row0, ts, n_heads, hd, tp):
    past = pk.shape[1]
    return _sample_attn_call(
        functools.partial(_sb_sample_body, past=past, ts=ts, tp=tp), "sb_sample",
        q, k, v, pk, pv, None, None, o_buf, row0, ts, n_heads, hd)


def _mem_attn_body(q_ref, k_ref, v_ref, *rest, n_heads, hd):
    o_ref = rest[-1]
    for h in range(n_heads):
        sl = slice(h * hd, (h + 1) * hd)
        s = _dot_nt(q_ref[:, sl], k_ref[:, sl].astype(bf16))
        p = jnp.exp(s - jnp.max(s, axis=-1, keepdims=True))
        acc = _dot(p.astype(bf16), v_ref[:, sl].astype(bf16))
        o_ref[:, sl] = (acc / jnp.sum(p, axis=-1, keepdims=True)).astype(o_ref.dtype)


def _mem_attn_prompt(qm, mk, mv, t_prompt, tq, n_heads, hd):
    M, W = qm.shape
    Mm = mk.shape[0]
    return pl.pallas_call(
        functools.partial(_mem_attn_body, n_heads=n_heads, hd=hd),
        grid=(t_prompt // tq,),
        in_specs=[pl.BlockSpec((tq, W), lambda i: (i, 0)),
                  pl.BlockSpec((Mm, W), lambda i: (0, 0)),
                  pl.BlockSpec((Mm, W), lambda i: (0, 0))],
        out_specs=pl.BlockSpec((tq, W), lambda i: (i, 0)),
        out_shape=jax.ShapeDtypeStruct((M, W), bf16),
        compiler_params=_params(1),
        name="mem_attn_prompt",
    )(qm, mk, mv)


def _mem_attn_sample(qm, mk, mv, o_buf, row0, ts, n_heads, hd):
    M, W = qm.shape
    B, Mm, _ = mk.shape
    blk0 = row0 // ts
    row_spec = pl.BlockSpec((ts, W), lambda b: (blk0 + b, 0))
    mem_spec = pl.BlockSpec((None, Mm, W), lambda b: (b, 0, 0))
    return pl.pallas_call(
        functools.partial(_mem_attn_body, n_heads=n_heads, hd=hd),
        grid=(B,),
        in_specs=[row_spec, mem_spec, mem_spec, pl.BlockSpec(memory_space=pl.ANY)],
        out_specs=row_spec,
        out_shape=jax.ShapeDtypeStruct((M, W), bf16),
        input_output_aliases={3: 0},
        compiler_params=_params(1),
        name="mem_attn_sample",
    )(qm, mk, mv, o_buf)


def _gla_body(q_ref, k_ref, v_ref, r_ref, gl_ref, w2_ref, bg_ref, br_ref, ng_ref, s0_ref,
              o_ref, so_ref, S, *, n_prompt_chunks, n_heads, dk, dkp, dv):
    c = pl.program_id(0)
    C, U = GLA_CHUNK, GLA_SUB

    @pl.when(c == 0)
    def _():
        S[...] = jnp.zeros_like(S)

    @pl.when(c >= n_prompt_chunks)
    def _():
        for h in range(n_heads):
            S[h, :dk, :] = s0_ref[h]
            S[h, dk:, :] = jnp.zeros((dkp - dk, dv), f32)

    log_a = _log_sigmoid(_dot(gl_ref[...].astype(bf16), w2_ref[...]) + bg_ref[...]) * (1.0 / GLA_GATE_TEMP)
    tril = _tri(C, C, lambda r, cc: r >= cc)
    row_in_sub = lax.broadcasted_iota(jnp.int32, (U, 1), 0)

    for h in range(n_heads):
        ks = slice(h * dkp, (h + 1) * dkp)
        vs = slice(h * dv, (h + 1) * dv)
        hi, mid, lo = _split3(log_a[:, ks])
        G = _dot(tril, hi) + _dot(tril, mid) + _dot(tril, lo)
        q = q_ref[:, ks]
        k = k_ref[:, ks]
        v = v_ref[:, vs]
        v16 = v.astype(bf16)
        S_h = S[h]
        o_inter = _dot((q * jnp.exp(G)).astype(bf16), S_h.astype(bf16))

        o_rows = []
        for b in range(C // U):
            r0 = b * U
            Gb, qb = G[r0:r0 + U], q[r0:r0 + U]
            o_b = o_inter[r0:r0 + U]
            if b > 0:
                ref = G[r0 - 1:r0]
                qg = qb * jnp.exp(Gb - ref)
                kg = k[:r0] * jnp.exp(ref - G[:r0])
                a = _dot_nt(qg.astype(bf16), kg.astype(bf16))
                o_b = o_b + _dot(a.astype(bf16), v16[:r0])
            for j in range(U):
                s = r0 + j
                e = jnp.exp(jnp.minimum(Gb - G[s:s + 1], 0.0))
                a = jnp.sum(qb * e * k[s:s + 1], axis=-1, keepdims=True)
                a = jnp.where(row_in_sub >= j, a, 0.0)
                o_b = o_b + a * v[s:s + 1]
            o_rows.append(o_b)
        o = jnp.concatenate(o_rows, axis=0)

        g_last = G[C - 1:C]
        kl = k * jnp.exp(g_last - G)
        decay_col = jnp.exp(G.T[:, C - 1:C])
        S_new = decay_col * S_h + _dot_tn(kl.astype(bf16), v16)
        S[h] = S_new
        so_ref[h] = S_new[:dk]

        on = o * lax.rsqrt(jnp.mean(o * o, axis=-1, keepdims=True) + RMS_EPS) * ng_ref[:, vs]
        x = r_ref[:, vs] + br_ref[:, vs]
        o_ref[:, vs] = (on * (x / (1.0 + jnp.exp(-x)))).astype(o_ref.dtype)


def _gla(q, k, v, r, glow, w2, bg, br, ng, s0, n_prompt_chunks, n_heads, dk, dkp, dv):
    M = q.shape[0]
    C = GLA_CHUNK
    n_chunks = M // C
    B = s0.shape[0]
    row = lambda width: pl.BlockSpec((C, width), lambda c: (c, 0))
    const = lambda a: pl.BlockSpec(a.shape, lambda c: (0,) * a.ndim)
    state_in = pl.BlockSpec((None, n_heads, dk, dv), lambda c: (jnp.maximum(c - n_prompt_chunks, 0), 0, 0, 0))
    state_out = pl.BlockSpec((None, n_heads, dk, dv),
                             lambda c: (jnp.maximum(c - (n_prompt_chunks - 1), 0), 0, 0, 0))
    return pl.pallas_call(
        functools.partial(_gla_body, n_prompt_chunks=n_prompt_chunks, n_heads=n_heads, dk=dk, dkp=dkp, dv=dv),
        grid=(n_chunks,),
        in_specs=[row(n_heads * dkp), row(n_heads * dkp), row(n_heads * dv), row(n_heads * dv),
                  row(glow.shape[1]), const(w2), const(bg), const(br), const(ng), state_in],
        out_specs=[row(n_heads * dv), state_out],
        out_shape=[jax.ShapeDtypeStruct((M, n_heads * dv), bf16),
                   jax.ShapeDtypeStruct((B + 1, n_heads, dk, dv), f32)],
        scratch_shapes=[pltpu.VMEM((n_heads, dkp, dv), f32)],
        compiler_params=_params(1),
        name="gla",
    )(q, k, v, r, glow, w2, bg, br, ng, s0)


def _out_proj_body(o_ref, om_ref, x_ref, w_ref, y_ref, *, wo):
    y_ref[...] = x_ref[...] + _dot(o_ref[...], w_ref[:wo, :]) + _dot(om_ref[...], w_ref[wo:, :])


def _out_proj(o, om, x, w, tm):
    M, D = x.shape
    wo, wm = o.shape[1], om.shape[1]
    return pl.pallas_call(
        functools.partial(_out_proj_body, wo=wo),
        grid=(M // tm,),
        in_specs=[pl.BlockSpec((tm, wo), lambda i: (i, 0)),
                  pl.BlockSpec((tm, wm), lambda i: (i, 0)),
                  pl.BlockSpec((tm, D), lambda i: (i, 0)),
                  pl.BlockSpec((wo + wm, D), lambda i: (0, 0))],
        out_specs=pl.BlockSpec((tm, D), lambda i: (i, 0)),
        out_shape=jax.ShapeDtypeStruct((M, D), f32),
        compiler_params=_params(1),
        name="out_proj",
    )(o, om, x, w)


def _mlp_body(x_ref, g_ref, wu_ref, wd_ref, gf_ref, y_ref, h_ref, *, final_norm):
    j = pl.program_id(1)

    @pl.when(j == 0)
    def _():
        x = x_ref[...]
        h_ref[...] = _rms(x, g_ref[...]).astype(bf16)
        y_ref[...] = x

    u = jnp.maximum(_dot(h_ref[...], wu_ref[...]), 0.0)
    y_ref[...] += _dot((u * u).astype(bf16), wd_ref[...])

    if final_norm:
        @pl.when(j == pl.num_programs(1) - 1)
        def _():
            y_ref[...] = _rms(y_ref[...], gf_ref[...])


def _mlp(x, g, wu, wd, gf, final_norm, tm, tf):
    M, D = x.shape
    F = wu.shape[1]
    return pl.pallas_call(
        functools.partial(_mlp_body, final_norm=final_norm),
        grid=(M // tm, F // tf),
        in_specs=[pl.BlockSpec((tm, D), lambda i, j: (i, 0)),
                  pl.BlockSpec((1, D), lambda i, j: (0, 0)),
                  pl.BlockSpec((D, tf), lambda i, j: (0, j)),
                  pl.BlockSpec((tf, D), lambda i, j: (j, 0)),
                  pl.BlockSpec((1, D), lambda i, j: (0, 0))],
        out_specs=pl.BlockSpec((tm, D), lambda i, j: (i, 0)),
        out_shape=jax.ShapeDtypeStruct((M, D), f32),
        scratch_shapes=[pltpu.VMEM((tm, D), bf16)],
        compiler_params=_params(2),
        name="mlp",
    )(x, g.reshape(1, D), wu, wd, gf.reshape(1, D))


def _largest_divisor(n, cap, mult):
    best = None
    for d in range(mult, min(n, cap) + 1, mult):
        if n % d == 0:
            best = d
    assert best is not None, (n, cap, mult)
    return best


def kernel(x_prompt, x_sample, cache_fox_k, cache_fox_v, cache_fox_logf, cache_sb_k, cache_sb_v, state_gla,
           cache_mem_k, cache_mem_v, mem_prompt, norm_mix_g, norm_mlp_g, norm_mem_g, norm_final_g, w_mem_kv,
           w_in_fox, b_forget, w_out_fox, w_in_sb, w_out_sb, w_in_gla, w_gate2_gla, b_gate_gla, b_outgate_gla,
           norm_gla_g, w_out_gla, w_up, w_down):
    Bp, Tp0, D = x_prompt.shape
    Bs, Ts, _ = x_sample.shape
    assert Bp == 1, "the prompt group is handled as one sequence"
    Tp = Bp * Tp0
    M = Tp + Bs * Ts
    depth = norm_mix_g.shape[0]
    H, hd = cache_fox_k.shape[-2:]
    aw = H * hd
    past = cache_fox_k.shape[2]
    Mm, MH = cache_mem_k.shape[2], cache_mem_k.shape[3]
    mw = MH * hd
    GH, dk, dv = state_gla.shape[-3:]
    dkp = -(-dk // LANES) * LANES
    rank = w_gate2_gla.shape[1]
    C = GLA_CHUNK
    assert Ts == C and Tp % C == 0 and M % 16 == 0

    tm = _largest_divisor(M, 512, 16)
    tm_mlp = _largest_divisor(M, 576, 16)
    tf = _largest_divisor(w_up.shape[2], 512, LANES)
    ta = _largest_divisor(Tp, 256, LANES)
    tg = _largest_divisor(Tp, 512, LANES)
    tq_mem = _largest_divisor(Tp, 512, 16)
    tp_sb = _largest_divisor(past, 256, LANES)
    att_scale = hd ** -0.5

    x = jnp.concatenate([x_prompt.reshape(Tp, D), x_sample.reshape(Bs * Ts, D)], axis=0)
    mk_p, mv_p = _memkv(mem_prompt.reshape(Mm, D), norm_mem_g, w_mem_kv)
    wu16 = w_up.astype(bf16)
    wd16 = w_down.astype(bf16)

    def split_rows(a):
        return a[:Tp], a[Tp:]

    fox_p, fox_s, sb_p, sb_s, gla_st = [], [], [], [], []
    for i in range(depth):
        kind, j = i % 3, i // 3
        g = norm_mix_g[i]
        if kind in (0, 1):
            w = (w_in_fox if kind == 0 else w_in_sb)[j]
            n_gate = H if kind == 0 else 0
            wq, wk, wv = (w[:, a * aw:(a + 1) * aw].astype(bf16) for a in range(3))
            wqm = w[:, 3 * aw + n_gate:].astype(bf16)
            (q,) = _proj(x, g, wq, [(0, aw, bf16, att_scale)], tm)
            (k,) = _proj(x, g, wk, [(0, aw, f32, 1.0)], tm)
            (v,) = _proj(x, g, wv, [(0, aw, f32, 1.0)], tm)
            (qm,) = _proj(x, g, wqm, [(0, mw, bf16, att_scale)], tm)
            k_p, k_s = split_rows(k)
            v_p, v_s = split_rows(v)
            if kind == 0:
                wf_t = jnp.pad(w[:, 3 * aw:3 * aw + H].T, ((0, FOX_GATE_ROWS - H), (0, 0))).astype(bf16)
                bf_col = jnp.pad(b_forget[j], (0, FOX_GATE_ROWS - H)).reshape(FOX_GATE_ROWS, 1)
                lf_p, cf_p = _fox_gate_prompt(x, g, wf_t, bf_col, Tp, tg)
                plogf_t = jnp.pad(jnp.swapaxes(cache_fox_logf[j], 1, 2), ((0, 0), (0, FOX_GATE_ROWS - H), (0, 0)))
                lf_s, cf_s = _fox_gate_sample(x, g, wf_t, bf_col, plogf_t, Tp, Ts)
                o = _fox_prompt(q, k, v, cf_p, Tp, H, hd, ta)
                o = _fox_sample(q, k, v, cache_fox_k[j].reshape(Bs, past, aw), cache_fox_v[j].reshape(Bs, past, aw),
                                cf_s, o, Tp, Ts, H, hd)
                fox_p.append((k_p, v_p, lf_p[:H].T))
                fox_s.append((k_s, v_s, jnp.swapaxes(lf_s[:, :H], 1, 2)))
                w_out = w_out_fox[j]
            else:
                o = _sb_prompt(q, k, v, Tp, H, hd, ta)
                o = _sb_sample(q, k, v, cache_sb_k[j].reshape(Bs, past, aw), cache_sb_v[j].reshape(Bs, past, aw),
                               o, Tp, Ts, H, hd, tp_sb)
                sb_p.append((k_p, v_p))
                sb_s.append((k_s, v_s))
                w_out = w_out_sb[j]
        else:
            w = w_in_gla[j]
            kw, vw = GH * dk, GH * dv

            def pad_heads(a):
                a = a.reshape(a.shape[:-1] + (GH, dk))
                a = jnp.pad(a, [(0, 0)] * (a.ndim - 1) + [(0, dkp - dk)])
                return a.reshape(a.shape[:-2] + (GH * dkp,))

            wqk = jnp.concatenate([pad_heads(w[:, :kw]), pad_heads(w[:, kw:2 * kw])], axis=1).astype(bf16)
            wv = w[:, 2 * kw:2 * kw + vw].astype(bf16)
            wr = w[:, 2 * kw + vw:2 * kw + 2 * vw].astype(bf16)
            c0 = 2 * kw + 2 * vw
            wmg = jnp.concatenate([w[:, c0 + rank:], jnp.pad(w[:, c0:c0 + rank], ((0, 0), (0, LANES - rank)))],
                                  axis=1).astype(bf16)
            q, k = _proj(x, g, wqk, [(0, GH * dkp, f32, dk ** -0.5), (GH * dkp, 2 * GH * dkp, f32, 1.0)], tm)
            (v,) = _proj(x, g, wv, [(0, vw, f32, 1.0)], tm)
            (r,) = _proj(x, g, wr, [(0, vw, f32, 1.0)], tm)
            qm, glow = _proj(x, g, wmg, [(0, mw, bf16, att_scale), (mw, mw + LANES, f32, 1.0)], tm)
            w2 = jnp.pad(pad_heads(w_gate2_gla[j]), ((0, LANES - rank), (0, 0))).astype(bf16)
            bg = pad_heads(b_gate_gla[j]).reshape(1, GH * dkp)
            o, st = _gla(q, k, v, r, glow, w2, bg, b_outgate_gla[j].reshape(1, vw), norm_gla_g[j].reshape(1, vw),
                         state_gla[j], Tp // C, GH, dk, dkp, dv)
            gla_st.append(st)
            w_out = w_out_gla[j]

        om = _mem_attn_prompt(qm, mk_p[i], mv_p[i], Tp, tq_mem, MH, hd)
        om = _mem_attn_sample(qm, cache_mem_k[i].reshape(Bs, Mm, mw), cache_mem_v[i].reshape(Bs, Mm, mw),
                              om, Tp, Ts, MH, hd)
        x = _out_proj(o, om, x, w_out.astype(bf16), tm)
        x = _mlp(x, norm_mlp_g[i], wu16[i], wd16[i], norm_final_g, i == depth - 1, tm_mlp, tf)

    def prompt_state(a, tail):
        return a.reshape((Bp, Tp0) + tail)

    def sample_state(a, tail):
        return a.reshape((Bs, Ts) + tail)

    y_p, y_s = split_rows(x)
    stack = lambda sts, n, fn, tail: jnp.stack([fn(st[n], tail) for st in sts])
    gla = jnp.stack(gla_st)
    return (y_p.reshape(Bp, Tp0, D), y_s.reshape(Bs, Ts, D),
            stack(fox_p, 0, prompt_state, (H, hd)), stack(fox_p, 1, prompt_state, (H, hd)),
            stack(fox_p, 2, prompt_state, (H,)),
            stack(sb_p, 0, prompt_state, (H, hd)), stack(sb_p, 1, prompt_state, (H, hd)),
            gla[:, :1],
            mk_p.reshape(depth, Bp, Mm, MH, hd), mv_p.reshape(depth, Bp, Mm, MH, hd),
            stack(fox_s, 0, sample_state, (H, hd)), stack(fox_s, 1, sample_state, (H, hd)),
            jnp.stack([st[2] for st in fox_s]),
            stack(sb_s, 0, sample_state, (H, hd)), stack(sb_s, 1, sample_state, (H, hd)),
            gla[:, 1:])
```

```python
import functools

import jax
import jax.numpy as jnp
from jax import lax
from jax.experimental import pallas as pl
from jax.experimental.pallas import tpu as pltpu

f32 = jnp.float32
bf16 = jnp.bfloat16

RMS_EPS = 1e-6
GLA_GATE_TEMP = 16.0
GLA_CHUNK = 64
GLA_SUB = 16
LOG2E = 1.4426950408889634
MASKED_LOGIT = -1e30
LANES = 128
FOX_GATE_ROWS = 16
VMEM_LIMIT_BYTES = 56 * 1024 * 1024


def _params(n_axes):
    return pltpu.CompilerParams(dimension_semantics=("arbitrary",) * n_axes,
                                vmem_limit_bytes=VMEM_LIMIT_BYTES)


def _rms(x, g):
    return x * lax.rsqrt(jnp.mean(x * x, axis=-1, keepdims=True) + RMS_EPS) * g


def _log_sigmoid(z):
    return jnp.minimum(z, 0.0) - jnp.log1p(jnp.exp(-jnp.abs(z)))


def _split3(x):
    hi = x.astype(bf16)
    r = x - hi.astype(f32)
    mid = r.astype(bf16)
    lo = (r - mid.astype(f32)).astype(bf16)
    return hi, mid, lo


def _split2(x):
    hi = x.astype(bf16)
    lo = (x - hi.astype(f32)).astype(bf16)
    return hi, lo


def _tri(n, m, fn):
    r = lax.broadcasted_iota(jnp.int32, (n, m), 0)
    c = lax.broadcasted_iota(jnp.int32, (n, m), 1)
    return jnp.where(fn(r, c), 1.0, 0.0).astype(bf16)


def _dot(a, b):
    return jnp.dot(a, b, preferred_element_type=f32)


def _dot_nt(a, b):
    return lax.dot_general(a, b, (((1,), (1,)), ((), ())), preferred_element_type=f32)


def _dot_tn(a, b):
    return lax.dot_general(a, b, (((0,), (0,)), ((), ())), preferred_element_type=f32)


def _cumsum_lanes(x, carry, tri):
    n = x.shape[1]
    outs = []
    for b0 in range(0, n, LANES):
        w = min(LANES, n - b0)
        hi, mid, lo = _split3(x[:, b0:b0 + w])
        t = tri[:w, :w]
        c = _dot(hi, t) + _dot(mid, t) + _dot(lo, t) + carry
        carry = c[:, w - 1:w]
        outs.append(c)
    return outs, carry


def _proj_body(x_ref, g_ref, w_ref, *o_refs, cols, scales):
    h = _rms(x_ref[...], g_ref[...]).astype(bf16)
    y = _dot(h, w_ref[...])
    for o_ref, (c0, c1), sc in zip(o_refs, cols, scales):
        part = y[:, c0:c1]
        if sc != 1.0:
            part = part * sc
        o_ref[...] = part.astype(o_ref.dtype)


def _proj(x, g, w, outs, tm):
    M, D = x.shape
    N = w.shape[1]
    cols = tuple((c0, c1) for c0, c1, _, _ in outs)
    scales = tuple(float(s) for _, _, _, s in outs)
    return pl.pallas_call(
        functools.partial(_proj_body, cols=cols, scales=scales),
        grid=(M // tm,),
        in_specs=[pl.BlockSpec((tm, D), lambda i: (i, 0)),
                  pl.BlockSpec((1, D), lambda i: (0, 0)),
                  pl.BlockSpec((D, N), lambda i: (0, 0))],
        out_specs=[pl.BlockSpec((tm, c1 - c0), lambda i: (i, 0)) for c0, c1 in cols],
        out_shape=[jax.ShapeDtypeStruct((M, c1 - c0), dt) for c0, c1, dt, _ in outs],
        compiler_params=_params(1),
        name="norm_proj",
    )(x, g.reshape(1, D), w)


def _memkv_body(m_ref, g_ref, w_ref, k_ref, v_ref, *, mw):
    h = _rms(m_ref[...], g_ref[...]).astype(bf16)
    w = w_ref[...].astype(bf16)
    k_ref[...] = _dot(h, w[:, :mw])
    v_ref[...] = _dot(h, w[:, mw:])


def _memkv(mem, g, w):
    L, D, two_mw = w.shape
    Mm = mem.shape[0]
    mw = two_mw // 2
    return pl.pallas_call(
        functools.partial(_memkv_body, mw=mw),
        grid=(L,),
        in_specs=[pl.BlockSpec((Mm, D), lambda l: (0, 0)),
                  pl.BlockSpec((None, 1, D), lambda l: (l, 0, 0)),
                  pl.BlockSpec((None, D, two_mw), lambda l: (l, 0, 0))],
        out_specs=[pl.BlockSpec((None, Mm, mw), lambda l: (l, 0, 0))] * 2,
        out_shape=[jax.ShapeDtypeStruct((L, Mm, mw), f32)] * 2,
        compiler_params=_params(1),
        name="mem_kv",
    )(mem, g.reshape(L, 1, D), w)


def _gate_logits(x_ref, g_ref, wf_ref, bf_ref):
    h = _rms(x_ref[...], g_ref[...]).astype(bf16)
    return _log_sigmoid(_dot_nt(wf_ref[...], h) + bf_ref[...])


def _fox_gate_prompt_body(x_ref, g_ref, wf_ref, bf_ref, lf_ref, cf_ref, carry_ref):
    @pl.when(pl.program_id(0) == 0)
    def _():
        carry_ref[...] = jnp.zeros_like(carry_ref)

    logf = _gate_logits(x_ref, g_ref, wf_ref, bf_ref)
    lf_ref[...] = logf
    tri = _tri(LANES, LANES, lambda r, c: r <= c)
    outs, carry = _cumsum_lanes(logf, carry_ref[:, 0:1], tri)
    for b, c in enumerate(outs):
        cf_ref[:, b * LANES:(b + 1) * LANES] = c * LOG2E
    carry_ref[...] = jnp.broadcast_to(carry, carry_ref.shape)


def _fox_gate_prompt(x, g, wf_t, bf, t_prompt, tg):
    D = x.shape[1]
    R = wf_t.shape[0]
    return pl.pallas_call(
        _fox_gate_prompt_body,
        grid=(t_prompt // tg,),
        in_specs=[pl.BlockSpec((tg, D), lambda i: (i, 0)),
                  pl.BlockSpec((1, D), lambda i: (0, 0)),
                  pl.BlockSpec((R, D), lambda i: (0, 0)),
                  pl.BlockSpec((R, 1), lambda i: (0, 0))],
        out_specs=[pl.BlockSpec((R, tg), lambda i: (0, i))] * 2,
        out_shape=[jax.ShapeDtypeStruct((R, t_prompt), f32)] * 2,
        scratch_shapes=[pltpu.VMEM((R, LANES), f32)],
        compiler_params=_params(1),
        name="fox_gate_prompt",
    )(x, g.reshape(1, D), wf_t, bf)


def _fox_gate_sample_body(x_ref, g_ref, wf_ref, bf_ref, pl_ref, lf_ref, cf_ref, *, past, ts):
    logf = _gate_logits(x_ref, g_ref, wf_ref, bf_ref)
    lf_ref[...] = logf
    tri = _tri(LANES, LANES, lambda r, c: r <= c)
    zero = jnp.zeros((logf.shape[0], 1), f32)
    outs, carry = _cumsum_lanes(pl_ref[...], zero, tri)
    for b, c in enumerate(outs):
        cf_ref[:, b * LANES:(b + 1) * LANES] = c * LOG2E
    (new,), _ = _cumsum_lanes(logf, carry, tri)
    cf_ref[:, past:past + ts] = new * LOG2E
    cf_ref[:, past + ts:] = jnp.zeros((logf.shape[0], LANES - ts), f32)


def _fox_gate_sample(x, g, wf_t, bf, plogf_t, row0, ts):
    D = x.shape[1]
    R = wf_t.shape[0]
    B, _, past = plogf_t.shape
    blk0 = row0 // ts
    return pl.pallas_call(
        functools.partial(_fox_gate_sample_body, past=past, ts=ts),
        grid=(B,),
        in_specs=[pl.BlockSpec((ts, D), lambda b: (blk0 + b, 0)),
                  pl.BlockSpec((1, D), lambda b: (0, 0)),
                  pl.BlockSpec((R, D), lambda b: (0, 0)),
                  pl.BlockSpec((R, 1), lambda b: (0, 0)),
                  pl.BlockSpec((None, R, past), lambda b: (b, 0, 0))],
        out_specs=[pl.BlockSpec((None, R, ts), lambda b: (b, 0, 0)),
                   pl.BlockSpec((None, R, past + LANES), lambda b: (b, 0, 0))],
        out_shape=[jax.ShapeDtypeStruct((B, R, ts), f32),
                   jax.ShapeDtypeStruct((B, R, past + LANES), f32)],
        compiler_params=_params(1),
        name="fox_gate_sample",
    )(x, g.reshape(1, D), wf_t, bf, plogf_t)


def _softmax_tile(s, m, l, acc, v):
    m_new = jnp.maximum(m, jnp.max(s, axis=-1, keepdims=True))
    alpha = jnp.exp2(m - m_new)
    p = jnp.exp2(s - m_new)
    l = alpha * l + jnp.sum(p, axis=-1, keepdims=True)
    acc = alpha * acc + _dot(p.astype(bf16), v)
    return m_new, l, acc


def _causal(n, strict):
    r = lax.broadcasted_iota(jnp.int32, (n, n), 0)
    c = lax.broadcasted_iota(jnp.int32, (n, n), 1)
    return c < r if strict else c <= r


def _fox_prompt_body(q_ref, k_ref, v_ref, f_ref, o_ref, k16, v16, *, t):
    qi = pl.program_id(1)

    @pl.when(qi == 0)
    def _():
        k16[...] = k_ref[...].astype(bf16)
        v16[...] = v_ref[...].astype(bf16)

    q = q_ref[...]
    col_minus_row = (lax.broadcasted_iota(jnp.int32, (t, t), 1) - lax.broadcasted_iota(jnp.int32, (t, t), 0))

    def tile(kb, carry, max_col_minus_row):
        rows = pl.ds(pl.multiple_of(kb * t, t), t)
        s = _dot_nt(q, k16[rows, :]) - f_ref[pl.ds(kb, 1), :]
        if max_col_minus_row is not None:
            s = jnp.where(col_minus_row <= max_col_minus_row, s, MASKED_LOGIT)
        return _softmax_tile(s, *carry, v16[rows, :])

    init = (jnp.full((t, 1), MASKED_LOGIT, f32), jnp.zeros((t, 1), f32), jnp.zeros((t, q.shape[1]), f32))
    n_pairs = qi // 2
    even, odd = lax.fori_loop(
        0, n_pairs, lambda i, c: (tile(2 * i, c[0], None), tile(2 * i + 1, c[1], None)), (init, init))
    last_even = 2 * n_pairs
    m_a, l_a, acc_a = tile(last_even, even, (qi - last_even) * t)
    m_b, l_b, acc_b = tile(qi, odd, jnp.where(qi % 2 == 1, 0, -t))
    m = jnp.maximum(m_a, m_b)
    w_a = jnp.exp2(m_a - m)
    w_b = jnp.exp2(m_b - m)
    o_ref[...] = ((w_a * acc_a + w_b * acc_b) / (w_a * l_a + w_b * l_b)).astype(o_ref.dtype)


def _fox_prompt(q, k, v, cf, t_prompt, n_heads, hd, t):
    M = q.shape[0]
    nq = t_prompt // t
    return pl.pallas_call(
        functools.partial(_fox_prompt_body, t=t),
        grid=(n_heads, nq),
        in_specs=[pl.BlockSpec((t, hd), lambda h, i: (i, h)),
                  pl.BlockSpec((t_prompt, hd), lambda h, i: (0, h)),
                  pl.BlockSpec((t_prompt, hd), lambda h, i: (0, h)),
                  pl.BlockSpec((None, nq, t), lambda h, i: (h, 0, 0))],
        out_specs=pl.BlockSpec((t, hd), lambda h, i: (i, h)),
        out_shape=jax.ShapeDtypeStruct((M, n_heads * hd), bf16),
        scratch_shapes=[pltpu.VMEM((t_prompt, hd), bf16)] * 2,
        compiler_params=_params(2),
        name="fox_prompt",
    )(q, k, v, cf.reshape(cf.shape[0], nq, t))


def _fox_sample_body(q_ref, k_ref, v_ref, pk_ref, pv_ref, f_ref, _, o_ref, *, past, ts, n_heads, hd):
    for h in range(n_heads):
        cols = slice(h * hd, (h + 1) * hd)
        q = q_ref[:, cols]
        f = f_ref[h:h + 1, :]
        s_past = _dot_nt(q, pk_ref[:, h, :].astype(bf16)) - f[:, :past]
        s_new = _dot_nt(q, k_ref[:, cols].astype(bf16)) - f[:, past:past + ts]
        s_new = jnp.where(_causal(ts, strict=False), s_new, -jnp.inf)
        m = jnp.maximum(jnp.max(s_past, axis=-1, keepdims=True), jnp.max(s_new, axis=-1, keepdims=True))
        p_past = jnp.exp2(s_past - m)
        p_new = jnp.exp2(s_new - m)
        l = jnp.sum(p_past, axis=-1, keepdims=True) + jnp.sum(p_new, axis=-1, keepdims=True)
        acc = (_dot(p_past.astype(bf16), pv_ref[:, h, :].astype(bf16))
               + _dot(p_new.astype(bf16), v_ref[:, cols].astype(bf16)))
        o_ref[:, cols] = (acc / l).astype(o_ref.dtype)


def _sample_attn_call(body, name, q, k, v, pk, pv, extra, o_buf, row0, ts):
    B, past, n_heads, hd = pk.shape
    blk0 = row0 // ts
    new_spec = pl.BlockSpec((ts, n_heads * hd), lambda b: (blk0 + b, 0))
    past_spec = pl.BlockSpec((None, past, n_heads, hd), lambda b: (b, 0, 0, 0))
    in_specs = [new_spec, new_spec, new_spec, past_spec, past_spec]
    args = [q, k, v, pk, pv]
    if extra is not None:
        in_specs.append(pl.BlockSpec((None,) + extra.shape[1:], lambda b: (b, 0, 0)))
        args.append(extra)
    in_specs.append(pl.BlockSpec(memory_space=pl.ANY))
    args.append(o_buf)
    return pl.pallas_call(
        functools.partial(body, past=past, ts=ts, n_heads=n_heads, hd=hd),
        grid=(B,),
        in_specs=in_specs,
        out_specs=new_spec,
        out_shape=jax.ShapeDtypeStruct(o_buf.shape, o_buf.dtype),
        input_output_aliases={len(args) - 1: 0},
        compiler_params=_params(1),
        name=name,
    )(*args)


def _fox_sample(q, k, v, pk, pv, cf, o_buf, row0, ts):
    return _sample_attn_call(_fox_sample_body, "fox_sample", q, k, v, pk, pv, cf, o_buf, row0, ts)


def _tri2(n):
    tri = _tri(n, n, lambda r, c: r >= c)
    return jnp.concatenate([tri, tri], axis=0)


def _sb_scores(q, k, tri2, valid):
    z = _dot_nt(q, k)
    sp = jnp.maximum(z, 0.0) + jnp.log(1.0 + jnp.exp2(-jnp.abs(z))) * LOG2E
    if valid is not None:
        sp = jnp.where(valid, sp, 0.0)
    cum = _dot(jnp.concatenate(_split2(sp), axis=1), tri2)
    return z - cum, cum[:, 0:1]


def _sb_out(zc, run, v, valid):
    a = jnp.exp2(zc)
    if valid is not None:
        a = jnp.where(valid, a, 0.0)
    return jnp.exp2(-run) * _dot(a.astype(bf16), v)


def _sb_pair(q, right, left, tri2, run, acc, valid_right=None, valid_left=None):
    (k_r, v_r), (k_l, v_l) = right, left
    zc_r, tot_r = _sb_scores(q, k_r, tri2, valid_right)
    zc_l, tot_l = _sb_scores(q, k_l, tri2, valid_left)
    acc = acc + _sb_out(zc_r, run, v_r, valid_right) + _sb_out(zc_l, run + tot_r, v_l, valid_left)
    return run + tot_r + tot_l, acc


def _sb_prompt_body(q_ref, k_ref, v_ref, o_ref, k16, v16, tri2_ref, *, tq, tk):
    qi = pl.program_id(1)

    @pl.when(qi == 0)
    def _():
        k16[...] = k_ref[...].astype(bf16)
        v16[...] = v_ref[...].astype(bf16)
        tri2_ref[...] = _tri2(tk)

    q = q_ref[...]
    tri2 = tri2_ref[...]

    def kv(kb):
        rows = pl.ds(pl.multiple_of(kb * tk, tk), tk)
        return k16[rows, :], v16[rows, :]

    r = lax.broadcasted_iota(jnp.int32, (tq, tk), 0)
    c = lax.broadcasted_iota(jnp.int32, (tq, tk), 1)
    init = (jnp.zeros((tq, 1), f32), jnp.zeros((tq, q.shape[1]), f32))
    carry = _sb_pair(q, kv(2 * qi + 1), kv(2 * qi), tri2, *init, valid_right=c + tk < r, valid_left=c < r)

    def below(i, carry):
        kb = 2 * (qi - 1 - i)
        return _sb_pair(q, kv(kb + 1), kv(kb), tri2, *carry)

    _, acc = lax.fori_loop(0, qi, below, carry)
    o_ref[...] = acc.astype(o_ref.dtype)


def _sb_prompt(q, k, v, t_prompt, n_heads, hd, tq):
    M = q.shape[0]
    tk = tq // 2
    return pl.pallas_call(
        functools.partial(_sb_prompt_body, tq=tq, tk=tk),
        grid=(n_heads, t_prompt // tq),
        in_specs=[pl.BlockSpec((tq, hd), lambda h, i: (i, h)),
                  pl.BlockSpec((t_prompt, hd), lambda h, i: (0, h)),
                  pl.BlockSpec((t_prompt, hd), lambda h, i: (0, h))],
        out_specs=pl.BlockSpec((tq, hd), lambda h, i: (i, h)),
        out_shape=jax.ShapeDtypeStruct((M, n_heads * hd), bf16),
        scratch_shapes=[pltpu.VMEM((t_prompt, hd), bf16)] * 2 + [pltpu.VMEM((2 * tk, tk), bf16)],
        compiler_params=_params(2),
        name="sb_prompt",
    )(q, k, v)


def _sb_sample_body(q_ref, k_ref, v_ref, pk_ref, pv_ref, _, o_ref, *, past, ts, n_heads, hd, tp):
    tri2_new = _tri2(ts)
    tri2 = _tri2(tp)
    strict = _causal(ts, strict=True)
    for h in range(n_heads):
        cols = slice(h * hd, (h + 1) * hd)
        q = q_ref[:, cols]
        zc, run = _sb_scores(q, k_ref[:, cols].astype(bf16), tri2_new, strict)
        acc = _sb_out(zc, jnp.zeros((ts, 1), f32), v_ref[:, cols].astype(bf16), strict)

        def past_kv(p0):
            return pk_ref[p0:p0 + tp, h, :].astype(bf16), pv_ref[p0:p0 + tp, h, :].astype(bf16)

        for p0 in range(past - tp, -1, -2 * tp):
            run, acc = _sb_pair(q, past_kv(p0), past_kv(p0 - tp), tri2, run, acc)
        o_ref[:, cols] = acc.astype(o_ref.dtype)


def _sb_sample(q, k, v, pk, pv, o_buf, row0, ts, tp):
    return _sample_attn_call(functools.partial(_sb_sample_body, tp=tp), "sb_sample",
                             q, k, v, pk, pv, None, o_buf, row0, ts)


def _mem_attn_body(q_ref, k_ref, v_ref, *rest, n_heads, hd):
    o_ref = rest[-1]
    for h in range(n_heads):
        sl = slice(h * hd, (h + 1) * hd)
        s = _dot_nt(q_ref[:, sl], k_ref[:, sl].astype(bf16))
        p = jnp.exp(s - jnp.max(s, axis=-1, keepdims=True))
        acc = _dot(p.astype(bf16), v_ref[:, sl].astype(bf16))
        o_ref[:, sl] = (acc / jnp.sum(p, axis=-1, keepdims=True)).astype(o_ref.dtype)


def _mem_attn_prompt(qm, mk, mv, t_prompt, tq, n_heads, hd):
    M, W = qm.shape
    Mm = mk.shape[0]
    return pl.pallas_call(
        functools.partial(_mem_attn_body, n_heads=n_heads, hd=hd),
        grid=(t_prompt // tq,),
        in_specs=[pl.BlockSpec((tq, W), lambda i: (i, 0)),
                  pl.BlockSpec((Mm, W), lambda i: (0, 0)),
                  pl.BlockSpec((Mm, W), lambda i: (0, 0))],
        out_specs=pl.BlockSpec((tq, W), lambda i: (i, 0)),
        out_shape=jax.ShapeDtypeStruct((M, W), bf16),
        compiler_params=_params(1),
        name="mem_attn_prompt",
    )(qm, mk, mv)


def _mem_attn_sample(qm, mk, mv, o_buf, row0, ts, n_heads, hd):
    M, W = qm.shape
    B, Mm, _ = mk.shape
    blk0 = row0 // ts
    row_spec = pl.BlockSpec((ts, W), lambda b: (blk0 + b, 0))
    mem_spec = pl.BlockSpec((None, Mm, W), lambda b: (b, 0, 0))
    return pl.pallas_call(
        functools.partial(_mem_attn_body, n_heads=n_heads, hd=hd),
        grid=(B,),
        in_specs=[row_spec, mem_spec, mem_spec, pl.BlockSpec(memory_space=pl.ANY)],
        out_specs=row_spec,
        out_shape=jax.ShapeDtypeStruct((M, W), bf16),
        input_output_aliases={3: 0},
        compiler_params=_params(1),
        name="mem_attn_sample",
    )(qm, mk, mv, o_buf)


def _gla_body(q_ref, k_ref, v_ref, r_ref, gl_ref, w2_ref, bg_ref, br_ref, ng_ref, s0_ref,
              o_ref, so_ref, S, *, n_prompt_chunks, n_heads, dk, dkp, dv):
    c = pl.program_id(0)
    C, U = GLA_CHUNK, GLA_SUB

    @pl.when(c == 0)
    def _():
        S[...] = jnp.zeros_like(S)

    @pl.when(c >= n_prompt_chunks)
    def _():
        for h in range(n_heads):
            S[h, :dk, :] = s0_ref[h]
            S[h, dk:, :] = jnp.zeros((dkp - dk, dv), f32)

    log_a = _log_sigmoid(_dot(gl_ref[...].astype(bf16), w2_ref[...]) + bg_ref[...]) * (1.0 / GLA_GATE_TEMP)
    tril = _tri(C, C, lambda r, cc: r >= cc)
    row_in_sub = lax.broadcasted_iota(jnp.int32, (U, 1), 0)

    for h in range(n_heads):
        ks = slice(h * dkp, (h + 1) * dkp)
        vs = slice(h * dv, (h + 1) * dv)
        hi, mid, lo = _split3(log_a[:, ks])
        G = _dot(tril, hi) + _dot(tril, mid) + _dot(tril, lo)
        q = q_ref[:, ks]
        k = k_ref[:, ks]
        v = v_ref[:, vs]
        v16 = v.astype(bf16)
        S_h = S[h]
        o_inter = _dot((q * jnp.exp(G)).astype(bf16), S_h.astype(bf16))

        o_rows = []
        for b in range(C // U):
            r0 = b * U
            Gb, qb = G[r0:r0 + U], q[r0:r0 + U]
            o_b = o_inter[r0:r0 + U]
            if b > 0:
                ref = G[r0 - 1:r0]
                qg = qb * jnp.exp(Gb - ref)
                kg = k[:r0] * jnp.exp(ref - G[:r0])
                a = _dot_nt(qg.astype(bf16), kg.astype(bf16))
                o_b = o_b + _dot(a.astype(bf16), v16[:r0])
            for j in range(U):
                s = r0 + j
                e = jnp.exp(jnp.minimum(Gb - G[s:s + 1], 0.0))
                a = jnp.sum(qb * e * k[s:s + 1], axis=-1, keepdims=True)
                a = jnp.where(row_in_sub >= j, a, 0.0)
                o_b = o_b + a * v[s:s + 1]
            o_rows.append(o_b)
        o = jnp.concatenate(o_rows, axis=0)

        g_last = G[C - 1:C]
        kl = k * jnp.exp(g_last - G)
        decay_col = jnp.exp(G.T[:, C - 1:C])
        S_new = decay_col * S_h + _dot_tn(kl.astype(bf16), v16)
        S[h] = S_new
        so_ref[h] = S_new[:dk]

        on = o * lax.rsqrt(jnp.mean(o * o, axis=-1, keepdims=True) + RMS_EPS) * ng_ref[:, vs]
        x = r_ref[:, vs] + br_ref[:, vs]
        o_ref[:, vs] = (on * (x / (1.0 + jnp.exp(-x)))).astype(o_ref.dtype)


def _gla(q, k, v, r, glow, w2, bg, br, ng, s0, n_prompt_chunks, n_heads, dk, dkp, dv):
    M = q.shape[0]
    C = GLA_CHUNK
    n_chunks = M // C
    B = s0.shape[0]
    row = lambda width: pl.BlockSpec((C, width), lambda c: (c, 0))
    const = lambda a: pl.BlockSpec(a.shape, lambda c: (0,) * a.ndim)
    state_in = pl.BlockSpec((None, n_heads, dk, dv), lambda c: (jnp.maximum(c - n_prompt_chunks, 0), 0, 0, 0))
    state_out = pl.BlockSpec((None, n_heads, dk, dv),
                             lambda c: (jnp.maximum(c - (n_prompt_chunks - 1), 0), 0, 0, 0))
    return pl.pallas_call(
        functools.partial(_gla_body, n_prompt_chunks=n_prompt_chunks, n_heads=n_heads, dk=dk, dkp=dkp, dv=dv),
        grid=(n_chunks,),
        in_specs=[row(n_heads * dkp), row(n_heads * dkp), row(n_heads * dv), row(n_heads * dv),
                  row(glow.shape[1]), const(w2), const(bg), const(br), const(ng), state_in],
        out_specs=[row(n_heads * dv), state_out],
        out_shape=[jax.ShapeDtypeStruct((M, n_heads * dv), bf16),
                   jax.ShapeDtypeStruct((B + 1, n_heads, dk, dv), f32)],
        scratch_shapes=[pltpu.VMEM((n_heads, dkp, dv), f32)],
        compiler_params=_params(1),
        name="gla",
    )(q, k, v, r, glow, w2, bg, br, ng, s0)


def _out_proj_body(o_ref, om_ref, x_ref, w_ref, y_ref, *, wo):
    y_ref[...] = x_ref[...] + _dot(o_ref[...], w_ref[:wo, :]) + _dot(om_ref[...], w_ref[wo:, :])


def _out_proj(o, om, x, w, tm):
    M, D = x.shape
    wo, wm = o.shape[1], om.shape[1]
    return pl.pallas_call(
        functools.partial(_out_proj_body, wo=wo),
        grid=(M // tm,),
        in_specs=[pl.BlockSpec((tm, wo), lambda i: (i, 0)),
                  pl.BlockSpec((tm, wm), lambda i: (i, 0)),
                  pl.BlockSpec((tm, D), lambda i: (i, 0)),
                  pl.BlockSpec((wo + wm, D), lambda i: (0, 0))],
        out_specs=pl.BlockSpec((tm, D), lambda i: (i, 0)),
        out_shape=jax.ShapeDtypeStruct((M, D), f32),
        compiler_params=_params(1),
        name="out_proj",
    )(o, om, x, w)


def _mlp_body(x_ref, g_ref, wu_ref, wd_ref, gf_ref, y_ref, h_ref, *, final_norm):
    j = pl.program_id(1)

    @pl.when(j == 0)
    def _():
        x = x_ref[...]
        h_ref[...] = _rms(x, g_ref[...]).astype(bf16)
        y_ref[...] = x

    u = jnp.maximum(_dot(h_ref[...], wu_ref[...]), 0.0)
    y_ref[...] += _dot((u * u).astype(bf16), wd_ref[...])

    if final_norm:
        @pl.when(j == pl.num_programs(1) - 1)
        def _():
            y_ref[...] = _rms(y_ref[...], gf_ref[...])


def _mlp(x, g, wu, wd, gf, final_norm, tm, tf):
    M, D = x.shape
    F = wu.shape[1]
    return pl.pallas_call(
        functools.partial(_mlp_body, final_norm=final_norm),
        grid=(M // tm, F // tf),
        in_specs=[pl.BlockSpec((tm, D), lambda i, j: (i, 0)),
                  pl.BlockSpec((1, D), lambda i, j: (0, 0)),
                  pl.BlockSpec((D, tf), lambda i, j: (0, j)),
                  pl.BlockSpec((tf, D), lambda i, j: (j, 0)),
                  pl.BlockSpec((1, D), lambda i, j: (0, 0))],
        out_specs=pl.BlockSpec((tm, D), lambda i, j: (i, 0)),
        out_shape=jax.ShapeDtypeStruct((M, D), f32),
        scratch_shapes=[pltpu.VMEM((tm, D), bf16)],
        compiler_params=_params(2),
        name="mlp",
    )(x, g.reshape(1, D), wu, wd, gf.reshape(1, D))


def _largest_divisor(n, cap, mult):
    best = None
    for d in range(mult, min(n, cap) + 1, mult):
        if n % d == 0:
            best = d
    assert best is not None, (n, cap, mult)
    return best


def kernel(x_prompt, x_sample, cache_fox_k, cache_fox_v, cache_fox_logf, cache_sb_k, cache_sb_v, state_gla,
           cache_mem_k, cache_mem_v, mem_prompt, norm_mix_g, norm_mlp_g, norm_mem_g, norm_final_g, w_mem_kv,
           w_in_fox, b_forget, w_out_fox, w_in_sb, w_out_sb, w_in_gla, w_gate2_gla, b_gate_gla, b_outgate_gla,
           norm_gla_g, w_out_gla, w_up, w_down):
    Bp, Tp0, D = x_prompt.shape
    Bs, Ts, _ = x_sample.shape
    assert Bp == 1, "the prompt group is handled as one sequence"
    Tp = Bp * Tp0
    M = Tp + Bs * Ts
    depth = norm_mix_g.shape[0]
    H, hd = cache_fox_k.shape[-2:]
    aw = H * hd
    past = cache_fox_k.shape[2]
    Mm, MH = cache_mem_k.shape[2], cache_mem_k.shape[3]
    mw = MH * hd
    GH, dk, dv = state_gla.shape[-3:]
    dkp = -(-dk // LANES) * LANES
    rank = w_gate2_gla.shape[1]
    C = GLA_CHUNK
    assert Ts == C and Tp % C == 0 and M % 16 == 0

    tm = _largest_divisor(M, 512, 16)
    tm_mlp = _largest_divisor(M, 576, 16)
    tf = _largest_divisor(w_up.shape[2], 512, LANES)
    ta = _largest_divisor(Tp, 512, 2 * LANES)
    tg = _largest_divisor(Tp, 512, LANES)
    tq_mem = _largest_divisor(Tp, 512, 16)
    tp_sb = _largest_divisor(past // 2, 256, LANES)
    att_scale = hd ** -0.5
    att_scale2 = att_scale * LOG2E

    x = jnp.concatenate([x_prompt.reshape(Tp, D), x_sample.reshape(Bs * Ts, D)], axis=0)
    mk_p, mv_p = _memkv(mem_prompt.reshape(Mm, D), norm_mem_g, w_mem_kv)

    def split_rows(a):
        return a[:Tp], a[Tp:]

    fox_p, fox_s, sb_p, sb_s, gla_st = [], [], [], [], []
    for i in range(depth):
        kind, j = i % 3, i // 3
        g = norm_mix_g[i]
        if kind in (0, 1):
            w = (w_in_fox if kind == 0 else w_in_sb)[j]
            n_gate = H if kind == 0 else 0
            wq, wk, wv = (w[:, a * aw:(a + 1) * aw].astype(bf16) for a in range(3))
            wqm = w[:, 3 * aw + n_gate:].astype(bf16)
            (q,) = _proj(x, g, wq, [(0, aw, bf16, att_scale2)], tm)
            (k,) = _proj(x, g, wk, [(0, aw, f32, 1.0)], tm)
            (v,) = _proj(x, g, wv, [(0, aw, f32, 1.0)], tm)
            (qm,) = _proj(x, g, wqm, [(0, mw, bf16, att_scale)], tm)
            k_p, k_s = split_rows(k)
            v_p, v_s = split_rows(v)
            if kind == 0:
                wf_t = jnp.pad(w[:, 3 * aw:3 * aw + H].T, ((0, FOX_GATE_ROWS - H), (0, 0))).astype(bf16)
                bf_col = jnp.pad(b_forget[j], (0, FOX_GATE_ROWS - H)).reshape(FOX_GATE_ROWS, 1)
                lf_p, cf_p = _fox_gate_prompt(x, g, wf_t, bf_col, Tp, tg)
                plogf_t = jnp.pad(jnp.swapaxes(cache_fox_logf[j], 1, 2), ((0, 0), (0, FOX_GATE_ROWS - H), (0, 0)))
                lf_s, cf_s = _fox_gate_sample(x, g, wf_t, bf_col, plogf_t, Tp, Ts)
                o = _fox_prompt(q, k, v, cf_p, Tp, H, hd, ta)
                o = _fox_sample(q, k, v, cache_fox_k[j], cache_fox_v[j], cf_s, o, Tp, Ts)
                fox_p.append((k_p, v_p, lf_p[:H].T))
                fox_s.append((k_s, v_s, jnp.swapaxes(lf_s[:, :H], 1, 2)))
                w_out = w_out_fox[j]
            else:
                o = _sb_prompt(q, k, v, Tp, H, hd, ta)
                o = _sb_sample(q, k, v, cache_sb_k[j], cache_sb_v[j], o, Tp, Ts, tp_sb)
                sb_p.append((k_p, v_p))
                sb_s.append((k_s, v_s))
                w_out = w_out_sb[j]
        else:
            w = w_in_gla[j]
            kw, vw = GH * dk, GH * dv

            def pad_heads(a):
                a = a.reshape(a.shape[:-1] + (GH, dk))
                a = jnp.pad(a, [(0, 0)] * (a.ndim - 1) + [(0, dkp - dk)])
                return a.reshape(a.shape[:-2] + (GH * dkp,))

            wqk = jnp.concatenate([pad_heads(w[:, :kw]), pad_heads(w[:, kw:2 * kw])], axis=1).astype(bf16)
            wv = w[:, 2 * kw:2 * kw + vw].astype(bf16)
            wr = w[:, 2 * kw + vw:2 * kw + 2 * vw].astype(bf16)
            c0 = 2 * kw + 2 * vw
            wmg = jnp.concatenate([w[:, c0 + rank:], jnp.pad(w[:, c0:c0 + rank], ((0, 0), (0, LANES - rank)))],
                                  axis=1).astype(bf16)
            q, k = _proj(x, g, wqk, [(0, GH * dkp, f32, dk ** -0.5), (GH * dkp, 2 * GH * dkp, f32, 1.0)], tm)
            (v,) = _proj(x, g, wv, [(0, vw, f32, 1.0)], tm)
            (r,) = _proj(x, g, wr, [(0, vw, f32, 1.0)], tm)
            qm, glow = _proj(x, g, wmg, [(0, mw, bf16, att_scale), (mw, mw + LANES, f32, 1.0)], tm)
            w2 = jnp.pad(pad_heads(w_gate2_gla[j]), ((0, LANES - rank), (0, 0))).astype(bf16)
            bg = pad_heads(b_gate_gla[j]).reshape(1, GH * dkp)
            o, st = _gla(q, k, v, r, glow, w2, bg, b_outgate_gla[j].reshape(1, vw), norm_gla_g[j].reshape(1, vw),
                         state_gla[j], Tp // C, GH, dk, dkp, dv)
            gla_st.append(st)
            w_out = w_out_gla[j]

        om = _mem_attn_prompt(qm, mk_p[i], mv_p[i], Tp, tq_mem, MH, hd)
        om = _mem_attn_sample(qm, cache_mem_k[i].reshape(Bs, Mm, mw), cache_mem_v[i].reshape(Bs, Mm, mw),
                              om, Tp, Ts, MH, hd)
        x = _out_proj(o, om, x, w_out.astype(bf16), tm)
        x = _mlp(x, norm_mlp_g[i], w_up[i].astype(bf16), w_down[i].astype(bf16), norm_final_g, i == depth - 1,
                 tm_mlp, tf)

    def prompt_state(a, tail):
        return a.reshape((Bp, Tp0) + tail)

    def sample_state(a, tail):
        return a.reshape((Bs, Ts) + tail)

    y_p, y_s = split_rows(x)
    stack = lambda sts, n, fn, tail: jnp.stack([fn(st[n], tail) for st in sts])
    gla = jnp.stack(gla_st)
    return (y_p.reshape(Bp, Tp0, D), y_s.reshape(Bs, Ts, D),
            stack(fox_p, 0, prompt_state, (H, hd)), stack(fox_p, 1, prompt_state, (H, hd)),
            stack(fox_p, 2, prompt_state, (H,)),
            stack(sb_p, 0, prompt_state, (H, hd)), stack(sb_p, 1, prompt_state, (H, hd)),
            gla[:, :1],
            mk_p.reshape(depth, Bp, Mm, MH, hd), mv_p.reshape(depth, Bp, Mm, MH, hd),
            stack(fox_s, 0, sample_state, (H, hd)), stack(fox_s, 1, sample_state, (H, hd)),
            jnp.stack([st[2] for st in fox_s]),
            stack(sb_s, 0, sample_state, (H, hd)), stack(sb_s, 1, sample_state, (H, hd)),
            gla[:, 1:])
```

```python
import functools
import math

import jax
import jax.numpy as jnp
from jax import lax
from jax.experimental import pallas as pl
from jax.experimental.pallas import tpu as pltpu

f32 = jnp.float32
bf16 = jnp.bfloat16

RMS_EPS = 1e-6
GLA_GATE_TEMP = 16.0
GLA_CHUNK = 64
GLA_SUB = 8
LOG2E = 1.4426950408889634
MASKED_LOGIT = -1e30
SKIP_BITS = 160.0
LANES = 128
FOX_GATE_ROWS = 16
VMEM_LIMIT_BYTES = 56 * 1024 * 1024


def _params(n_axes):
    return pltpu.CompilerParams(dimension_semantics=("arbitrary",) * n_axes,
                                vmem_limit_bytes=VMEM_LIMIT_BYTES)


def _rms(x, g):
    return x * lax.rsqrt(jnp.mean(x * x, axis=-1, keepdims=True) + RMS_EPS) * g


def _log_sigmoid(z):
    return jnp.minimum(z, 0.0) - jnp.log1p(jnp.exp(-jnp.abs(z)))


def _split3(x):
    hi = x.astype(bf16)
    r = x - hi.astype(f32)
    mid = r.astype(bf16)
    lo = (r - mid.astype(f32)).astype(bf16)
    return hi, mid, lo


def _split2(x):
    hi = x.astype(bf16)
    lo = (x - hi.astype(f32)).astype(bf16)
    return hi, lo


def _tri(n, m, fn):
    r = lax.broadcasted_iota(jnp.int32, (n, m), 0)
    c = lax.broadcasted_iota(jnp.int32, (n, m), 1)
    return jnp.where(fn(r, c), 1.0, 0.0).astype(bf16)


def _dot(a, b):
    return jnp.dot(a, b, preferred_element_type=f32)


def _dot_nt(a, b):
    return lax.dot_general(a, b, (((1,), (1,)), ((), ())), preferred_element_type=f32)


def _dot_tn(a, b):
    return lax.dot_general(a, b, (((0,), (0,)), ((), ())), preferred_element_type=f32)


def _cumsum_lanes(x, carry, tri):
    n = x.shape[1]
    outs = []
    for b0 in range(0, n, LANES):
        w = min(LANES, n - b0)
        hi, mid, lo = _split3(x[:, b0:b0 + w])
        t = tri[:w, :w]
        c = _dot(hi, t) + _dot(mid, t) + _dot(lo, t) + carry
        carry = c[:, w - 1:w]
        outs.append(c)
    return outs, carry


def _proj_body(x_ref, g_ref, w_ref, *o_refs, cols, scales):
    h = _rms(x_ref[...], g_ref[...]).astype(bf16)
    y = _dot(h, w_ref[...])
    for o_ref, (c0, c1), sc in zip(o_refs, cols, scales):
        part = y[:, c0:c1]
        if sc != 1.0:
            part = part * sc
        o_ref[...] = part.astype(o_ref.dtype)


def _proj(x, g, w, outs, tm, w_block=None):
    M, D = x.shape
    if w_block is None:
        w_spec = pl.BlockSpec(w.shape, lambda i: (0, 0))
    else:
        layer, col_block, width = w_block
        w_spec = pl.BlockSpec((None, D, width), lambda i: (layer, 0, col_block))
    cols = tuple((c0, c1) for c0, c1, _, _ in outs)
    scales = tuple(float(s) for _, _, _, s in outs)
    return pl.pallas_call(
        functools.partial(_proj_body, cols=cols, scales=scales),
        grid=(M // tm,),
        in_specs=[pl.BlockSpec((tm, D), lambda i: (i, 0)),
                  pl.BlockSpec((1, D), lambda i: (0, 0)),
                  w_spec],
        out_specs=[pl.BlockSpec((tm, c1 - c0), lambda i: (i, 0)) for c0, c1 in cols],
        out_shape=[jax.ShapeDtypeStruct((M, c1 - c0), dt) for c0, c1, dt, _ in outs],
        compiler_params=_params(1),
        name="norm_proj",
    )(x, g.reshape(1, D), w)


def _proj_state_body(x_ref, g_ref, w_ref, *refs, n_prompt_blocks, n_heads, hd):
    a_ref, sp_ref, ss_ref = refs[-3:]
    i = pl.program_id(0)
    h = _rms(x_ref[...], g_ref[...]).astype(bf16)
    y = _dot(h, w_ref[...])
    a_ref[...] = y.astype(a_ref.dtype)

    def store(ref):
        for hh in range(n_heads):
            ref[:, hh, :] = y[:, hh * hd:(hh + 1) * hd]

    pl.when(i < n_prompt_blocks)(lambda: store(sp_ref))
    pl.when(i >= n_prompt_blocks)(lambda: store(ss_ref))


def _proj_state(x, g, w, w_block, slot, prev, t_prompt, tm, n_heads, hd):
    M, D = x.shape
    layer, col_block, width = w_block
    n_p = t_prompt // tm
    in_specs = [pl.BlockSpec((tm, D), lambda i: (i, 0)),
                pl.BlockSpec((1, D), lambda i: (0, 0)),
                pl.BlockSpec((None, D, width), lambda i: (layer, 0, col_block))]
    args = [x, g.reshape(1, D), w]
    if isinstance(prev, int):
        shapes = [jax.ShapeDtypeStruct((prev, rows, n_heads, hd), f32) for rows in (t_prompt, M - t_prompt)]
        aliases = {}
    else:
        shapes = [jax.ShapeDtypeStruct(p.shape, p.dtype) for p in prev]
        in_specs += [pl.BlockSpec(memory_space=pl.ANY)] * 2
        args += list(prev)
        aliases = {3: 1, 4: 2}
    return pl.pallas_call(
        functools.partial(_proj_state_body, n_prompt_blocks=n_p, n_heads=n_heads, hd=hd),
        grid=(M // tm,),
        in_specs=in_specs,
        out_specs=[pl.BlockSpec((tm, width), lambda i: (i, 0)),
                   pl.BlockSpec((None, tm, n_heads, hd), lambda i: (slot, jnp.minimum(i, n_p - 1), 0, 0)),
                   pl.BlockSpec((None, tm, n_heads, hd), lambda i: (slot, jnp.maximum(i - n_p, 0), 0, 0))],
        out_shape=[jax.ShapeDtypeStruct((M, width), bf16)] + shapes,
        input_output_aliases=aliases,
        compiler_params=_params(1),
        name="norm_proj_state",
    )(*args)


def _memkv_body(m_ref, g_ref, w_ref, k_ref, v_ref, *, mw):
    h = _rms(m_ref[...], g_ref[...]).astype(bf16)
    w = w_ref[...].astype(bf16)
    k_ref[...] = _dot(h, w[:, :mw])
    v_ref[...] = _dot(h, w[:, mw:])


def _memkv(mem, g, w):
    L, D, two_mw = w.shape
    Mm = mem.shape[0]
    mw = two_mw // 2
    return pl.pallas_call(
        functools.partial(_memkv_body, mw=mw),
        grid=(L,),
        in_specs=[pl.BlockSpec((Mm, D), lambda l: (0, 0)),
                  pl.BlockSpec((None, 1, D), lambda l: (l, 0, 0)),
                  pl.BlockSpec((None, D, two_mw), lambda l: (l, 0, 0))],
        out_specs=[pl.BlockSpec((None, Mm, mw), lambda l: (l, 0, 0))] * 2,
        out_shape=[jax.ShapeDtypeStruct((L, Mm, mw), f32)] * 2,
        compiler_params=_params(1),
        name="mem_kv",
    )(mem, g.reshape(L, 1, D), w)


def _gate_logits(x_ref, g_ref, wf_ref, bf_ref):
    h = _rms(x_ref[...], g_ref[...]).astype(bf16)
    return _log_sigmoid(_dot_nt(wf_ref[...], h) + bf_ref[...])


def _fox_gate_prompt_body(x_ref, g_ref, wf_ref, bf_ref, lf_ref, cf_ref, carry_ref):
    @pl.when(pl.program_id(0) == 0)
    def _():
        carry_ref[...] = jnp.zeros_like(carry_ref)

    logf = _gate_logits(x_ref, g_ref, wf_ref, bf_ref)
    lf_ref[...] = logf
    tri = _tri(LANES, LANES, lambda r, c: r <= c)
    outs, carry = _cumsum_lanes(logf, carry_ref[:, 0:1], tri)
    for b, c in enumerate(outs):
        cf_ref[:, b * LANES:(b + 1) * LANES] = c * LOG2E
    carry_ref[...] = jnp.broadcast_to(carry, carry_ref.shape)


def _fox_gate_prompt(x, g, wf_t, bf, t_prompt, tg):
    D = x.shape[1]
    R = wf_t.shape[0]
    return pl.pallas_call(
        _fox_gate_prompt_body,
        grid=(t_prompt // tg,),
        in_specs=[pl.BlockSpec((tg, D), lambda i: (i, 0)),
                  pl.BlockSpec((1, D), lambda i: (0, 0)),
                  pl.BlockSpec((R, D), lambda i: (0, 0)),
                  pl.BlockSpec((R, 1), lambda i: (0, 0))],
        out_specs=[pl.BlockSpec((R, tg), lambda i: (0, i))] * 2,
        out_shape=[jax.ShapeDtypeStruct((R, t_prompt), f32)] * 2,
        scratch_shapes=[pltpu.VMEM((R, LANES), f32)],
        compiler_params=_params(1),
        name="fox_gate_prompt",
    )(x, g.reshape(1, D), wf_t, bf)


def _fox_gate_sample_body(x_ref, g_ref, wf_ref, bf_ref, pl_ref, lf_ref, cf_ref, *, past, ts):
    logf = _gate_logits(x_ref, g_ref, wf_ref, bf_ref)
    lf_ref[...] = logf
    tri = _tri(LANES, LANES, lambda r, c: r <= c)
    zero = jnp.zeros((logf.shape[0], 1), f32)
    outs, carry = _cumsum_lanes(pl_ref[...], zero, tri)
    for b, c in enumerate(outs):
        cf_ref[:, b * LANES:(b + 1) * LANES] = c * LOG2E
    (new,), _ = _cumsum_lanes(logf, carry, tri)
    cf_ref[:, past:past + ts] = new * LOG2E
    cf_ref[:, past + ts:] = jnp.zeros((logf.shape[0], LANES - ts), f32)


def _fox_gate_sample(x, g, wf_t, bf, plogf_t, row0, ts):
    D = x.shape[1]
    R = wf_t.shape[0]
    B, _, past = plogf_t.shape
    blk0 = row0 // ts
    return pl.pallas_call(
        functools.partial(_fox_gate_sample_body, past=past, ts=ts),
        grid=(B,),
        in_specs=[pl.BlockSpec((ts, D), lambda b: (blk0 + b, 0)),
                  pl.BlockSpec((1, D), lambda b: (0, 0)),
                  pl.BlockSpec((R, D), lambda b: (0, 0)),
                  pl.BlockSpec((R, 1), lambda b: (0, 0)),
                  pl.BlockSpec((None, R, past), lambda b: (b, 0, 0))],
        out_specs=[pl.BlockSpec((None, R, ts), lambda b: (b, 0, 0)),
                   pl.BlockSpec((None, R, past + LANES), lambda b: (b, 0, 0))],
        out_shape=[jax.ShapeDtypeStruct((B, R, ts), f32),
                   jax.ShapeDtypeStruct((B, R, past + LANES), f32)],
        compiler_params=_params(1),
        name="fox_gate_sample",
    )(x, g.reshape(1, D), wf_t, bf, plogf_t)


def _softmax_tile(s, m, l, acc, v):
    m_new = jnp.maximum(m, jnp.max(s, axis=-1, keepdims=True))
    alpha = jnp.exp2(m - m_new)
    p = jnp.exp2(s - m_new)
    l = alpha * l + jnp.sum(p, axis=-1, keepdims=True)
    acc = alpha * acc + _dot(p.astype(bf16), v)
    return m_new, l, acc


def _causal(n, strict):
    r = lax.broadcasted_iota(jnp.int32, (n, n), 0)
    c = lax.broadcasted_iota(jnp.int32, (n, n), 1)
    return c < r if strict else c <= r


def _fox_prompt_body(q_ref, k16, v16, f_ref, o_ref, knorm_ref, *, t):
    qi = pl.program_id(1)

    @pl.when(qi == 0)
    def _():
        kf = k16[...].astype(f32)
        k_sq = jnp.max(jnp.sum(kf * kf, axis=-1, keepdims=True), axis=0, keepdims=True)
        knorm_ref[...] = jnp.broadcast_to(jnp.sqrt(k_sq), knorm_ref.shape)

    q = q_ref[...]
    qf = q.astype(f32)
    qk_bound = jnp.sqrt(jnp.sum(qf * qf, axis=-1, keepdims=True)) * knorm_ref[0:1, 0:1]
    col_minus_row = (lax.broadcasted_iota(jnp.int32, (t, t), 1) - lax.broadcasted_iota(jnp.int32, (t, t), 0))

    def tile(kb, carry, max_col_minus_row):
        rows = pl.ds(pl.multiple_of(kb * t, t), t)
        s = _dot_nt(q, k16[rows, :]) - f_ref[pl.ds(kb, 1), :]
        if max_col_minus_row is not None:
            s = jnp.where(col_minus_row <= max_col_minus_row, s, MASKED_LOGIT)
        return _softmax_tile(s, *carry, v16[rows, :])

    def rest_is_zero(a, b, kb):
        bias_max = jnp.max(-f_ref[pl.ds(kb, 1), :])
        return (jnp.max(qk_bound + bias_max - jnp.maximum(a[0], b[0])) < -SKIP_BITS).astype(jnp.int32)

    init = (jnp.full((t, 1), MASKED_LOGIT, f32), jnp.zeros((t, 1), f32), jnp.zeros((t, q.shape[1]), f32))
    odd = qi % 2
    a = tile(qi, init, 0)
    b = tile(jnp.maximum(qi - 1, 0), init, jnp.where(odd == 1, t, -t))
    first = qi - 1 - odd
    n_pairs = qi // 2

    def pair(c):
        i, a, b, _ = c
        kb = first - 2 * i
        a = tile(kb, a, None)
        b = tile(kb - 1, b, None)
        return i + 1, a, b, rest_is_zero(a, b, jnp.maximum(kb - 2, 0))

    _, (m_a, l_a, acc_a), (m_b, l_b, acc_b), _ = lax.while_loop(
        lambda c: jnp.logical_and(c[0] < n_pairs, c[3] == 0), pair,
        (jnp.int32(0), a, b, rest_is_zero(a, b, jnp.maximum(first, 0))))
    m = jnp.maximum(m_a, m_b)
    w_a = jnp.exp2(m_a - m)
    w_b = jnp.exp2(m_b - m)
    o_ref[...] = ((w_a * acc_a + w_b * acc_b) / (w_a * l_a + w_b * l_b)).astype(o_ref.dtype)


def _fox_prompt(q, k, v, cf, t_prompt, n_heads, hd, t):
    M = q.shape[0]
    nq = t_prompt // t
    return pl.pallas_call(
        functools.partial(_fox_prompt_body, t=t),
        grid=(n_heads, nq),
        in_specs=[pl.BlockSpec((t, hd), lambda h, i: (i, h)),
                  pl.BlockSpec((t_prompt, hd), lambda h, i: (0, h)),
                  pl.BlockSpec((t_prompt, hd), lambda h, i: (0, h)),
                  pl.BlockSpec((None, nq, t), lambda h, i: (h, 0, 0))],
        out_specs=pl.BlockSpec((t, hd), lambda h, i: (i, h)),
        out_shape=jax.ShapeDtypeStruct((M, n_heads * hd), bf16),
        scratch_shapes=[pltpu.VMEM((8, LANES), f32)],
        compiler_params=_params(2),
        name="fox_prompt",
    )(q, k, v, cf.reshape(cf.shape[0], nq, t))


def _fox_sample_body(q_ref, k_ref, v_ref, pk_ref, pv_ref, f_ref, _, o_ref, *, past, ts, n_heads, hd):
    for h in range(n_heads):
        cols = slice(h * hd, (h + 1) * hd)
        q = q_ref[:, cols]
        f = f_ref[h:h + 1, :]
        s_past = _dot_nt(q, pk_ref[:, h, :].astype(bf16)) - f[:, :past]
        s_new = _dot_nt(q, k_ref[:, cols].astype(bf16)) - f[:, past:past + ts]
        s_new = jnp.where(_causal(ts, strict=False), s_new, -jnp.inf)
        m = jnp.maximum(jnp.max(s_past, axis=-1, keepdims=True), jnp.max(s_new, axis=-1, keepdims=True))
        p_past = jnp.exp2(s_past - m)
        p_new = jnp.exp2(s_new - m)
        l = jnp.sum(p_past, axis=-1, keepdims=True) + jnp.sum(p_new, axis=-1, keepdims=True)
        acc = (_dot(p_past.astype(bf16), pv_ref[:, h, :].astype(bf16))
               + _dot(p_new.astype(bf16), v_ref[:, cols].astype(bf16)))
        o_ref[:, cols] = (acc / l).astype(o_ref.dtype)


def _sample_attn_call(body, name, q, k, v, pk, pv, layer, extra, o_buf, row0, ts):
    _, B, past, n_heads, hd = pk.shape
    blk0 = row0 // ts
    new_spec = pl.BlockSpec((ts, n_heads * hd), lambda b: (blk0 + b, 0))
    past_spec = pl.BlockSpec((None, None, past, n_heads, hd), lambda b: (layer, b, 0, 0, 0))
    in_specs = [new_spec, new_spec, new_spec, past_spec, past_spec]
    args = [q, k, v, pk, pv]
    if extra is not None:
        in_specs.append(pl.BlockSpec((None,) + extra.shape[1:], lambda b: (b, 0, 0)))
        args.append(extra)
    in_specs.append(pl.BlockSpec(memory_space=pl.ANY))
    args.append(o_buf)
    return pl.pallas_call(
        functools.partial(body, past=past, ts=ts, n_heads=n_heads, hd=hd),
        grid=(B,),
        in_specs=in_specs,
        out_specs=new_spec,
        out_shape=jax.ShapeDtypeStruct(o_buf.shape, o_buf.dtype),
        input_output_aliases={len(args) - 1: 0},
        compiler_params=_params(1),
        name=name,
    )(*args)


def _fox_sample(q, k, v, pk, pv, layer, cf, o_buf, row0, ts):
    return _sample_attn_call(_fox_sample_body, "fox_sample", q, k, v, pk, pv, layer, cf, o_buf, row0, ts)


def _tri2(n):
    tri = _tri(n, n, lambda r, c: r >= c)
    return jnp.concatenate([tri, tri], axis=0)


def _sb_scores(q, k, tri2, valid):
    z = _dot_nt(q, k)
    sp = jnp.maximum(z, 0.0) + jnp.log(1.0 + jnp.exp2(-jnp.abs(z))) * LOG2E
    if valid is not None:
        sp = jnp.where(valid, sp, 0.0)
    cum = _dot(jnp.concatenate(_split2(sp), axis=1), tri2)
    return z - cum, cum[:, 0:1]


def _sb_out(zc, run, v, valid):
    a = jnp.exp2(zc)
    if valid is not None:
        a = jnp.where(valid, a, 0.0)
    return jnp.exp2(-run) * _dot(a.astype(bf16), v)


def _sb_pair(q, right, left, tri2, run, acc, valid_right=None, valid_left=None):
    (k_r, v_r), (k_l, v_l) = right, left
    zc_r, tot_r = _sb_scores(q, k_r, tri2, valid_right)
    zc_l, tot_l = _sb_scores(q, k_l, tri2, valid_left)
    acc = acc + _sb_out(zc_r, run, v_r, valid_right) + _sb_out(zc_l, run + tot_r, v_l, valid_left)
    return run + tot_r + tot_l, acc


def _sb_prompt_body(q_ref, k16, v16, o_ref, tri2_ref, *, tq, tk):
    qi = pl.program_id(1)

    @pl.when(qi == 0)
    def _():
        tri2_ref[...] = _tri2(tk)

    q = q_ref[...]
    tri2 = tri2_ref[...]

    def kv(kb):
        rows = pl.ds(pl.multiple_of(kb * tk, tk), tk)
        return k16[rows, :], v16[rows, :]

    r = lax.broadcasted_iota(jnp.int32, (tq, tk), 0)
    c = lax.broadcasted_iota(jnp.int32, (tq, tk), 1)
    init = (jnp.zeros((tq, 1), f32), jnp.zeros((tq, q.shape[1]), f32))
    carry = _sb_pair(q, kv(2 * qi + 1), kv(2 * qi), tri2, *init, valid_right=c + tk < r, valid_left=c < r)

    def all_rows_spent(run):
        return (jnp.min(run) >= SKIP_BITS).astype(jnp.int32)

    def below(c):
        i, run, acc, _ = c
        kb = 2 * (qi - 1 - i)
        run, acc = _sb_pair(q, kv(kb + 1), kv(kb), tri2, run, acc)
        return i + 1, run, acc, all_rows_spent(run)

    _, _, acc, _ = lax.while_loop(lambda c: jnp.logical_and(c[0] < qi, c[3] == 0), below,
                                  (jnp.int32(0), *carry, all_rows_spent(carry[0])))
    o_ref[...] = acc.astype(o_ref.dtype)


def _sb_prompt(q, k, v, t_prompt, n_heads, hd, tq):
    M = q.shape[0]
    tk = tq // 2
    return pl.pallas_call(
        functools.partial(_sb_prompt_body, tq=tq, tk=tk),
        grid=(n_heads, t_prompt // tq),
        in_specs=[pl.BlockSpec((tq, hd), lambda h, i: (i, h)),
                  pl.BlockSpec((t_prompt, hd), lambda h, i: (0, h)),
                  pl.BlockSpec((t_prompt, hd), lambda h, i: (0, h))],
        out_specs=pl.BlockSpec((tq, hd), lambda h, i: (i, h)),
        out_shape=jax.ShapeDtypeStruct((M, n_heads * hd), bf16),
        scratch_shapes=[pltpu.VMEM((2 * tk, tk), bf16)],
        compiler_params=_params(2),
        name="sb_prompt",
    )(q, k, v)


def _sb_sample_body(q_ref, k_ref, v_ref, pk_ref, pv_ref, _, o_ref, *, past, ts, n_heads, hd, tp):
    tri2_new = _tri2(ts)
    tri2 = _tri2(tp)
    strict = _causal(ts, strict=True)
    for h in range(n_heads):
        cols = slice(h * hd, (h + 1) * hd)
        q = q_ref[:, cols]
        zc, run = _sb_scores(q, k_ref[:, cols].astype(bf16), tri2_new, strict)
        acc = _sb_out(zc, jnp.zeros((ts, 1), f32), v_ref[:, cols].astype(bf16), strict)

        def past_kv(p0):
            return pk_ref[p0:p0 + tp, h, :].astype(bf16), pv_ref[p0:p0 + tp, h, :].astype(bf16)

        for p0 in range(past - tp, -1, -2 * tp):
            run, acc = _sb_pair(q, past_kv(p0), past_kv(p0 - tp), tri2, run, acc)
        o_ref[:, cols] = acc.astype(o_ref.dtype)


def _sb_sample(q, k, v, pk, pv, layer, o_buf, row0, ts, tp):
    return _sample_attn_call(functools.partial(_sb_sample_body, tp=tp), "sb_sample",
                             q, k, v, pk, pv, layer, None, o_buf, row0, ts)


def _mem_attn_body(q_ref, k_ref, v_ref, *rest, n_heads, hd):
    o_ref = rest[-1]
    for h in range(n_heads):
        sl = slice(h * hd, (h + 1) * hd)
        head = (lambda ref: ref[:, h, :]) if len(k_ref.shape) == 3 else (lambda ref: ref[:, sl])
        s = _dot_nt(q_ref[:, sl], head(k_ref).astype(bf16))
        p = jnp.exp(s - jnp.max(s, axis=-1, keepdims=True))
        acc = _dot(p.astype(bf16), head(v_ref).astype(bf16))
        o_ref[:, sl] = (acc / jnp.sum(p, axis=-1, keepdims=True)).astype(o_ref.dtype)


def _mem_attn_prompt(qm, mk, mv, t_prompt, tq, n_heads, hd):
    M, W = qm.shape
    Mm = mk.shape[0]
    return pl.pallas_call(
        functools.partial(_mem_attn_body, n_heads=n_heads, hd=hd),
        grid=(t_prompt // tq,),
        in_specs=[pl.BlockSpec((tq, W), lambda i: (i, 0)),
                  pl.BlockSpec((Mm, W), lambda i: (0, 0)),
                  pl.BlockSpec((Mm, W), lambda i: (0, 0))],
        out_specs=pl.BlockSpec((tq, W), lambda i: (i, 0)),
        out_shape=jax.ShapeDtypeStruct((M, W), bf16),
        compiler_params=_params(1),
        name="mem_attn_prompt",
    )(qm, mk, mv)


def _mem_attn_sample(qm, mk, mv, layer, o_buf, row0, ts):
    M, W = qm.shape
    _, B, Mm, n_heads, hd = mk.shape
    blk0 = row0 // ts
    row_spec = pl.BlockSpec((ts, W), lambda b: (blk0 + b, 0))
    mem_spec = pl.BlockSpec((None, None, Mm, n_heads, hd), lambda b: (layer, b, 0, 0, 0))
    return pl.pallas_call(
        functools.partial(_mem_attn_body, n_heads=n_heads, hd=hd),
        grid=(B,),
        in_specs=[row_spec, mem_spec, mem_spec, pl.BlockSpec(memory_space=pl.ANY)],
        out_specs=row_spec,
        out_shape=jax.ShapeDtypeStruct((M, W), bf16),
        input_output_aliases={3: 0},
        compiler_params=_params(1),
        name="mem_attn_sample",
    )(qm, mk, mv, o_buf)


def _gla_body(q_ref, k_ref, v_ref, r_ref, gl_ref, w2_ref, bg_ref, br_ref, ng_ref, s0_ref,
              o_ref, so_ref, S, *, n_prompt_chunks, n_heads, dk, dkp, dv):
    c = pl.program_id(0)
    C, U = GLA_CHUNK, GLA_SUB

    @pl.when(c == 0)
    def _():
        S[...] = jnp.zeros_like(S)

    @pl.when(c >= n_prompt_chunks)
    def _():
        for h in range(n_heads):
            S[h, :dk, :] = s0_ref[h]
            S[h, dk:, :] = jnp.zeros((dkp - dk, dv), f32)

    log_a = _log_sigmoid(_dot(gl_ref[...].astype(bf16), w2_ref[...]) + bg_ref[...]) * (1.0 / GLA_GATE_TEMP)
    tril = _tri(C, C, lambda r, cc: r >= cc)
    row_in_sub = lax.broadcasted_iota(jnp.int32, (U, 1), 0)

    for h in range(n_heads):
        ks = slice(h * dkp, (h + 1) * dkp)
        vs = slice(h * dv, (h + 1) * dv)
        hi, mid, lo = _split3(log_a[:, ks])
        G = _dot(tril, hi) + _dot(tril, mid) + _dot(tril, lo)
        q = q_ref[:, ks]
        k = k_ref[:, ks]
        v = v_ref[:, vs]
        v16 = v.astype(bf16)
        S_h = S[h]
        o_inter = _dot((q * jnp.exp(G)).astype(bf16), S_h.astype(bf16))

        o_rows = []
        for b in range(C // U):
            r0 = b * U
            Gb, qb = G[r0:r0 + U], q[r0:r0 + U]
            o_b = o_inter[r0:r0 + U]
            if b > 0:
                ref = G[r0 - 1:r0]
                qg = qb * jnp.exp(Gb - ref)
                kg = k[:r0] * jnp.exp(ref - G[:r0])
                a = _dot_nt(qg.astype(bf16), kg.astype(bf16))
                o_b = o_b + _dot(a.astype(bf16), v16[:r0])
            for j in range(U):
                s = r0 + j
                e = jnp.exp(jnp.minimum(Gb - G[s:s + 1], 0.0))
                a = jnp.sum(qb * e * k[s:s + 1], axis=-1, keepdims=True)
                a = jnp.where(row_in_sub >= j, a, 0.0)
                o_b = o_b + a * v[s:s + 1]
            o_rows.append(o_b)
        o = jnp.concatenate(o_rows, axis=0)

        g_last = G[C - 1:C]
        kl = k * jnp.exp(g_last - G)
        decay_col = jnp.exp(G.T[:, C - 1:C])
        S_new = decay_col * S_h + _dot_tn(kl.astype(bf16), v16)
        S[h] = S_new
        so_ref[h] = S_new[:dk]

        on = o * lax.rsqrt(jnp.mean(o * o, axis=-1, keepdims=True) + RMS_EPS) * ng_ref[:, vs]
        x = r_ref[:, vs] + br_ref[:, vs]
        o_ref[:, vs] = (on * (x / (1.0 + jnp.exp(-x)))).astype(o_ref.dtype)


def _gla(q, k, v, r, glow, w2, bg, br, ng, s0, n_prompt_chunks, n_heads, dk, dkp, dv):
    M = q.shape[0]
    C = GLA_CHUNK
    n_chunks = M // C
    B = s0.shape[0]
    row = lambda width: pl.BlockSpec((C, width), lambda c: (c, 0))
    const = lambda a: pl.BlockSpec(a.shape, lambda c: (0,) * a.ndim)
    state_in = pl.BlockSpec((None, n_heads, dk, dv), lambda c: (jnp.maximum(c - n_prompt_chunks, 0), 0, 0, 0))
    state_out = pl.BlockSpec((None, n_heads, dk, dv),
                             lambda c: (jnp.maximum(c - (n_prompt_chunks - 1), 0), 0, 0, 0))
    return pl.pallas_call(
        functools.partial(_gla_body, n_prompt_chunks=n_prompt_chunks, n_heads=n_heads, dk=dk, dkp=dkp, dv=dv),
        grid=(n_chunks,),
        in_specs=[row(n_heads * dkp), row(n_heads * dkp), row(n_heads * dv), row(n_heads * dv),
                  row(glow.shape[1]), const(w2), const(bg), const(br), const(ng), state_in],
        out_specs=[row(n_heads * dv), state_out],
        out_shape=[jax.ShapeDtypeStruct((M, n_heads * dv), bf16),
                   jax.ShapeDtypeStruct((B + 1, n_heads, dk, dv), f32)],
        scratch_shapes=[pltpu.VMEM((n_heads, dkp, dv), f32)],
        compiler_params=_params(1),
        name="gla",
    )(q, k, v, r, glow, w2, bg, br, ng, s0)


def _out_proj_body(o_ref, om_ref, x_ref, w_ref, y_ref, *, wo):
    y_ref[...] = x_ref[...] + _dot(o_ref[...], w_ref[:wo, :]) + _dot(om_ref[...], w_ref[wo:, :])


def _out_proj(o, om, x, w, tm):
    M, D = x.shape
    wo, wm = o.shape[1], om.shape[1]
    return pl.pallas_call(
        functools.partial(_out_proj_body, wo=wo),
        grid=(M // tm,),
        in_specs=[pl.BlockSpec((tm, wo), lambda i: (i, 0)),
                  pl.BlockSpec((tm, wm), lambda i: (i, 0)),
                  pl.BlockSpec((tm, D), lambda i: (i, 0)),
                  pl.BlockSpec((wo + wm, D), lambda i: (0, 0))],
        out_specs=pl.BlockSpec((tm, D), lambda i: (i, 0)),
        out_shape=jax.ShapeDtypeStruct((M, D), f32),
        compiler_params=_params(1),
        name="out_proj",
    )(o, om, x, w)


def _mlp_body(x_ref, g_ref, wu_ref, wd_ref, gf_ref, y_ref, h_ref, *, final_norm):
    j = pl.program_id(1)

    @pl.when(j == 0)
    def _():
        x = x_ref[...]
        h_ref[...] = _rms(x, g_ref[...]).astype(bf16)
        y_ref[...] = x

    u = jnp.maximum(_dot(h_ref[...], wu_ref[...]), 0.0)
    y_ref[...] += _dot((u * u).astype(bf16), wd_ref[...])

    if final_norm:
        @pl.when(j == pl.num_programs(1) - 1)
        def _():
            y_ref[...] = _rms(y_ref[...], gf_ref[...])


def _mlp(x, g, wu, wd, layer, gf, final_norm, tm, tf):
    M, D = x.shape
    F = wu.shape[2]
    return pl.pallas_call(
        functools.partial(_mlp_body, final_norm=final_norm),
        grid=(M // tm, F // tf),
        in_specs=[pl.BlockSpec((tm, D), lambda i, j: (i, 0)),
                  pl.BlockSpec((1, D), lambda i, j: (0, 0)),
                  pl.BlockSpec((None, D, tf), lambda i, j: (layer, 0, j)),
                  pl.BlockSpec((None, tf, D), lambda i, j: (layer, j, 0)),
                  pl.BlockSpec((1, D), lambda i, j: (0, 0))],
        out_specs=pl.BlockSpec((tm, D), lambda i, j: (i, 0)),
        out_shape=jax.ShapeDtypeStruct((M, D), f32),
        scratch_shapes=[pltpu.VMEM((tm, D), bf16)],
        compiler_params=_params(2),
        name="mlp",
    )(x, g.reshape(1, D), wu, wd, gf.reshape(1, D))


def _largest_divisor(n, cap, mult):
    best = None
    for d in range(mult, min(n, cap) + 1, mult):
        if n % d == 0:
            best = d
    assert best is not None, (n, cap, mult)
    return best


def kernel(x_prompt, x_sample, cache_fox_k, cache_fox_v, cache_fox_logf, cache_sb_k, cache_sb_v, state_gla,
           cache_mem_k, cache_mem_v, mem_prompt, norm_mix_g, norm_mlp_g, norm_mem_g, norm_final_g, w_mem_kv,
           w_in_fox, b_forget, w_out_fox, w_in_sb, w_out_sb, w_in_gla, w_gate2_gla, b_gate_gla, b_outgate_gla,
           norm_gla_g, w_out_gla, w_up, w_down):
    Bp, Tp0, D = x_prompt.shape
    Bs, Ts, _ = x_sample.shape
    assert Bp == 1, "the prompt group is handled as one sequence"
    Tp = Bp * Tp0
    M = Tp + Bs * Ts
    depth = norm_mix_g.shape[0]
    H, hd = cache_fox_k.shape[-2:]
    aw = H * hd
    past = cache_fox_k.shape[2]
    Mm, MH = cache_mem_k.shape[2], cache_mem_k.shape[3]
    mw = MH * hd
    GH, dk, dv = state_gla.shape[-3:]
    dkp = -(-dk // LANES) * LANES
    rank = w_gate2_gla.shape[1]
    C = GLA_CHUNK
    assert Ts == C and Tp % C == 0 and M % 16 == 0

    tm = _largest_divisor(M, 512, 16)
    tm_st = _largest_divisor(math.gcd(Tp, M - Tp), 512, 16)
    tm_mlp = _largest_divisor(M, 576, 16)
    tf = _largest_divisor(w_up.shape[2], 512, LANES)
    ta = _largest_divisor(Tp, 512, 2 * LANES)
    tg = _largest_divisor(Tp, 512, LANES)
    tq_mem = _largest_divisor(Tp, 512, 16)
    tp_sb = _largest_divisor(past // 2, 256, LANES)
    att_scale = hd ** -0.5
    att_scale2 = att_scale * LOG2E

    x = jnp.concatenate([x_prompt.reshape(Tp, D), x_sample.reshape(Bs * Ts, D)], axis=0)
    mk_p, mv_p = _memkv(mem_prompt.reshape(Mm, D), norm_mem_g, w_mem_kv)
    wu16, wd16 = w_up.astype(bf16), w_down.astype(bf16)
    w16_fox, w16_sb = w_in_fox.astype(bf16), w_in_sb.astype(bf16)

    n_fox, n_sb = w_in_fox.shape[0], w_in_sb.shape[0]
    kv_state = {0: None, 1: None}
    logf_p, logf_s, gla_st = [], [], []
    for i in range(depth):
        kind, j = i % 3, i // 3
        g = norm_mix_g[i]
        if kind in (0, 1):
            w = (w_in_fox if kind == 0 else w_in_sb)[j]
            w16 = w16_fox if kind == 0 else w16_sb
            n_gate = H if kind == 0 else 0
            (q,) = _proj(x, g, w16, [(0, aw, bf16, att_scale2)], tm, w_block=(j, 0, aw))
            n_layers = n_fox if kind == 0 else n_sb
            st = kv_state[kind]
            k, *st_k = _proj_state(x, g, w16, (j, 1, aw), j, st[0] if st else n_layers, Tp, tm_st, H, hd)
            v, *st_v = _proj_state(x, g, w16, (j, 2, aw), j, st[1] if st else n_layers, Tp, tm_st, H, hd)
            kv_state[kind] = (st_k, st_v)
            (qm,) = _proj(x, g, w[:, 3 * aw + n_gate:].astype(bf16), [(0, mw, bf16, att_scale)], tm)
            if kind == 0:
                wf_t = jnp.pad(w[:, 3 * aw:3 * aw + H].T, ((0, FOX_GATE_ROWS - H), (0, 0))).astype(bf16)
                bf_col = jnp.pad(b_forget[j], (0, FOX_GATE_ROWS - H)).reshape(FOX_GATE_ROWS, 1)
                lf_p, cf_p = _fox_gate_prompt(x, g, wf_t, bf_col, Tp, tg)
                plogf_t = jnp.pad(jnp.swapaxes(cache_fox_logf[j], 1, 2), ((0, 0), (0, FOX_GATE_ROWS - H), (0, 0)))
                lf_s, cf_s = _fox_gate_sample(x, g, wf_t, bf_col, plogf_t, Tp, Ts)
                o = _fox_prompt(q, k, v, cf_p, Tp, H, hd, ta)
                o = _fox_sample(q, k, v, cache_fox_k, cache_fox_v, j, cf_s, o, Tp, Ts)
                logf_p.append(lf_p[:H].T.reshape(Bp, Tp0, H))
                logf_s.append(jnp.swapaxes(lf_s[:, :H], 1, 2))
                w_out = w_out_fox[j]
            else:
                o = _sb_prompt(q, k, v, Tp, H, hd, ta)
                o = _sb_sample(q, k, v, cache_sb_k, cache_sb_v, j, o, Tp, Ts, tp_sb)
                w_out = w_out_sb[j]
        else:
            w = w_in_gla[j]
            kw, vw = GH * dk, GH * dv

            def pad_heads(a):
                a = a.reshape(a.shape[:-1] + (GH, dk))
                a = jnp.pad(a, [(0, 0)] * (a.ndim - 1) + [(0, dkp - dk)])
                return a.reshape(a.shape[:-2] + (GH * dkp,))

            wqk = jnp.concatenate([pad_heads(w[:, :kw]), pad_heads(w[:, kw:2 * kw])], axis=1).astype(bf16)
            wv = w[:, 2 * kw:2 * kw + vw].astype(bf16)
            wr = w[:, 2 * kw + vw:2 * kw + 2 * vw].astype(bf16)
            c0 = 2 * kw + 2 * vw
            wmg = jnp.concatenate([w[:, c0 + rank:], jnp.pad(w[:, c0:c0 + rank], ((0, 0), (0, LANES - rank)))],
                                  axis=1).astype(bf16)
            q, k = _proj(x, g, wqk, [(0, GH * dkp, f32, dk ** -0.5), (GH * dkp, 2 * GH * dkp, f32, 1.0)], tm)
            (v,) = _proj(x, g, wv, [(0, vw, f32, 1.0)], tm)
            (r,) = _proj(x, g, wr, [(0, vw, f32, 1.0)], tm)
            qm, glow = _proj(x, g, wmg, [(0, mw, bf16, att_scale), (mw, mw + LANES, f32, 1.0)], tm)
            w2 = jnp.pad(pad_heads(w_gate2_gla[j]), ((0, LANES - rank), (0, 0))).astype(bf16)
            bg = pad_heads(b_gate_gla[j]).reshape(1, GH * dkp)
            o, st = _gla(q, k, v, r, glow, w2, bg, b_outgate_gla[j].reshape(1, vw), norm_gla_g[j].reshape(1, vw),
                         state_gla[j], Tp // C, GH, dk, dkp, dv)
            gla_st.append(st)
            w_out = w_out_gla[j]

        om = _mem_attn_prompt(qm, mk_p[i], mv_p[i], Tp, tq_mem, MH, hd)
        om = _mem_attn_sample(qm, cache_mem_k, cache_mem_v, i, om, Tp, Ts)
        x = _out_proj(o, om, x, w_out.astype(bf16), tm)
        x = _mlp(x, norm_mlp_g[i], wu16, wd16, i, norm_final_g, i == depth - 1, tm_mlp, tf)

    def prompt_state(a):
        return a.reshape((a.shape[0], Bp, Tp0, H, hd))

    def sample_state(a):
        return a.reshape((a.shape[0], Bs, Ts, H, hd))

    (fox_k, fox_v), (sb_k, sb_v) = kv_state[0], kv_state[1]
    gla = jnp.stack(gla_st)
    return (x[:Tp].reshape(Bp, Tp0, D), x[Tp:].reshape(Bs, Ts, D),
            prompt_state(fox_k[0]), prompt_state(fox_v[0]), jnp.stack(logf_p),
            prompt_state(sb_k[0]), prompt_state(sb_v[0]),
            gla[:, :1],
            mk_p.reshape(depth, Bp, Mm, MH, hd), mv_p.reshape(depth, Bp, Mm, MH, hd),
            sample_state(fox_k[1]), sample_state(fox_v[1]), jnp.stack(logf_s),
            sample_state(sb_k[1]), sample_state(sb_v[1]),
            gla[:, 1:])
```

```python
import functools
import math

import jax
import jax.numpy as jnp
from jax import lax
from jax.experimental import pallas as pl
from jax.experimental.pallas import tpu as pltpu

f32 = jnp.float32
bf16 = jnp.bfloat16

RMS_EPS = 1e-6
GLA_GATE_TEMP = 16.0
GLA_CHUNK = 64
GLA_SUB = 8
LOG2E = 1.4426950408889634
MASKED_LOGIT = -1e30
SKIP_BITS = 160.0
LANES = 128
FOX_GATE_ROWS = 16
VMEM_LIMIT_BYTES = 56 * 1024 * 1024


def _params(n_axes):
    return pltpu.CompilerParams(dimension_semantics=("arbitrary",) * n_axes,
                                vmem_limit_bytes=VMEM_LIMIT_BYTES)


def _rms(x, g):
    return x * lax.rsqrt(jnp.mean(x * x, axis=-1, keepdims=True) + RMS_EPS) * g


def _log_sigmoid(z):
    return jnp.minimum(z, 0.0) - jnp.log1p(jnp.exp(-jnp.abs(z)))


def _split3(x):
    hi = x.astype(bf16)
    r = x - hi.astype(f32)
    mid = r.astype(bf16)
    lo = (r - mid.astype(f32)).astype(bf16)
    return hi, mid, lo


def _split2(x):
    hi = x.astype(bf16)
    lo = (x - hi.astype(f32)).astype(bf16)
    return hi, lo


def _tri(n, m, fn):
    r = lax.broadcasted_iota(jnp.int32, (n, m), 0)
    c = lax.broadcasted_iota(jnp.int32, (n, m), 1)
    return jnp.where(fn(r, c), 1.0, 0.0).astype(bf16)


def _dot(a, b):
    return jnp.dot(a, b, preferred_element_type=f32)


def _dot_nt(a, b):
    return lax.dot_general(a, b, (((1,), (1,)), ((), ())), preferred_element_type=f32)


def _dot_tn(a, b):
    return lax.dot_general(a, b, (((0,), (0,)), ((), ())), preferred_element_type=f32)


def _cumsum_lanes(x, carry, tri):
    n = x.shape[1]
    outs = []
    for b0 in range(0, n, LANES):
        w = min(LANES, n - b0)
        hi, mid, lo = _split3(x[:, b0:b0 + w])
        t = tri[:w, :w]
        c = _dot(hi, t) + _dot(mid, t) + _dot(lo, t) + carry
        carry = c[:, w - 1:w]
        outs.append(c)
    return outs, carry


def _proj_body(x_ref, g_ref, w_ref, *o_refs, cols, scales):
    h = _rms(x_ref[...], g_ref[...]).astype(bf16)
    y = _dot(h, w_ref[...])
    for o_ref, (c0, c1), sc in zip(o_refs, cols, scales):
        part = y[:, c0:c1]
        if sc != 1.0:
            part = part * sc
        o_ref[...] = part.astype(o_ref.dtype)


def _proj(x, g, w, outs, tm, w_block=None):
    M, D = x.shape
    if w_block is None:
        w_spec = pl.BlockSpec(w.shape, lambda i: (0, 0))
    else:
        layer, col_block, width = w_block
        w_spec = pl.BlockSpec((None, D, width), lambda i: (layer, 0, col_block))
    cols = tuple((c0, c1) for c0, c1, _, _ in outs)
    scales = tuple(float(s) for _, _, _, s in outs)
    return pl.pallas_call(
        functools.partial(_proj_body, cols=cols, scales=scales),
        grid=(M // tm,),
        in_specs=[pl.BlockSpec((tm, D), lambda i: (i, 0)),
                  pl.BlockSpec((1, D), lambda i: (0, 0)),
                  w_spec],
        out_specs=[pl.BlockSpec((tm, c1 - c0), lambda i: (i, 0)) for c0, c1 in cols],
        out_shape=[jax.ShapeDtypeStruct((M, c1 - c0), dt) for c0, c1, dt, _ in outs],
        compiler_params=_params(1),
        name="norm_proj",
    )(x, g.reshape(1, D), w)


def _proj_state_body(x_ref, g_ref, w_ref, *refs, n_prompt_blocks, n_heads, hd, ts):
    a_ref, sp_ref, ss_ref = refs[-3:]
    i = pl.program_id(0)
    h = _rms(x_ref[...], g_ref[...]).astype(bf16)
    y = _dot(h, w_ref[...])
    a_ref[...] = y.astype(a_ref.dtype)

    @pl.when(i < n_prompt_blocks)
    def _():
        for hh in range(n_heads):
            sp_ref[hh] = y[:, hh * hd:(hh + 1) * hd]

    @pl.when(i >= n_prompt_blocks)
    def _():
        for b in range(ss_ref.shape[0]):
            for hh in range(n_heads):
                ss_ref[b, hh] = y[b * ts:(b + 1) * ts, hh * hd:(hh + 1) * hd]


def _proj_state(x, g, w, w_block, slot, prev, t_prompt, ts, tm, n_heads, hd):
    M, D = x.shape
    layer, col_block, width = w_block
    n_p = t_prompt // tm
    in_specs = [pl.BlockSpec((tm, D), lambda i: (i, 0)),
                pl.BlockSpec((1, D), lambda i: (0, 0)),
                pl.BlockSpec((None, D, width), lambda i: (layer, 0, col_block))]
    args = [x, g.reshape(1, D), w]
    if isinstance(prev, int):
        shapes = [jax.ShapeDtypeStruct((prev, n_heads, t_prompt, hd), f32),
                  jax.ShapeDtypeStruct((prev, (M - t_prompt) // ts, n_heads, ts, hd), f32)]
        aliases = {}
    else:
        shapes = [jax.ShapeDtypeStruct(p.shape, p.dtype) for p in prev]
        in_specs += [pl.BlockSpec(memory_space=pl.ANY)] * 2
        args += list(prev)
        aliases = {3: 1, 4: 2}
    return pl.pallas_call(
        functools.partial(_proj_state_body, n_prompt_blocks=n_p, n_heads=n_heads, hd=hd, ts=ts),
        grid=(M // tm,),
        in_specs=in_specs,
        out_specs=[pl.BlockSpec((tm, width), lambda i: (i, 0)),
                   pl.BlockSpec((None, n_heads, tm, hd), lambda i: (slot, 0, jnp.minimum(i, n_p - 1), 0)),
                   pl.BlockSpec((None, tm // ts, n_heads, ts, hd),
                                lambda i: (slot, jnp.maximum(i - n_p, 0), 0, 0, 0))],
        out_shape=[jax.ShapeDtypeStruct((M, width), bf16)] + shapes,
        input_output_aliases=aliases,
        compiler_params=_params(1),
        name="norm_proj_state",
    )(*args)


def _memkv_body(m_ref, g_ref, w_ref, k_ref, v_ref, *, mw):
    h = _rms(m_ref[...], g_ref[...]).astype(bf16)
    w = w_ref[...].astype(bf16)
    k_ref[...] = _dot(h, w[:, :mw])
    v_ref[...] = _dot(h, w[:, mw:])


def _memkv(mem, g, w):
    L, D, two_mw = w.shape
    Mm = mem.shape[0]
    mw = two_mw // 2
    return pl.pallas_call(
        functools.partial(_memkv_body, mw=mw),
        grid=(L,),
        in_specs=[pl.BlockSpec((Mm, D), lambda l: (0, 0)),
                  pl.BlockSpec((None, 1, D), lambda l: (l, 0, 0)),
                  pl.BlockSpec((None, D, two_mw), lambda l: (l, 0, 0))],
        out_specs=[pl.BlockSpec((None, Mm, mw), lambda l: (l, 0, 0))] * 2,
        out_shape=[jax.ShapeDtypeStruct((L, Mm, mw), f32)] * 2,
        compiler_params=_params(1),
        name="mem_kv",
    )(mem, g.reshape(L, 1, D), w)


def _gate_logits(x_ref, g_ref, wf_ref, bf_ref):
    h = _rms(x_ref[...], g_ref[...]).astype(bf16)
    return _log_sigmoid(_dot_nt(wf_ref[...], h) + bf_ref[...])


def _fox_gate_prompt_body(x_ref, g_ref, wf_ref, bf_ref, lf_ref, cf_ref, carry_ref):
    @pl.when(pl.program_id(0) == 0)
    def _():
        carry_ref[...] = jnp.zeros_like(carry_ref)

    logf = _gate_logits(x_ref, g_ref, wf_ref, bf_ref)
    lf_ref[...] = logf
    tri = _tri(LANES, LANES, lambda r, c: r <= c)
    outs, carry = _cumsum_lanes(logf, carry_ref[:, 0:1], tri)
    for b, c in enumerate(outs):
        cf_ref[:, b * LANES:(b + 1) * LANES] = c * LOG2E
    carry_ref[...] = jnp.broadcast_to(carry, carry_ref.shape)


def _fox_gate_prompt(x, g, wf_t, bf, t_prompt, tg):
    D = x.shape[1]
    R = wf_t.shape[0]
    return pl.pallas_call(
        _fox_gate_prompt_body,
        grid=(t_prompt // tg,),
        in_specs=[pl.BlockSpec((tg, D), lambda i: (i, 0)),
                  pl.BlockSpec((1, D), lambda i: (0, 0)),
                  pl.BlockSpec((R, D), lambda i: (0, 0)),
                  pl.BlockSpec((R, 1), lambda i: (0, 0))],
        out_specs=[pl.BlockSpec((R, tg), lambda i: (0, i))] * 2,
        out_shape=[jax.ShapeDtypeStruct((R, t_prompt), f32)] * 2,
        scratch_shapes=[pltpu.VMEM((R, LANES), f32)],
        compiler_params=_params(1),
        name="fox_gate_prompt",
    )(x, g.reshape(1, D), wf_t, bf)


def _fox_gate_sample_body(x_ref, g_ref, wf_ref, bf_ref, pl_ref, lf_ref, cf_ref, *, past, ts):
    logf = _gate_logits(x_ref, g_ref, wf_ref, bf_ref)
    lf_ref[...] = logf
    tri = _tri(LANES, LANES, lambda r, c: r <= c)
    zero = jnp.zeros((logf.shape[0], 1), f32)
    outs, carry = _cumsum_lanes(pl_ref[...], zero, tri)
    for b, c in enumerate(outs):
        cf_ref[:, b * LANES:(b + 1) * LANES] = c * LOG2E
    (new,), _ = _cumsum_lanes(logf, carry, tri)
    cf_ref[:, past:past + ts] = new * LOG2E
    cf_ref[:, past + ts:] = jnp.zeros((logf.shape[0], LANES - ts), f32)


def _fox_gate_sample(x, g, wf_t, bf, plogf_t, row0, ts):
    D = x.shape[1]
    R = wf_t.shape[0]
    B, _, past = plogf_t.shape
    blk0 = row0 // ts
    return pl.pallas_call(
        functools.partial(_fox_gate_sample_body, past=past, ts=ts),
        grid=(B,),
        in_specs=[pl.BlockSpec((ts, D), lambda b: (blk0 + b, 0)),
                  pl.BlockSpec((1, D), lambda b: (0, 0)),
                  pl.BlockSpec((R, D), lambda b: (0, 0)),
                  pl.BlockSpec((R, 1), lambda b: (0, 0)),
                  pl.BlockSpec((None, R, past), lambda b: (b, 0, 0))],
        out_specs=[pl.BlockSpec((None, R, ts), lambda b: (b, 0, 0)),
                   pl.BlockSpec((None, R, past + LANES), lambda b: (b, 0, 0))],
        out_shape=[jax.ShapeDtypeStruct((B, R, ts), f32),
                   jax.ShapeDtypeStruct((B, R, past + LANES), f32)],
        compiler_params=_params(1),
        name="fox_gate_sample",
    )(x, g.reshape(1, D), wf_t, bf, plogf_t)


def _softmax_tile(s, m, l, acc, v):
    m_new = jnp.maximum(m, jnp.max(s, axis=-1, keepdims=True))
    alpha = jnp.exp2(m - m_new)
    p = jnp.exp2(s - m_new)
    l = alpha * l + jnp.sum(p, axis=-1, keepdims=True)
    acc = alpha * acc + _dot(p.astype(bf16), v)
    return m_new, l, acc


def _causal(n, strict):
    r = lax.broadcasted_iota(jnp.int32, (n, n), 0)
    c = lax.broadcasted_iota(jnp.int32, (n, n), 1)
    return c < r if strict else c <= r


def _fox_prompt_body(q_ref, k16, v16, f_ref, o_ref, knorm_ref, *, t):
    qi = pl.program_id(1)

    @pl.when(qi == 0)
    def _():
        kf = k16[...].astype(f32)
        k_sq = jnp.max(jnp.sum(kf * kf, axis=-1, keepdims=True), axis=0, keepdims=True)
        knorm_ref[...] = jnp.broadcast_to(jnp.sqrt(k_sq), knorm_ref.shape)

    q = q_ref[...]
    qf = q.astype(f32)
    qk_bound = jnp.sqrt(jnp.sum(qf * qf, axis=-1, keepdims=True)) * knorm_ref[0:1, 0:1]
    col_minus_row = (lax.broadcasted_iota(jnp.int32, (t, t), 1) - lax.broadcasted_iota(jnp.int32, (t, t), 0))

    def tile(kb, carry, max_col_minus_row):
        rows = pl.ds(pl.multiple_of(kb * t, t), t)
        s = _dot_nt(q, k16[rows, :]) - f_ref[pl.ds(kb, 1), :]
        if max_col_minus_row is not None:
            s = jnp.where(col_minus_row <= max_col_minus_row, s, MASKED_LOGIT)
        return _softmax_tile(s, *carry, v16[rows, :])

    def rest_is_zero(a, b, kb):
        bias_max = jnp.max(-f_ref[pl.ds(kb, 1), :])
        return (jnp.max(qk_bound + bias_max - jnp.maximum(a[0], b[0])) < -SKIP_BITS).astype(jnp.int32)

    init = (jnp.full((t, 1), MASKED_LOGIT, f32), jnp.zeros((t, 1), f32), jnp.zeros((t, q.shape[1]), f32))
    odd = qi % 2
    a = tile(qi, init, 0)
    b = lax.cond(odd == 1, lambda: tile(qi - 1, init, None), lambda: init)
    first = qi - 1 - odd
    n_pairs = qi // 2

    def pair(c):
        i, a, b, _ = c
        kb = first - 2 * i
        a = tile(kb, a, None)
        b = tile(kb - 1, b, None)
        return i + 1, a, b, rest_is_zero(a, b, jnp.maximum(kb - 2, 0))

    _, (m_a, l_a, acc_a), (m_b, l_b, acc_b), _ = lax.while_loop(
        lambda c: jnp.logical_and(c[0] < n_pairs, c[3] == 0), pair,
        (jnp.int32(0), a, b, rest_is_zero(a, b, jnp.maximum(first, 0))))
    m = jnp.maximum(m_a, m_b)
    w_a = jnp.exp2(m_a - m)
    w_b = jnp.exp2(m_b - m)
    o_ref[...] = ((w_a * acc_a + w_b * acc_b) / (w_a * l_a + w_b * l_b)).astype(o_ref.dtype)


def _fox_prompt(q, k, v, cf, t_prompt, n_heads, hd, t):
    M = q.shape[0]
    nq = t_prompt // t
    return pl.pallas_call(
        functools.partial(_fox_prompt_body, t=t),
        grid=(n_heads, nq),
        in_specs=[pl.BlockSpec((t, hd), lambda h, i: (i, h)),
                  pl.BlockSpec((t_prompt, hd), lambda h, i: (0, h)),
                  pl.BlockSpec((t_prompt, hd), lambda h, i: (0, h)),
                  pl.BlockSpec((None, nq, t), lambda h, i: (h, 0, 0))],
        out_specs=pl.BlockSpec((t, hd), lambda h, i: (i, h)),
        out_shape=jax.ShapeDtypeStruct((M, n_heads * hd), bf16),
        scratch_shapes=[pltpu.VMEM((8, LANES), f32)],
        compiler_params=_params(2),
        name="fox_prompt",
    )(q, k, v, cf.reshape(cf.shape[0], nq, t))


def _fox_sample_body(q_ref, k_ref, v_ref, pk_ref, pv_ref, f_ref, _, o_ref, *, past, ts, n_heads, hd):
    for h in range(n_heads):
        cols = slice(h * hd, (h + 1) * hd)
        q = q_ref[:, cols]
        f = f_ref[h:h + 1, :]
        s_past = _dot_nt(q, pk_ref[h].astype(bf16)) - f[:, :past]
        s_new = _dot_nt(q, k_ref[:, cols].astype(bf16)) - f[:, past:past + ts]
        s_new = jnp.where(_causal(ts, strict=False), s_new, -jnp.inf)
        m = jnp.maximum(jnp.max(s_past, axis=-1, keepdims=True), jnp.max(s_new, axis=-1, keepdims=True))
        p_past = jnp.exp2(s_past - m)
        p_new = jnp.exp2(s_new - m)
        l = jnp.sum(p_past, axis=-1, keepdims=True) + jnp.sum(p_new, axis=-1, keepdims=True)
        acc = (_dot(p_past.astype(bf16), pv_ref[h].astype(bf16))
               + _dot(p_new.astype(bf16), v_ref[:, cols].astype(bf16)))
        o_ref[:, cols] = (acc / l).astype(o_ref.dtype)


def _sample_attn_call(body, name, q, k, v, pk, pv, layer, extra, o_buf, row0, ts):
    _, B, n_heads, past, hd = pk.shape
    blk0 = row0 // ts
    new_spec = pl.BlockSpec((ts, n_heads * hd), lambda b: (blk0 + b, 0))
    past_spec = pl.BlockSpec((None, None, n_heads, past, hd), lambda b: (layer, b, 0, 0, 0))
    in_specs = [new_spec, new_spec, new_spec, past_spec, past_spec]
    args = [q, k, v, pk, pv]
    if extra is not None:
        in_specs.append(pl.BlockSpec((None,) + extra.shape[1:], lambda b: (b, 0, 0)))
        args.append(extra)
    in_specs.append(pl.BlockSpec(memory_space=pl.ANY))
    args.append(o_buf)
    return pl.pallas_call(
        functools.partial(body, past=past, ts=ts, n_heads=n_heads, hd=hd),
        grid=(B,),
        in_specs=in_specs,
        out_specs=new_spec,
        out_shape=jax.ShapeDtypeStruct(o_buf.shape, o_buf.dtype),
        input_output_aliases={len(args) - 1: 0},
        compiler_params=_params(1),
        name=name,
    )(*args)


def _fox_sample(q, k, v, pk, pv, layer, cf, o_buf, row0, ts):
    return _sample_attn_call(_fox_sample_body, "fox_sample", q, k, v, pk, pv, layer, cf, o_buf, row0, ts)


def _tri2(n):
    tri = _tri(n, n, lambda r, c: r >= c)
    return jnp.concatenate([tri, tri], axis=0)


def _sb_scores(q, k, tri2, valid):
    z = _dot_nt(q, k)
    sp = jnp.maximum(z, 0.0) + jnp.log(1.0 + jnp.exp2(-jnp.abs(z))) * LOG2E
    if valid is not None:
        sp = jnp.where(valid, sp, 0.0)
    cum = _dot(jnp.concatenate(_split2(sp), axis=1), tri2)
    return z - cum, cum[:, 0:1]


def _sb_out(zc, run, v, valid):
    a = jnp.exp2(zc)
    if valid is not None:
        a = jnp.where(valid, a, 0.0)
    return jnp.exp2(-run) * _dot(a.astype(bf16), v)


def _sb_pair(q, right, left, tri2, run, acc, valid_right=None, valid_left=None):
    (k_r, v_r), (k_l, v_l) = right, left
    zc_r, tot_r = _sb_scores(q, k_r, tri2, valid_right)
    zc_l, tot_l = _sb_scores(q, k_l, tri2, valid_left)
    acc = acc + _sb_out(zc_r, run, v_r, valid_right) + _sb_out(zc_l, run + tot_r, v_l, valid_left)
    return run + tot_r + tot_l, acc


def _sb_prompt_body(q_ref, k16, v16, o_ref, tri2_ref, *, tq, tk):
    qi = pl.program_id(1)

    @pl.when(qi == 0)
    def _():
        tri2_ref[...] = _tri2(tk)

    q = q_ref[...]
    tri2 = tri2_ref[...]

    def kv(kb):
        rows = pl.ds(pl.multiple_of(kb * tk, tk), tk)
        return k16[rows, :], v16[rows, :]

    r = lax.broadcasted_iota(jnp.int32, (tq, tk), 0)
    c = lax.broadcasted_iota(jnp.int32, (tq, tk), 1)
    init = (jnp.zeros((tq, 1), f32), jnp.zeros((tq, q.shape[1]), f32))
    carry = _sb_pair(q, kv(2 * qi + 1), kv(2 * qi), tri2, *init, valid_right=c + tk < r, valid_left=c < r)

    def all_rows_spent(run):
        return (jnp.min(run) >= SKIP_BITS).astype(jnp.int32)

    def below(c):
        i, run, acc, _ = c
        kb = 2 * (qi - 1 - i)
        run, acc = _sb_pair(q, kv(kb + 1), kv(kb), tri2, run, acc)
        return i + 1, run, acc, all_rows_spent(run)

    _, _, acc, _ = lax.while_loop(lambda c: jnp.logical_and(c[0] < qi, c[3] == 0), below,
                                  (jnp.int32(0), *carry, all_rows_spent(carry[0])))
    o_ref[...] = acc.astype(o_ref.dtype)


def _sb_prompt(q, k, v, t_prompt, n_heads, hd, tq):
    M = q.shape[0]
    tk = tq // 2
    return pl.pallas_call(
        functools.partial(_sb_prompt_body, tq=tq, tk=tk),
        grid=(n_heads, t_prompt // tq),
        in_specs=[pl.BlockSpec((tq, hd), lambda h, i: (i, h)),
                  pl.BlockSpec((t_prompt, hd), lambda h, i: (0, h)),
                  pl.BlockSpec((t_prompt, hd), lambda h, i: (0, h))],
        out_specs=pl.BlockSpec((tq, hd), lambda h, i: (i, h)),
        out_shape=jax.ShapeDtypeStruct((M, n_heads * hd), bf16),
        scratch_shapes=[pltpu.VMEM((2 * tk, tk), bf16)],
        compiler_params=_params(2),
        name="sb_prompt",
    )(q, k, v)


def _sb_sample_body(q_ref, k_ref, v_ref, pk_ref, pv_ref, _, o_ref, *, past, ts, n_heads, hd, tp):
    tri2_new = _tri2(ts)
    tri2 = _tri2(tp)
    strict = _causal(ts, strict=True)
    for h in range(n_heads):
        cols = slice(h * hd, (h + 1) * hd)
        q = q_ref[:, cols]
        zc, run = _sb_scores(q, k_ref[:, cols].astype(bf16), tri2_new, strict)
        acc = _sb_out(zc, jnp.zeros((ts, 1), f32), v_ref[:, cols].astype(bf16), strict)

        def past_kv(p0):
            return pk_ref[h, p0:p0 + tp, :].astype(bf16), pv_ref[h, p0:p0 + tp, :].astype(bf16)

        for p0 in range(past - tp, -1, -2 * tp):
            run, acc = _sb_pair(q, past_kv(p0), past_kv(p0 - tp), tri2, run, acc)
        o_ref[:, cols] = acc.astype(o_ref.dtype)


def _sb_sample(q, k, v, pk, pv, layer, o_buf, row0, ts, tp):
    return _sample_attn_call(functools.partial(_sb_sample_body, tp=tp), "sb_sample",
                             q, k, v, pk, pv, layer, None, o_buf, row0, ts)


def _mem_attn_body(q_ref, k_ref, v_ref, *rest, n_heads, hd):
    o_ref = rest[-1]
    for h in range(n_heads):
        sl = slice(h * hd, (h + 1) * hd)
        head = (lambda ref: ref[:, h, :]) if len(k_ref.shape) == 3 else (lambda ref: ref[:, sl])
        s = _dot_nt(q_ref[:, sl], head(k_ref).astype(bf16))
        p = jnp.exp(s - jnp.max(s, axis=-1, keepdims=True))
        acc = _dot(p.astype(bf16), head(v_ref).astype(bf16))
        o_ref[:, sl] = (acc / jnp.sum(p, axis=-1, keepdims=True)).astype(o_ref.dtype)


def _mem_attn_prompt(qm, mk, mv, t_prompt, tq, n_heads, hd):
    M, W = qm.shape
    Mm = mk.shape[0]
    return pl.pallas_call(
        functools.partial(_mem_attn_body, n_heads=n_heads, hd=hd),
        grid=(t_prompt // tq,),
        in_specs=[pl.BlockSpec((tq, W), lambda i: (i, 0)),
                  pl.BlockSpec((Mm, W), lambda i: (0, 0)),
                  pl.BlockSpec((Mm, W), lambda i: (0, 0))],
        out_specs=pl.BlockSpec((tq, W), lambda i: (i, 0)),
        out_shape=jax.ShapeDtypeStruct((M, W), bf16),
        compiler_params=_params(1),
        name="mem_attn_prompt",
    )(qm, mk, mv)


def _mem_attn_sample(qm, mk, mv, layer, o_buf, row0, ts):
    M, W = qm.shape
    _, B, Mm, n_heads, hd = mk.shape
    blk0 = row0 // ts
    row_spec = pl.BlockSpec((ts, W), lambda b: (blk0 + b, 0))
    mem_spec = pl.BlockSpec((None, None, Mm, n_heads, hd), lambda b: (layer, b, 0, 0, 0))
    return pl.pallas_call(
        functools.partial(_mem_attn_body, n_heads=n_heads, hd=hd),
        grid=(B,),
        in_specs=[row_spec, mem_spec, mem_spec, pl.BlockSpec(memory_space=pl.ANY)],
        out_specs=row_spec,
        out_shape=jax.ShapeDtypeStruct((M, W), bf16),
        input_output_aliases={3: 0},
        compiler_params=_params(1),
        name="mem_attn_sample",
    )(qm, mk, mv, o_buf)


def _gla_body(q_ref, k_ref, v_ref, r_ref, gl_ref, w2_ref, bg_ref, br_ref, ng_ref, s0_ref,
              o_ref, so_ref, S, *, n_prompt_chunks, n_heads, dk, dkp, dv):
    c = pl.program_id(0)
    C, U = GLA_CHUNK, GLA_SUB

    @pl.when(c == 0)
    def _():
        S[...] = jnp.zeros_like(S)

    @pl.when(c >= n_prompt_chunks)
    def _():
        for h in range(n_heads):
            S[h, :dk, :] = s0_ref[h]
            S[h, dk:, :] = jnp.zeros((dkp - dk, dv), f32)

    log_a = _log_sigmoid(_dot(gl_ref[...].astype(bf16), w2_ref[...]) + bg_ref[...]) * (1.0 / GLA_GATE_TEMP)
    tril = _tri(C, C, lambda r, cc: r >= cc)
    row_in_sub = lax.broadcasted_iota(jnp.int32, (U, 1), 0)

    for h in range(n_heads):
        ks = slice(h * dkp, (h + 1) * dkp)
        vs = slice(h * dv, (h + 1) * dv)
        hi, mid, lo = _split3(log_a[:, ks])
        G = _dot(tril, hi) + _dot(tril, mid) + _dot(tril, lo)
        q = q_ref[:, ks]
        k = k_ref[:, ks]
        v = v_ref[:, vs]
        v16 = v.astype(bf16)
        S_h = S[h]
        o_inter = _dot((q * jnp.exp(G)).astype(bf16), S_h.astype(bf16))

        o_rows = []
        for b in range(C // U):
            r0 = b * U
            Gb, qb = G[r0:r0 + U], q[r0:r0 + U]
            o_b = o_inter[r0:r0 + U]
            if b > 0:
                ref = G[r0 - 1:r0]
                qg = qb * jnp.exp(Gb - ref)
                kg = k[:r0] * jnp.exp(ref - G[:r0])
                a = _dot_nt(qg.astype(bf16), kg.astype(bf16))
                o_b = o_b + _dot(a.astype(bf16), v16[:r0])
            for j in range(U):
                s = r0 + j
                e = jnp.exp(jnp.minimum(Gb - G[s:s + 1], 0.0))
                a = jnp.sum(qb * e * k[s:s + 1], axis=-1, keepdims=True)
                a = jnp.where(row_in_sub >= j, a, 0.0)
                o_b = o_b + a * v[s:s + 1]
            o_rows.append(o_b)
        o = jnp.concatenate(o_rows, axis=0)

        g_last = G[C - 1:C]
        kl = k * jnp.exp(g_last - G)
        decay_col = jnp.exp(G.T[:, C - 1:C])
        S_new = decay_col * S_h + _dot_tn(kl.astype(bf16), v16)
        S[h] = S_new
        so_ref[h] = S_new[:dk]

        on = o * lax.rsqrt(jnp.mean(o * o, axis=-1, keepdims=True) + RMS_EPS) * ng_ref[:, vs]
        x = r_ref[:, vs] + br_ref[:, vs]
        o_ref[:, vs] = (on * (x / (1.0 + jnp.exp(-x)))).astype(o_ref.dtype)


def _gla(q, k, v, r, glow, w2, bg, br, ng, s0, n_prompt_chunks, n_heads, dk, dkp, dv):
    M = q.shape[0]
    C = GLA_CHUNK
    n_chunks = M // C
    B = s0.shape[0]
    row = lambda width: pl.BlockSpec((C, width), lambda c: (c, 0))
    const = lambda a: pl.BlockSpec(a.shape, lambda c: (0,) * a.ndim)
    state_in = pl.BlockSpec((None, n_heads, dk, dv), lambda c: (jnp.maximum(c - n_prompt_chunks, 0), 0, 0, 0))
    state_out = pl.BlockSpec((None, n_heads, dk, dv),
                             lambda c: (jnp.maximum(c - (n_prompt_chunks - 1), 0), 0, 0, 0))
    return pl.pallas_call(
        functools.partial(_gla_body, n_prompt_chunks=n_prompt_chunks, n_heads=n_heads, dk=dk, dkp=dkp, dv=dv),
        grid=(n_chunks,),
        in_specs=[row(n_heads * dkp), row(n_heads * dkp), row(n_heads * dv), row(n_heads * dv),
                  row(glow.shape[1]), const(w2), const(bg), const(br), const(ng), state_in],
        out_specs=[row(n_heads * dv), state_out],
        out_shape=[jax.ShapeDtypeStruct((M, n_heads * dv), bf16),
                   jax.ShapeDtypeStruct((B + 1, n_heads, dk, dv), f32)],
        scratch_shapes=[pltpu.VMEM((n_heads, dkp, dv), f32)],
        compiler_params=_params(1),
        name="gla",
    )(q, k, v, r, glow, w2, bg, br, ng, s0)


def _out_proj_body(o_ref, om_ref, x_ref, w_ref, y_ref, *, wo):
    y_ref[...] = x_ref[...] + _dot(o_ref[...], w_ref[:wo, :]) + _dot(om_ref[...], w_ref[wo:, :])


def _out_proj(o, om, x, w, tm):
    M, D = x.shape
    wo, wm = o.shape[1], om.shape[1]
    return pl.pallas_call(
        functools.partial(_out_proj_body, wo=wo),
        grid=(M // tm,),
        in_specs=[pl.BlockSpec((tm, wo), lambda i: (i, 0)),
                  pl.BlockSpec((tm, wm), lambda i: (i, 0)),
                  pl.BlockSpec((tm, D), lambda i: (i, 0)),
                  pl.BlockSpec((wo + wm, D), lambda i: (0, 0))],
        out_specs=pl.BlockSpec((tm, D), lambda i: (i, 0)),
        out_shape=jax.ShapeDtypeStruct((M, D), f32),
        compiler_params=_params(1),
        name="out_proj",
    )(o, om, x, w)


def _mlp_body(x_ref, g_ref, wu_ref, wd_ref, gf_ref, y_ref, h_ref, *, final_norm):
    j = pl.program_id(1)

    @pl.when(j == 0)
    def _():
        x = x_ref[...]
        h_ref[...] = _rms(x, g_ref[...]).astype(bf16)
        y_ref[...] = x

    u = jnp.maximum(_dot(h_ref[...], wu_ref[...]), 0.0)
    y_ref[...] += _dot((u * u).astype(bf16), wd_ref[...])

    if final_norm:
        @pl.when(j == pl.num_programs(1) - 1)
        def _():
            y_ref[...] = _rms(y_ref[...], gf_ref[...])


def _mlp(x, g, wu, wd, layer, gf, final_norm, tm, tf):
    M, D = x.shape
    F = wu.shape[2]
    return pl.pallas_call(
        functools.partial(_mlp_body, final_norm=final_norm),
        grid=(M // tm, F // tf),
        in_specs=[pl.BlockSpec((tm, D), lambda i, j: (i, 0)),
                  pl.BlockSpec((1, D), lambda i, j: (0, 0)),
                  pl.BlockSpec((None, D, tf), lambda i, j: (layer, 0, j)),
                  pl.BlockSpec((None, tf, D), lambda i, j: (layer, j, 0)),
                  pl.BlockSpec((1, D), lambda i, j: (0, 0))],
        out_specs=pl.BlockSpec((tm, D), lambda i, j: (i, 0)),
        out_shape=jax.ShapeDtypeStruct((M, D), f32),
        scratch_shapes=[pltpu.VMEM((tm, D), bf16)],
        compiler_params=_params(2),
        name="mlp",
    )(x, g.reshape(1, D), wu, wd, gf.reshape(1, D))


def _largest_divisor(n, cap, mult):
    best = None
    for d in range(mult, min(n, cap) + 1, mult):
        if n % d == 0:
            best = d
    assert best is not None, (n, cap, mult)
    return best


def kernel(x_prompt, x_sample, cache_fox_k, cache_fox_v, cache_fox_logf, cache_sb_k, cache_sb_v, state_gla,
           cache_mem_k, cache_mem_v, mem_prompt, norm_mix_g, norm_mlp_g, norm_mem_g, norm_final_g, w_mem_kv,
           w_in_fox, b_forget, w_out_fox, w_in_sb, w_out_sb, w_in_gla, w_gate2_gla, b_gate_gla, b_outgate_gla,
           norm_gla_g, w_out_gla, w_up, w_down):
    Bp, Tp0, D = x_prompt.shape
    Bs, Ts, _ = x_sample.shape
    assert Bp == 1, "the prompt group is handled as one sequence"
    Tp = Bp * Tp0
    M = Tp + Bs * Ts
    depth = norm_mix_g.shape[0]
    H, hd = cache_fox_k.shape[-2:]
    aw = H * hd
    past = cache_fox_k.shape[2]
    Mm, MH = cache_mem_k.shape[2], cache_mem_k.shape[3]
    mw = MH * hd
    GH, dk, dv = state_gla.shape[-3:]
    dkp = -(-dk // LANES) * LANES
    rank = w_gate2_gla.shape[1]
    C = GLA_CHUNK
    assert Ts == C and Tp % C == 0 and M % 16 == 0

    tm = _largest_divisor(M, 512, 16)
    tm_st = _largest_divisor(math.gcd(Tp, M - Tp), 512, Ts)
    tm_mlp = _largest_divisor(M, 576, 16)
    tf = _largest_divisor(w_up.shape[2], 512, LANES)
    ta = _largest_divisor(Tp, 512, 2 * LANES)
    tg = _largest_divisor(Tp, 512, LANES)
    tq_mem = _largest_divisor(Tp, 512, 16)
    tp_sb = _largest_divisor(past // 2, 256, LANES)
    att_scale = hd ** -0.5
    att_scale2 = att_scale * LOG2E

    x = jnp.concatenate([x_prompt.reshape(Tp, D), x_sample.reshape(Bs * Ts, D)], axis=0)
    mk_p, mv_p = _memkv(mem_prompt.reshape(Mm, D), norm_mem_g, w_mem_kv)
    wu16, wd16 = w_up.astype(bf16), w_down.astype(bf16)
    w16_fox, w16_sb = w_in_fox.astype(bf16), w_in_sb.astype(bf16)

    n_fox, n_sb = w_in_fox.shape[0], w_in_sb.shape[0]
    kv_state = {0: None, 1: None}
    logf_p, logf_s, gla_st = [], [], []
    for i in range(depth):
        kind, j = i % 3, i // 3
        g = norm_mix_g[i]
        if kind in (0, 1):
            w = (w_in_fox if kind == 0 else w_in_sb)[j]
            w16 = w16_fox if kind == 0 else w16_sb
            n_gate = H if kind == 0 else 0
            (q,) = _proj(x, g, w16, [(0, aw, bf16, att_scale2)], tm, w_block=(j, 0, aw))
            n_layers = n_fox if kind == 0 else n_sb
            st = kv_state[kind]
            k, *st_k = _proj_state(x, g, w16, (j, 1, aw), j, st[0] if st else n_layers, Tp, Ts, tm_st, H, hd)
            v, *st_v = _proj_state(x, g, w16, (j, 2, aw), j, st[1] if st else n_layers, Tp, Ts, tm_st, H, hd)
            kv_state[kind] = (st_k, st_v)
            (qm,) = _proj(x, g, w[:, 3 * aw + n_gate:].astype(bf16), [(0, mw, bf16, att_scale)], tm)
            if kind == 0:
                wf_t = jnp.pad(w[:, 3 * aw:3 * aw + H].T, ((0, FOX_GATE_ROWS - H), (0, 0))).astype(bf16)
                bf_col = jnp.pad(b_forget[j], (0, FOX_GATE_ROWS - H)).reshape(FOX_GATE_ROWS, 1)
                lf_p, cf_p = _fox_gate_prompt(x, g, wf_t, bf_col, Tp, tg)
                plogf_t = jnp.pad(jnp.swapaxes(cache_fox_logf[j], 1, 2), ((0, 0), (0, FOX_GATE_ROWS - H), (0, 0)))
                lf_s, cf_s = _fox_gate_sample(x, g, wf_t, bf_col, plogf_t, Tp, Ts)
                o = _fox_prompt(q, k, v, cf_p, Tp, H, hd, ta)
                o = _fox_sample(q, k, v, jnp.swapaxes(cache_fox_k, 2, 3), jnp.swapaxes(cache_fox_v, 2, 3), j,
                                cf_s, o, Tp, Ts)
                logf_p.append(lf_p[:H].T.reshape(Bp, Tp0, H))
                logf_s.append(jnp.swapaxes(lf_s[:, :H], 1, 2))
                w_out = w_out_fox[j]
            else:
                o = _sb_prompt(q, k, v, Tp, H, hd, ta)
                o = _sb_sample(q, k, v, jnp.swapaxes(cache_sb_k, 2, 3), jnp.swapaxes(cache_sb_v, 2, 3), j,
                               o, Tp, Ts, tp_sb)
                w_out = w_out_sb[j]
        else:
            w = w_in_gla[j]
            kw, vw = GH * dk, GH * dv

            def pad_heads(a):
                a = a.reshape(a.shape[:-1] + (GH, dk))
                a = jnp.pad(a, [(0, 0)] * (a.ndim - 1) + [(0, dkp - dk)])
                return a.reshape(a.shape[:-2] + (GH * dkp,))

            wqk = jnp.concatenate([pad_heads(w[:, :kw]), pad_heads(w[:, kw:2 * kw])], axis=1).astype(bf16)
            wv = w[:, 2 * kw:2 * kw + vw].astype(bf16)
            wr = w[:, 2 * kw + vw:2 * kw + 2 * vw].astype(bf16)
            c0 = 2 * kw + 2 * vw
            wmg = jnp.concatenate([w[:, c0 + rank:], jnp.pad(w[:, c0:c0 + rank], ((0, 0), (0, LANES - rank)))],
                                  axis=1).astype(bf16)
            q, k = _proj(x, g, wqk, [(0, GH * dkp, f32, dk ** -0.5), (GH * dkp, 2 * GH * dkp, f32, 1.0)], tm)
            (v,) = _proj(x, g, wv, [(0, vw, f32, 1.0)], tm)
            (r,) = _proj(x, g, wr, [(0, vw, f32, 1.0)], tm)
            qm, glow = _proj(x, g, wmg, [(0, mw, bf16, att_scale), (mw, mw + LANES, f32, 1.0)], tm)
            w2 = jnp.pad(pad_heads(w_gate2_gla[j]), ((0, LANES - rank), (0, 0))).astype(bf16)
            bg = pad_heads(b_gate_gla[j]).reshape(1, GH * dkp)
            o, st = _gla(q, k, v, r, glow, w2, bg, b_outgate_gla[j].reshape(1, vw), norm_gla_g[j].reshape(1, vw),
                         state_gla[j], Tp // C, GH, dk, dkp, dv)
            gla_st.append(st)
            w_out = w_out_gla[j]

        om = _mem_attn_prompt(qm, mk_p[i], mv_p[i], Tp, tq_mem, MH, hd)
        om = _mem_attn_sample(qm, cache_mem_k, cache_mem_v, i, om, Tp, Ts)
        x = _out_proj(o, om, x, w_out.astype(bf16), tm)
        x = _mlp(x, norm_mlp_g[i], wu16, wd16, i, norm_final_g, i == depth - 1, tm_mlp, tf)

    def prompt_state(a):
        return jnp.swapaxes(a.reshape((a.shape[0], Bp, H, Tp0, hd)), 2, 3)

    def sample_state(a):
        return jnp.swapaxes(a, 2, 3)

    (fox_k, fox_v), (sb_k, sb_v) = kv_state[0], kv_state[1]
    gla = jnp.stack(gla_st)
    return (x[:Tp].reshape(Bp, Tp0, D), x[Tp:].reshape(Bs, Ts, D),
            prompt_state(fox_k[0]), prompt_state(fox_v[0]), jnp.stack(logf_p),
            prompt_state(sb_k[0]), prompt_state(sb_v[0]),
            gla[:, :1],
            mk_p.reshape(depth, Bp, Mm, MH, hd), mv_p.reshape(depth, Bp, Mm, MH, hd),
            sample_state(fox_k[1]), sample_state(fox_v[1]), jnp.stack(logf_s),
            sample_state(sb_k[1]), sample_state(sb_v[1]),
            gla[:, 1:])
```

```python
import functools
import math

import jax
import jax.numpy as jnp
from jax import lax
from jax.experimental import pallas as pl
from jax.experimental.pallas import tpu as pltpu

f32 = jnp.float32
bf16 = jnp.bfloat16

RMS_EPS = 1e-6
GLA_GATE_TEMP = 16.0
GLA_CHUNK = 64
GLA_SUB = 8
GLA_STEP_CHUNKS = 2
LOG2E = 1.4426950408889634
MASKED_LOGIT = -1e30
SKIP_BITS = 160.0
LANES = 128
FOX_GATE_ROWS = 16
VMEM_LIMIT_BYTES = 56 * 1024 * 1024


def _params(n_axes):
    return pltpu.CompilerParams(dimension_semantics=("arbitrary",) * n_axes,
                                vmem_limit_bytes=VMEM_LIMIT_BYTES)


def _rms(x, g):
    return x * lax.rsqrt(jnp.mean(x * x, axis=-1, keepdims=True) + RMS_EPS) * g


def _log_sigmoid(z):
    return jnp.minimum(z, 0.0) - jnp.log1p(jnp.exp(-jnp.abs(z)))


def _split3(x):
    hi = x.astype(bf16)
    r = x - hi.astype(f32)
    mid = r.astype(bf16)
    lo = (r - mid.astype(f32)).astype(bf16)
    return hi, mid, lo


def _split2(x):
    hi = x.astype(bf16)
    lo = (x - hi.astype(f32)).astype(bf16)
    return hi, lo


def _tri(n, m, fn):
    r = lax.broadcasted_iota(jnp.int32, (n, m), 0)
    c = lax.broadcasted_iota(jnp.int32, (n, m), 1)
    return jnp.where(fn(r, c), 1.0, 0.0).astype(bf16)


def _dot(a, b):
    return jnp.dot(a, b, preferred_element_type=f32)


def _dot_nt(a, b):
    return lax.dot_general(a, b, (((1,), (1,)), ((), ())), preferred_element_type=f32)


def _dot_tn(a, b):
    return lax.dot_general(a, b, (((0,), (0,)), ((), ())), preferred_element_type=f32)


def _cumsum_lanes(x, carry, tri):
    n = x.shape[1]
    outs = []
    for b0 in range(0, n, LANES):
        w = min(LANES, n - b0)
        hi, mid, lo = _split3(x[:, b0:b0 + w])
        t = tri[:w, :w]
        c = _dot(hi, t) + _dot(mid, t) + _dot(lo, t) + carry
        carry = c[:, w - 1:w]
        outs.append(c)
    return outs, carry


def _proj_body(x_ref, g_ref, w_ref, *o_refs, cols, scales):
    h = _rms(x_ref[...], g_ref[...]).astype(bf16)
    y = _dot(h, w_ref[...])
    for o_ref, (c0, c1), sc in zip(o_refs, cols, scales):
        part = y[:, c0:c1]
        if sc != 1.0:
            part = part * sc
        o_ref[...] = part.astype(o_ref.dtype)


def _proj(x, g, w, outs, tm, w_block=None):
    M, D = x.shape
    if w_block is None:
        w_spec = pl.BlockSpec(w.shape, lambda i: (0, 0))
    else:
        layer, col_block, width = w_block
        w_spec = pl.BlockSpec((None, D, width), lambda i: (layer, 0, col_block))
    cols = tuple((c0, c1) for c0, c1, _, _ in outs)
    scales = tuple(float(s) for _, _, _, s in outs)
    return pl.pallas_call(
        functools.partial(_proj_body, cols=cols, scales=scales),
        grid=(M // tm,),
        in_specs=[pl.BlockSpec((tm, D), lambda i: (i, 0)),
                  pl.BlockSpec((1, D), lambda i: (0, 0)),
                  w_spec],
        out_specs=[pl.BlockSpec((tm, c1 - c0), lambda i: (i, 0)) for c0, c1 in cols],
        out_shape=[jax.ShapeDtypeStruct((M, c1 - c0), dt) for c0, c1, dt, _ in outs],
        compiler_params=_params(1),
        name="norm_proj",
    )(x, g.reshape(1, D), w)


def _proj_state_body(x_ref, g_ref, w_ref, *refs, n_prompt_blocks, n_heads, hd, ts):
    a_ref, sp_ref, ss_ref = refs[-3:]
    i = pl.program_id(0)
    h = _rms(x_ref[...], g_ref[...]).astype(bf16)
    y = _dot(h, w_ref[...])
    a_ref[...] = y.astype(a_ref.dtype)

    @pl.when(i < n_prompt_blocks)
    def _():
        for hh in range(n_heads):
            sp_ref[hh] = y[:, hh * hd:(hh + 1) * hd]

    @pl.when(i >= n_prompt_blocks)
    def _():
        for b in range(ss_ref.shape[0]):
            for hh in range(n_heads):
                ss_ref[b, hh] = y[b * ts:(b + 1) * ts, hh * hd:(hh + 1) * hd]


def _proj_state(x, g, w, w_block, slot, prev, t_prompt, ts, tm, n_heads, hd):
    M, D = x.shape
    layer, col_block, width = w_block
    n_p = t_prompt // tm
    in_specs = [pl.BlockSpec((tm, D), lambda i: (i, 0)),
                pl.BlockSpec((1, D), lambda i: (0, 0)),
                pl.BlockSpec((None, D, width), lambda i: (layer, 0, col_block))]
    args = [x, g.reshape(1, D), w]
    if isinstance(prev, int):
        shapes = [jax.ShapeDtypeStruct((prev, n_heads, t_prompt, hd), f32),
                  jax.ShapeDtypeStruct((prev, (M - t_prompt) // ts, n_heads, ts, hd), f32)]
        aliases = {}
    else:
        shapes = [jax.ShapeDtypeStruct(p.shape, p.dtype) for p in prev]
        in_specs += [pl.BlockSpec(memory_space=pl.ANY)] * 2
        args += list(prev)
        aliases = {3: 1, 4: 2}
    return pl.pallas_call(
        functools.partial(_proj_state_body, n_prompt_blocks=n_p, n_heads=n_heads, hd=hd, ts=ts),
        grid=(M // tm,),
        in_specs=in_specs,
        out_specs=[pl.BlockSpec((tm, width), lambda i: (i, 0)),
                   pl.BlockSpec((None, n_heads, tm, hd), lambda i: (slot, 0, jnp.minimum(i, n_p - 1), 0)),
                   pl.BlockSpec((None, tm // ts, n_heads, ts, hd),
                                lambda i: (slot, jnp.maximum(i - n_p, 0), 0, 0, 0))],
        out_shape=[jax.ShapeDtypeStruct((M, width), bf16)] + shapes,
        input_output_aliases=aliases,
        compiler_params=_params(1),
        name="norm_proj_state",
    )(*args)


def _memkv_body(m_ref, g_ref, w_ref, k_ref, v_ref, *, mw):
    h = _rms(m_ref[...], g_ref[...]).astype(bf16)
    w = w_ref[...].astype(bf16)
    k_ref[...] = _dot(h, w[:, :mw])
    v_ref[...] = _dot(h, w[:, mw:])


def _memkv(mem, g, w):
    L, D, two_mw = w.shape
    Mm = mem.shape[0]
    mw = two_mw // 2
    return pl.pallas_call(
        functools.partial(_memkv_body, mw=mw),
        grid=(L,),
        in_specs=[pl.BlockSpec((Mm, D), lambda l: (0, 0)),
                  pl.BlockSpec((None, 1, D), lambda l: (l, 0, 0)),
                  pl.BlockSpec((None, D, two_mw), lambda l: (l, 0, 0))],
        out_specs=[pl.BlockSpec((None, Mm, mw), lambda l: (l, 0, 0))] * 2,
        out_shape=[jax.ShapeDtypeStruct((L, Mm, mw), f32)] * 2,
        compiler_params=_params(1),
        name="mem_kv",
    )(mem, g.reshape(L, 1, D), w)


def _gate_logits(x_ref, g_ref, wf_ref, bf_ref):
    h = _rms(x_ref[...], g_ref[...]).astype(bf16)
    return _log_sigmoid(_dot_nt(wf_ref[...], h) + bf_ref[...])


def _fox_gate_prompt_body(x_ref, g_ref, wf_ref, bf_ref, lf_ref, cf_ref, carry_ref):
    @pl.when(pl.program_id(0) == 0)
    def _():
        carry_ref[...] = jnp.zeros_like(carry_ref)

    logf = _gate_logits(x_ref, g_ref, wf_ref, bf_ref)
    lf_ref[...] = logf
    tri = _tri(LANES, LANES, lambda r, c: r <= c)
    outs, carry = _cumsum_lanes(logf, carry_ref[:, 0:1], tri)
    for b, c in enumerate(outs):
        cf_ref[:, b * LANES:(b + 1) * LANES] = c * LOG2E
    carry_ref[...] = jnp.broadcast_to(carry, carry_ref.shape)


def _fox_gate_prompt(x, g, wf_t, bf, t_prompt, tg):
    D = x.shape[1]
    R = wf_t.shape[0]
    return pl.pallas_call(
        _fox_gate_prompt_body,
        grid=(t_prompt // tg,),
        in_specs=[pl.BlockSpec((tg, D), lambda i: (i, 0)),
                  pl.BlockSpec((1, D), lambda i: (0, 0)),
                  pl.BlockSpec((R, D), lambda i: (0, 0)),
                  pl.BlockSpec((R, 1), lambda i: (0, 0))],
        out_specs=[pl.BlockSpec((R, tg), lambda i: (0, i))] * 2,
        out_shape=[jax.ShapeDtypeStruct((R, t_prompt), f32)] * 2,
        scratch_shapes=[pltpu.VMEM((R, LANES), f32)],
        compiler_params=_params(1),
        name="fox_gate_prompt",
    )(x, g.reshape(1, D), wf_t, bf)


def _fox_gate_sample_body(x_ref, g_ref, wf_ref, bf_ref, pl_ref, lf_ref, cf_ref, *, past, ts):
    logf = _gate_logits(x_ref, g_ref, wf_ref, bf_ref)
    lf_ref[...] = logf
    tri = _tri(LANES, LANES, lambda r, c: r <= c)
    zero = jnp.zeros((logf.shape[0], 1), f32)
    outs, carry = _cumsum_lanes(pl_ref[...], zero, tri)
    for b, c in enumerate(outs):
        cf_ref[:, b * LANES:(b + 1) * LANES] = c * LOG2E
    (new,), _ = _cumsum_lanes(logf, carry, tri)
    cf_ref[:, past:past + ts] = new * LOG2E
    cf_ref[:, past + ts:] = jnp.zeros((logf.shape[0], LANES - ts), f32)


def _fox_gate_sample(x, g, wf_t, bf, plogf_t, row0, ts):
    D = x.shape[1]
    R = wf_t.shape[0]
    B, _, past = plogf_t.shape
    blk0 = row0 // ts
    return pl.pallas_call(
        functools.partial(_fox_gate_sample_body, past=past, ts=ts),
        grid=(B,),
        in_specs=[pl.BlockSpec((ts, D), lambda b: (blk0 + b, 0)),
                  pl.BlockSpec((1, D), lambda b: (0, 0)),
                  pl.BlockSpec((R, D), lambda b: (0, 0)),
                  pl.BlockSpec((R, 1), lambda b: (0, 0)),
                  pl.BlockSpec((None, R, past), lambda b: (b, 0, 0))],
        out_specs=[pl.BlockSpec((None, R, ts), lambda b: (b, 0, 0)),
                   pl.BlockSpec((None, R, past + LANES), lambda b: (b, 0, 0))],
        out_shape=[jax.ShapeDtypeStruct((B, R, ts), f32),
                   jax.ShapeDtypeStruct((B, R, past + LANES), f32)],
        compiler_params=_params(1),
        name="fox_gate_sample",
    )(x, g.reshape(1, D), wf_t, bf, plogf_t)


def _softmax_tile(s, m, l, acc, v):
    m_new = jnp.maximum(m, jnp.max(s, axis=-1, keepdims=True))
    alpha = jnp.exp2(m - m_new)
    p = jnp.exp2(s - m_new)
    l = alpha * l + jnp.sum(p, axis=-1, keepdims=True)
    acc = alpha * acc + _dot(p.astype(bf16), v)
    return m_new, l, acc


def _causal(n, strict):
    r = lax.broadcasted_iota(jnp.int32, (n, n), 0)
    c = lax.broadcasted_iota(jnp.int32, (n, n), 1)
    return c < r if strict else c <= r


def _fox_prompt_body(q_ref, k16, v16, f_ref, o_ref, knorm_ref, *, t):
    qi = pl.program_id(1)

    @pl.when(qi == 0)
    def _():
        kf = k16[...].astype(f32)
        k_sq = jnp.max(jnp.sum(kf * kf, axis=-1, keepdims=True), axis=0, keepdims=True)
        knorm_ref[...] = jnp.broadcast_to(jnp.sqrt(k_sq), knorm_ref.shape)

    q = q_ref[...]
    qf = q.astype(f32)
    qk_bound = jnp.sqrt(jnp.sum(qf * qf, axis=-1, keepdims=True)) * knorm_ref[0:1, 0:1]
    col_minus_row = (lax.broadcasted_iota(jnp.int32, (t, t), 1) - lax.broadcasted_iota(jnp.int32, (t, t), 0))

    def tile(kb, carry, max_col_minus_row):
        rows = pl.ds(pl.multiple_of(kb * t, t), t)
        s = _dot_nt(q, k16[rows, :]) - f_ref[pl.ds(kb, 1), :]
        if max_col_minus_row is not None:
            s = jnp.where(col_minus_row <= max_col_minus_row, s, MASKED_LOGIT)
        return _softmax_tile(s, *carry, v16[rows, :])

    def rest_is_zero(a, b, kb):
        bias_max = jnp.max(-f_ref[pl.ds(kb, 1), :])
        return (jnp.max(qk_bound + bias_max - jnp.maximum(a[0], b[0])) < -SKIP_BITS).astype(jnp.int32)

    init = (jnp.full((t, 1), MASKED_LOGIT, f32), jnp.zeros((t, 1), f32), jnp.zeros((t, q.shape[1]), f32))
    odd = qi % 2
    a, b = lax.cond(odd == 1, lambda: (tile(qi, init, 0), tile(qi - 1, init, None)),
                    lambda: (tile(qi, init, 0), init))
    first = qi - 1 - odd
    n_pairs = qi // 2

    def pair(c):
        i, a, b, _ = c
        kb = first - 2 * i
        a = tile(kb, a, None)
        b = tile(kb - 1, b, None)
        return i + 1, a, b, rest_is_zero(a, b, jnp.maximum(kb - 2, 0))

    _, (m_a, l_a, acc_a), (m_b, l_b, acc_b), _ = lax.while_loop(
        lambda c: jnp.logical_and(c[0] < n_pairs, c[3] == 0), pair,
        (jnp.int32(0), a, b, rest_is_zero(a, b, jnp.maximum(first, 0))))
    m = jnp.maximum(m_a, m_b)
    w_a = jnp.exp2(m_a - m)
    w_b = jnp.exp2(m_b - m)
    o_ref[...] = ((w_a * acc_a + w_b * acc_b) / (w_a * l_a + w_b * l_b)).astype(o_ref.dtype)


def _fox_prompt(q, k, v, cf, t_prompt, n_heads, hd, t):
    M = q.shape[0]
    nq = t_prompt // t
    return pl.pallas_call(
        functools.partial(_fox_prompt_body, t=t),
        grid=(n_heads, nq),
        in_specs=[pl.BlockSpec((t, hd), lambda h, i: (i, h)),
                  pl.BlockSpec((t_prompt, hd), lambda h, i: (0, h)),
                  pl.BlockSpec((t_prompt, hd), lambda h, i: (0, h)),
                  pl.BlockSpec((None, nq, t), lambda h, i: (h, 0, 0))],
        out_specs=pl.BlockSpec((t, hd), lambda h, i: (i, h)),
        out_shape=jax.ShapeDtypeStruct((M, n_heads * hd), bf16),
        scratch_shapes=[pltpu.VMEM((8, LANES), f32)],
        compiler_params=_params(2),
        name="fox_prompt",
    )(q, k, v, cf.reshape(cf.shape[0], nq, t))


def _fox_sample_body(q_ref, k_ref, v_ref, pk_ref, pv_ref, f_ref, _, o_ref, *, past, ts, n_heads, hd):
    for h in range(n_heads):
        cols = slice(h * hd, (h + 1) * hd)
        q = q_ref[:, cols]
        f = f_ref[h:h + 1, :]
        s_past = _dot_nt(q, pk_ref[h].astype(bf16)) - f[:, :past]
        s_new = _dot_nt(q, k_ref[:, cols].astype(bf16)) - f[:, past:past + ts]
        s_new = jnp.where(_causal(ts, strict=False), s_new, -jnp.inf)
        m = jnp.maximum(jnp.max(s_past, axis=-1, keepdims=True), jnp.max(s_new, axis=-1, keepdims=True))
        p_past = jnp.exp2(s_past - m)
        p_new = jnp.exp2(s_new - m)
        l = jnp.sum(p_past, axis=-1, keepdims=True) + jnp.sum(p_new, axis=-1, keepdims=True)
        acc = (_dot(p_past.astype(bf16), pv_ref[h].astype(bf16))
               + _dot(p_new.astype(bf16), v_ref[:, cols].astype(bf16)))
        o_ref[:, cols] = (acc / l).astype(o_ref.dtype)


def _sample_attn_call(body, name, q, k, v, pk, pv, layer, extra, o_buf, row0, ts):
    _, B, n_heads, past, hd = pk.shape
    blk0 = row0 // ts
    new_spec = pl.BlockSpec((ts, n_heads * hd), lambda b: (blk0 + b, 0))
    past_spec = pl.BlockSpec((None, None, n_heads, past, hd), lambda b: (layer, b, 0, 0, 0))
    in_specs = [new_spec, new_spec, new_spec, past_spec, past_spec]
    args = [q, k, v, pk, pv]
    if extra is not None:
        in_specs.append(pl.BlockSpec((None,) + extra.shape[1:], lambda b: (b, 0, 0)))
        args.append(extra)
    in_specs.append(pl.BlockSpec(memory_space=pl.ANY))
    args.append(o_buf)
    return pl.pallas_call(
        functools.partial(body, past=past, ts=ts, n_heads=n_heads, hd=hd),
        grid=(B,),
        in_specs=in_specs,
        out_specs=new_spec,
        out_shape=jax.ShapeDtypeStruct(o_buf.shape, o_buf.dtype),
        input_output_aliases={len(args) - 1: 0},
        compiler_params=_params(1),
        name=name,
    )(*args)


def _fox_sample(q, k, v, pk, pv, layer, cf, o_buf, row0, ts):
    return _sample_attn_call(_fox_sample_body, "fox_sample", q, k, v, pk, pv, layer, cf, o_buf, row0, ts)


def _tri2(n):
    tri = _tri(n, n, lambda r, c: r >= c)
    return jnp.concatenate([tri, tri], axis=0)


def _sb_scores(q, k, tri2, valid):
    z = _dot_nt(q, k)
    sp = jnp.maximum(z, 0.0) + jnp.log(1.0 + jnp.exp2(-jnp.abs(z))) * LOG2E
    if valid is not None:
        sp = jnp.where(valid, sp, 0.0)
    cum = _dot(jnp.concatenate(_split2(sp), axis=1), tri2)
    return z - cum, cum[:, 0:1]


def _sb_out(zc, run, v, valid):
    a = jnp.exp2(zc)
    if valid is not None:
        a = jnp.where(valid, a, 0.0)
    return jnp.exp2(-run) * _dot(a.astype(bf16), v)


def _sb_pair(q, right, left, tri2, run, acc, valid_right=None, valid_left=None):
    (k_r, v_r), (k_l, v_l) = right, left
    zc_r, tot_r = _sb_scores(q, k_r, tri2, valid_right)
    zc_l, tot_l = _sb_scores(q, k_l, tri2, valid_left)
    acc = acc + _sb_out(zc_r, run, v_r, valid_right) + _sb_out(zc_l, run + tot_r, v_l, valid_left)
    return run + tot_r + tot_l, acc


def _sb_prompt_body(q_ref, k16, v16, o_ref, tri2_ref, *, tq, tk):
    qi = pl.program_id(1)

    @pl.when(qi == 0)
    def _():
        tri2_ref[...] = _tri2(tk)

    q = q_ref[...]
    tri2 = tri2_ref[...]

    def kv(kb):
        rows = pl.ds(pl.multiple_of(kb * tk, tk), tk)
        return k16[rows, :], v16[rows, :]

    r = lax.broadcasted_iota(jnp.int32, (tq, tk), 0)
    c = lax.broadcasted_iota(jnp.int32, (tq, tk), 1)
    init = (jnp.zeros((tq, 1), f32), jnp.zeros((tq, q.shape[1]), f32))
    carry = _sb_pair(q, kv(2 * qi + 1), kv(2 * qi), tri2, *init, valid_right=c + tk < r, valid_left=c < r)

    def all_rows_spent(run):
        return (jnp.min(run) >= SKIP_BITS).astype(jnp.int32)

    def below(c):
        i, run, acc, _ = c
        k_t, v_t = kv(2 * qi - 1 - i)
        zc, tot = _sb_scores(q, k_t, tri2, None)
        acc = acc + _sb_out(zc, run, v_t, None)
        return i + 1, run + tot, acc, all_rows_spent(run + tot)

    _, _, acc, _ = lax.while_loop(lambda c: jnp.logical_and(c[0] < 2 * qi, c[3] == 0), below,
                                  (jnp.int32(0), *carry, all_rows_spent(carry[0])))
    o_ref[...] = acc.astype(o_ref.dtype)


def _sb_prompt(q, k, v, t_prompt, n_heads, hd, tq):
    M = q.shape[0]
    tk = tq // 2
    return pl.pallas_call(
        functools.partial(_sb_prompt_body, tq=tq, tk=tk),
        grid=(n_heads, t_prompt // tq),
        in_specs=[pl.BlockSpec((tq, hd), lambda h, i: (i, h)),
                  pl.BlockSpec((t_prompt, hd), lambda h, i: (0, h)),
                  pl.BlockSpec((t_prompt, hd), lambda h, i: (0, h))],
        out_specs=pl.BlockSpec((tq, hd), lambda h, i: (i, h)),
        out_shape=jax.ShapeDtypeStruct((M, n_heads * hd), bf16),
        scratch_shapes=[pltpu.VMEM((2 * tk, tk), bf16)],
        compiler_params=_params(2),
        name="sb_prompt",
    )(q, k, v)


def _sb_sample_body(q_ref, k_ref, v_ref, pk_ref, pv_ref, _, o_ref, *, past, ts, n_heads, hd, tp):
    tri2_new = _tri2(ts)
    tri2 = _tri2(tp)
    strict = _causal(ts, strict=True)
    for h in range(n_heads):
        cols = slice(h * hd, (h + 1) * hd)
        q = q_ref[:, cols]
        zc, run = _sb_scores(q, k_ref[:, cols].astype(bf16), tri2_new, strict)
        acc = _sb_out(zc, jnp.zeros((ts, 1), f32), v_ref[:, cols].astype(bf16), strict)

        def past_kv(p0):
            return pk_ref[h, p0:p0 + tp, :].astype(bf16), pv_ref[h, p0:p0 + tp, :].astype(bf16)

        for p0 in range(past - tp, -1, -2 * tp):
            run, acc = _sb_pair(q, past_kv(p0), past_kv(p0 - tp), tri2, run, acc)
        o_ref[:, cols] = acc.astype(o_ref.dtype)


def _sb_sample(q, k, v, pk, pv, layer, o_buf, row0, ts, tp):
    return _sample_attn_call(functools.partial(_sb_sample_body, tp=tp), "sb_sample",
                             q, k, v, pk, pv, layer, None, o_buf, row0, ts)


def _mem_attn_body(q_ref, k_ref, v_ref, *rest, n_heads, hd):
    o_ref = rest[-1]
    for h in range(n_heads):
        sl = slice(h * hd, (h + 1) * hd)
        head = (lambda ref: ref[:, h, :]) if len(k_ref.shape) == 3 else (lambda ref: ref[:, sl])
        s = _dot_nt(q_ref[:, sl], head(k_ref).astype(bf16))
        p = jnp.exp(s - jnp.max(s, axis=-1, keepdims=True))
        acc = _dot(p.astype(bf16), head(v_ref).astype(bf16))
        o_ref[:, sl] = (acc / jnp.sum(p, axis=-1, keepdims=True)).astype(o_ref.dtype)


def _mem_attn_prompt(qm, mk, mv, t_prompt, tq, n_heads, hd):
    M, W = qm.shape
    Mm = mk.shape[0]
    return pl.pallas_call(
        functools.partial(_mem_attn_body, n_heads=n_heads, hd=hd),
        grid=(t_prompt // tq,),
        in_specs=[pl.BlockSpec((tq, W), lambda i: (i, 0)),
                  pl.BlockSpec((Mm, W), lambda i: (0, 0)),
                  pl.BlockSpec((Mm, W), lambda i: (0, 0))],
        out_specs=pl.BlockSpec((tq, W), lambda i: (i, 0)),
        out_shape=jax.ShapeDtypeStruct((M, W), bf16),
        compiler_params=_params(1),
        name="mem_attn_prompt",
    )(qm, mk, mv)


def _mem_attn_sample(qm, mk, mv, layer, o_buf, row0, ts):
    M, W = qm.shape
    _, B, Mm, n_heads, hd = mk.shape
    blk0 = row0 // ts
    row_spec = pl.BlockSpec((ts, W), lambda b: (blk0 + b, 0))
    mem_spec = pl.BlockSpec((None, None, Mm, n_heads, hd), lambda b: (layer, b, 0, 0, 0))
    return pl.pallas_call(
        functools.partial(_mem_attn_body, n_heads=n_heads, hd=hd),
        grid=(B,),
        in_specs=[row_spec, mem_spec, mem_spec, pl.BlockSpec(memory_space=pl.ANY)],
        out_specs=row_spec,
        out_shape=jax.ShapeDtypeStruct((M, W), bf16),
        input_output_aliases={3: 0},
        compiler_params=_params(1),
        name="mem_attn_sample",
    )(qm, mk, mv, o_buf)


def _gla_chunk(rows, h, log_a, tril, q_ref, k_ref, v_ref, r_ref, br_ref, ng_ref, o_ref, S_h, *, dkp, dv):
    C, U = GLA_CHUNK, GLA_SUB
    ks = slice(h * dkp, (h + 1) * dkp)
    vs = slice(h * dv, (h + 1) * dv)
    row_in_sub = lax.broadcasted_iota(jnp.int32, (U, 1), 0)
    hi, mid, lo = _split3(log_a[rows, ks])
    G = _dot(tril, hi) + _dot(tril, mid) + _dot(tril, lo)
    q = q_ref[rows, ks]
    k = k_ref[rows, ks]
    v = v_ref[rows, vs]
    v16 = v.astype(bf16)
    o_inter = _dot((q * jnp.exp(G)).astype(bf16), S_h.astype(bf16))

    o_rows = []
    for b in range(C // U):
        r0 = b * U
        Gb, qb = G[r0:r0 + U], q[r0:r0 + U]
        o_b = o_inter[r0:r0 + U]
        if b > 0:
            ref = G[r0 - 1:r0]
            qg = qb * jnp.exp(Gb - ref)
            kg = k[:r0] * jnp.exp(ref - G[:r0])
            a = _dot_nt(qg.astype(bf16), kg.astype(bf16))
            o_b = o_b + _dot(a.astype(bf16), v16[:r0])
        for j in range(U):
            s = r0 + j
            e = jnp.exp(jnp.minimum(Gb - G[s:s + 1], 0.0))
            a = jnp.sum(qb * e * k[s:s + 1], axis=-1, keepdims=True)
            a = jnp.where(row_in_sub >= j, a, 0.0)
            o_b = o_b + a * v[s:s + 1]
        o_rows.append(o_b)
    o = jnp.concatenate(o_rows, axis=0)

    on = o * lax.rsqrt(jnp.mean(o * o, axis=-1, keepdims=True) + RMS_EPS) * ng_ref[:, vs]
    x = r_ref[rows, vs] + br_ref[:, vs]
    o_ref[rows, vs] = (on * (x / (1.0 + jnp.exp(-x)))).astype(o_ref.dtype)

    kl = k * jnp.exp(G[C - 1:C] - G)
    decay_col = jnp.exp(G.T[:, C - 1:C])
    return decay_col * S_h + _dot_tn(kl.astype(bf16), v16)


def _gla_body(q_ref, k_ref, v_ref, r_ref, gl_ref, w2_ref, bg_ref, br_ref, ng_ref, s0_ref,
              o_ref, sp_ref, ss_ref, S, *, n_prompt_steps, n_heads, dk, dkp, dv):
    c = pl.program_id(0)
    C = GLA_CHUNK
    is_sample = c >= n_prompt_steps

    @pl.when(c == 0)
    def _():
        S[...] = jnp.zeros_like(S)

    log_a = _log_sigmoid(_dot(gl_ref[...].astype(bf16), w2_ref[...]) + bg_ref[...]) * (1.0 / GLA_GATE_TEMP)
    tril = _tri(C, C, lambda r, cc: r >= cc)
    refs = (q_ref, k_ref, v_ref, r_ref, br_ref, ng_ref, o_ref)

    for h in range(n_heads):
        S_h = S_prompt = S[h]
        for ci in range(GLA_STEP_CHUNKS):
            s0 = jnp.concatenate([s0_ref[ci, h], jnp.zeros((dkp - dk, dv), f32)], axis=0)
            S_h = _gla_chunk(slice(ci * C, (ci + 1) * C), h, log_a, tril, *refs,
                             jnp.where(is_sample, s0, S_h), dkp=dkp, dv=dv)
            ss_ref[ci, h] = S_h[:dk]
        S_prompt = jnp.where(is_sample, S_prompt, S_h)
        S[h] = S_prompt
        sp_ref[h] = S_prompt[:dk]


def _gla(q, k, v, r, glow, w2, bg, br, ng, s0, t_prompt, n_heads, dk, dkp, dv):
    M = q.shape[0]
    rows = GLA_CHUNK * GLA_STEP_CHUNKS
    B = s0.shape[0]
    n_p = t_prompt // rows
    row = lambda width: pl.BlockSpec((rows, width), lambda c: (c, 0))
    const = lambda a: pl.BlockSpec(a.shape, lambda c: (0,) * a.ndim)
    sample_state = pl.BlockSpec((GLA_STEP_CHUNKS, n_heads, dk, dv), lambda c: (jnp.maximum(c - n_p, 0), 0, 0, 0))
    return pl.pallas_call(
        functools.partial(_gla_body, n_prompt_steps=n_p, n_heads=n_heads, dk=dk, dkp=dkp, dv=dv),
        grid=(M // rows,),
        in_specs=[row(n_heads * dkp), row(n_heads * dkp), row(n_heads * dv), row(n_heads * dv),
                  row(glow.shape[1]), const(w2), const(bg), const(br), const(ng), sample_state],
        out_specs=[row(n_heads * dv), pl.BlockSpec((n_heads, dk, dv), lambda c: (0, 0, 0)), sample_state],
        out_shape=[jax.ShapeDtypeStruct((M, n_heads * dv), bf16),
                   jax.ShapeDtypeStruct((n_heads, dk, dv), f32),
                   jax.ShapeDtypeStruct((B, n_heads, dk, dv), f32)],
        scratch_shapes=[pltpu.VMEM((n_heads, dkp, dv), f32)],
        compiler_params=_params(1),
        name="gla",
    )(q, k, v, r, glow, w2, bg, br, ng, s0)


def _out_proj_body(o_ref, om_ref, x_ref, w_ref, y_ref, *, wo):
    y_ref[...] = x_ref[...] + _dot(o_ref[...], w_ref[:wo, :]) + _dot(om_ref[...], w_ref[wo:, :])


def _out_proj(o, om, x, w, tm):
    M, D = x.shape
    wo, wm = o.shape[1], om.shape[1]
    return pl.pallas_call(
        functools.partial(_out_proj_body, wo=wo),
        grid=(M // tm,),
        in_specs=[pl.BlockSpec((tm, wo), lambda i: (i, 0)),
                  pl.BlockSpec((tm, wm), lambda i: (i, 0)),
                  pl.BlockSpec((tm, D), lambda i: (i, 0)),
                  pl.BlockSpec((wo + wm, D), lambda i: (0, 0))],
        out_specs=pl.BlockSpec((tm, D), lambda i: (i, 0)),
        out_shape=jax.ShapeDtypeStruct((M, D), f32),
        compiler_params=_params(1),
        name="out_proj",
    )(o, om, x, w)


def _mlp_body(x_ref, g_ref, wu_ref, wd_ref, gf_ref, y_ref, h_ref, *, final_norm):
    j = pl.program_id(1)

    @pl.when(j == 0)
    def _():
        x = x_ref[...]
        h_ref[...] = _rms(x, g_ref[...]).astype(bf16)
        y_ref[...] = x

    u = jnp.maximum(_dot(h_ref[...], wu_ref[...]), 0.0)
    y_ref[...] += _dot((u * u).astype(bf16), wd_ref[...])

    if final_norm:
        @pl.when(j == pl.num_programs(1) - 1)
        def _():
            y_ref[...] = _rms(y_ref[...], gf_ref[...])


def _mlp(x, g, wu, wd, layer, gf, final_norm, tm, tf):
    M, D = x.shape
    F = wu.shape[2]
    return pl.pallas_call(
        functools.partial(_mlp_body, final_norm=final_norm),
        grid=(M // tm, F // tf),
        in_specs=[pl.BlockSpec((tm, D), lambda i, j: (i, 0)),
                  pl.BlockSpec((1, D), lambda i, j: (0, 0)),
                  pl.BlockSpec((None, D, tf), lambda i, j: (layer, 0, j)),
                  pl.BlockSpec((None, tf, D), lambda i, j: (layer, j, 0)),
                  pl.BlockSpec((1, D), lambda i, j: (0, 0))],
        out_specs=pl.BlockSpec((tm, D), lambda i, j: (i, 0)),
        out_shape=jax.ShapeDtypeStruct((M, D), f32),
        scratch_shapes=[pltpu.VMEM((tm, D), bf16)],
        compiler_params=_params(2),
        name="mlp",
    )(x, g.reshape(1, D), wu, wd, gf.reshape(1, D))


def _largest_divisor(n, cap, mult):
    best = None
    for d in range(mult, min(n, cap) + 1, mult):
        if n % d == 0:
            best = d
    assert best is not None, (n, cap, mult)
    return best


def kernel(x_prompt, x_sample, cache_fox_k, cache_fox_v, cache_fox_logf, cache_sb_k, cache_sb_v, state_gla,
           cache_mem_k, cache_mem_v, mem_prompt, norm_mix_g, norm_mlp_g, norm_mem_g, norm_final_g, w_mem_kv,
           w_in_fox, b_forget, w_out_fox, w_in_sb, w_out_sb, w_in_gla, w_gate2_gla, b_gate_gla, b_outgate_gla,
           norm_gla_g, w_out_gla, w_up, w_down):
    Bp, Tp0, D = x_prompt.shape
    Bs, Ts, _ = x_sample.shape
    assert Bp == 1, "the prompt group is handled as one sequence"
    Tp = Bp * Tp0
    M = Tp + Bs * Ts
    depth = norm_mix_g.shape[0]
    H, hd = cache_fox_k.shape[-2:]
    aw = H * hd
    past = cache_fox_k.shape[2]
    Mm, MH = cache_mem_k.shape[2], cache_mem_k.shape[3]
    mw = MH * hd
    GH, dk, dv = state_gla.shape[-3:]
    dkp = -(-dk // LANES) * LANES
    rank = w_gate2_gla.shape[1]
    C = GLA_CHUNK
    assert Ts == C and Tp % (C * GLA_STEP_CHUNKS) == 0 and Bs % GLA_STEP_CHUNKS == 0 and M % 16 == 0

    tm = _largest_divisor(M, 512, 16)
    tm_st = _largest_divisor(math.gcd(Tp, M - Tp), 512, Ts)
    tm_mlp = _largest_divisor(M, 576, 16)
    tf = _largest_divisor(w_up.shape[2], 512, LANES)
    ta = _largest_divisor(Tp, 512, 2 * LANES)
    tg = _largest_divisor(Tp, 512, LANES)
    tq_mem = _largest_divisor(Tp, 512, 16)
    tp_sb = _largest_divisor(past // 2, 256, LANES)
    att_scale = hd ** -0.5
    att_scale2 = att_scale * LOG2E

    x = jnp.concatenate([x_prompt.reshape(Tp, D), x_sample.reshape(Bs * Ts, D)], axis=0)
    mk_p, mv_p = _memkv(mem_prompt.reshape(Mm, D), norm_mem_g, w_mem_kv)
    wu16, wd16 = w_up.astype(bf16), w_down.astype(bf16)
    w16_fox, w16_sb = w_in_fox.astype(bf16), w_in_sb.astype(bf16)

    n_fox, n_sb = w_in_fox.shape[0], w_in_sb.shape[0]
    kv_state = {0: None, 1: None}
    logf_p, logf_s, gla_p, gla_s = [], [], [], []
    for i in range(depth):
        kind, j = i % 3, i // 3
        g = norm_mix_g[i]
        if kind in (0, 1):
            w = (w_in_fox if kind == 0 else w_in_sb)[j]
            w16 = w16_fox if kind == 0 else w16_sb
            n_gate = H if kind == 0 else 0
            (q,) = _proj(x, g, w16, [(0, aw, bf16, att_scale2)], tm, w_block=(j, 0, aw))
            n_layers = n_fox if kind == 0 else n_sb
            st = kv_state[kind]
            k, *st_k = _proj_state(x, g, w16, (j, 1, aw), j, st[0] if st else n_layers, Tp, Ts, tm_st, H, hd)
            v, *st_v = _proj_state(x, g, w16, (j, 2, aw), j, st[1] if st else n_layers, Tp, Ts, tm_st, H, hd)
            kv_state[kind] = (st_k, st_v)
            (qm,) = _proj(x, g, w[:, 3 * aw + n_gate:].astype(bf16), [(0, mw, bf16, att_scale)], tm)
            if kind == 0:
                wf_t = jnp.pad(w[:, 3 * aw:3 * aw + H].T, ((0, FOX_GATE_ROWS - H), (0, 0))).astype(bf16)
                bf_col = jnp.pad(b_forget[j], (0, FOX_GATE_ROWS - H)).reshape(FOX_GATE_ROWS, 1)
                lf_p, cf_p = _fox_gate_prompt(x, g, wf_t, bf_col, Tp, tg)
                plogf_t = jnp.pad(jnp.swapaxes(cache_fox_logf[j], 1, 2), ((0, 0), (0, FOX_GATE_ROWS - H), (0, 0)))
                lf_s, cf_s = _fox_gate_sample(x, g, wf_t, bf_col, plogf_t, Tp, Ts)
                o = _fox_prompt(q, k, v, cf_p, Tp, H, hd, ta)
                o = _fox_sample(q, k, v, jnp.swapaxes(cache_fox_k, 2, 3), jnp.swapaxes(cache_fox_v, 2, 3), j,
                                cf_s, o, Tp, Ts)
                logf_p.append(lf_p[:H].T.reshape(Bp, Tp0, H))
                logf_s.append(jnp.swapaxes(lf_s[:, :H], 1, 2))
                w_out = w_out_fox[j]
            else:
                o = _sb_prompt(q, k, v, Tp, H, hd, ta)
                o = _sb_sample(q, k, v, jnp.swapaxes(cache_sb_k, 2, 3), jnp.swapaxes(cache_sb_v, 2, 3), j,
                               o, Tp, Ts, tp_sb)
                w_out = w_out_sb[j]
        else:
            w = w_in_gla[j]
            kw, vw = GH * dk, GH * dv

            def pad_heads(a):
                a = a.reshape(a.shape[:-1] + (GH, dk))
                a = jnp.pad(a, [(0, 0)] * (a.ndim - 1) + [(0, dkp - dk)])
                return a.reshape(a.shape[:-2] + (GH * dkp,))

            wqk = jnp.concatenate([pad_heads(w[:, :kw]), pad_heads(w[:, kw:2 * kw])], axis=1).astype(bf16)
            wv = w[:, 2 * kw:2 * kw + vw].astype(bf16)
            wr = w[:, 2 * kw + vw:2 * kw + 2 * vw].astype(bf16)
            c0 = 2 * kw + 2 * vw
            wmg = jnp.concatenate([w[:, c0 + rank:], jnp.pad(w[:, c0:c0 + rank], ((0, 0), (0, LANES - rank)))],
                                  axis=1).astype(bf16)
            q, k = _proj(x, g, wqk, [(0, GH * dkp, f32, dk ** -0.5), (GH * dkp, 2 * GH * dkp, f32, 1.0)], tm)
            (v,) = _proj(x, g, wv, [(0, vw, f32, 1.0)], tm)
            (r,) = _proj(x, g, wr, [(0, vw, f32, 1.0)], tm)
            qm, glow = _proj(x, g, wmg, [(0, mw, bf16, att_scale), (mw, mw + LANES, f32, 1.0)], tm)
            w2 = jnp.pad(pad_heads(w_gate2_gla[j]), ((0, LANES - rank), (0, 0))).astype(bf16)
            bg = pad_heads(b_gate_gla[j]).reshape(1, GH * dkp)
            o, st_p, st_s = _gla(q, k, v, r, glow, w2, bg, b_outgate_gla[j].reshape(1, vw),
                                 norm_gla_g[j].reshape(1, vw), state_gla[j], Tp, GH, dk, dkp, dv)
            gla_p.append(st_p[None])
            gla_s.append(st_s)
            w_out = w_out_gla[j]

        om = _mem_attn_prompt(qm, mk_p[i], mv_p[i], Tp, tq_mem, MH, hd)
        om = _mem_attn_sample(qm, cache_mem_k, cache_mem_v, i, om, Tp, Ts)
        x = _out_proj(o, om, x, w_out.astype(bf16), tm)
        x = _mlp(x, norm_mlp_g[i], wu16, wd16, i, norm_final_g, i == depth - 1, tm_mlp, tf)

    def prompt_state(a):
        return jnp.swapaxes(a.reshape((a.shape[0], Bp, H, Tp0, hd)), 2, 3)

    def sample_state(a):
        return jnp.swapaxes(a, 2, 3)

    (fox_k, fox_v), (sb_k, sb_v) = kv_state[0], kv_state[1]
    return (x[:Tp].reshape(Bp, Tp0, D), x[Tp:].reshape(Bs, Ts, D),
            prompt_state(fox_k[0]), prompt_state(fox_v[0]), jnp.stack(logf_p),
            prompt_state(sb_k[0]), prompt_state(sb_v[0]),
            jnp.stack(gla_p),
            mk_p.reshape(depth, Bp, Mm, MH, hd), mv_p.reshape(depth, Bp, Mm, MH, hd),
            sample_state(fox_k[1]), sample_state(fox_v[1]), jnp.stack(logf_s),
            sample_state(sb_k[1]), sample_state(sb_v[1]),
            jnp.stack(gla_s))
```

```python
import functools
import math

import jax
import jax.numpy as jnp
from jax import lax
from jax.experimental import pallas as pl
from jax.experimental.pallas import tpu as pltpu

f32 = jnp.float32
bf16 = jnp.bfloat16

RMS_EPS = 1e-6
GLA_GATE_TEMP = 16.0
GLA_CHUNK = 64
GLA_SUB = 8
GLA_STEP_CHUNKS = 2
LOG2E = 1.4426950408889634
MASKED_LOGIT = -1e30
SKIP_BITS = 160.0
LANES = 128
FOX_GATE_ROWS = 16
VMEM_LIMIT_BYTES = 56 * 1024 * 1024


def _params(n_axes):
    return pltpu.CompilerParams(dimension_semantics=("arbitrary",) * n_axes,
                                vmem_limit_bytes=VMEM_LIMIT_BYTES)


def _rms(x, g):
    return x * lax.rsqrt(jnp.mean(x * x, axis=-1, keepdims=True) + RMS_EPS) * g


def _log_sigmoid(z):
    return jnp.minimum(z, 0.0) - jnp.log1p(jnp.exp(-jnp.abs(z)))


def _split3(x):
    hi = x.astype(bf16)
    r = x - hi.astype(f32)
    mid = r.astype(bf16)
    lo = (r - mid.astype(f32)).astype(bf16)
    return hi, mid, lo


def _split2(x):
    hi = x.astype(bf16)
    lo = (x - hi.astype(f32)).astype(bf16)
    return hi, lo


def _tri(n, m, fn):
    r = lax.broadcasted_iota(jnp.int32, (n, m), 0)
    c = lax.broadcasted_iota(jnp.int32, (n, m), 1)
    return jnp.where(fn(r, c), 1.0, 0.0).astype(bf16)


def _dot(a, b):
    return jnp.dot(a, b, preferred_element_type=f32)


def _dot_nt(a, b):
    return lax.dot_general(a, b, (((1,), (1,)), ((), ())), preferred_element_type=f32)


def _dot_tn(a, b):
    return lax.dot_general(a, b, (((0,), (0,)), ((), ())), preferred_element_type=f32)


def _cumsum_lanes(x, carry, tri):
    n = x.shape[1]
    outs = []
    for b0 in range(0, n, LANES):
        w = min(LANES, n - b0)
        hi, mid, lo = _split3(x[:, b0:b0 + w])
        t = tri[:w, :w]
        c = _dot(hi, t) + _dot(mid, t) + _dot(lo, t) + carry
        carry = c[:, w - 1:w]
        outs.append(c)
    return outs, carry


def _proj_body(x_ref, g_ref, w_ref, *o_refs, cols, scales):
    h = _rms(x_ref[...], g_ref[...]).astype(bf16)
    y = _dot(h, w_ref[...])
    for o_ref, (c0, c1), sc in zip(o_refs, cols, scales):
        part = y[:, c0:c1]
        if sc != 1.0:
            part = part * sc
        o_ref[...] = part.astype(o_ref.dtype)


def _proj(x, g, w, outs, tm, w_block=None):
    M, D = x.shape
    if w_block is None:
        w_spec = pl.BlockSpec(w.shape, lambda i: (0, 0))
    else:
        layer, col_block, width = w_block
        w_spec = pl.BlockSpec((None, D, width), lambda i: (layer, 0, col_block))
    cols = tuple((c0, c1) for c0, c1, _, _ in outs)
    scales = tuple(float(s) for _, _, _, s in outs)
    return pl.pallas_call(
        functools.partial(_proj_body, cols=cols, scales=scales),
        grid=(M // tm,),
        in_specs=[pl.BlockSpec((tm, D), lambda i: (i, 0)),
                  pl.BlockSpec((1, D), lambda i: (0, 0)),
                  w_spec],
        out_specs=[pl.BlockSpec((tm, c1 - c0), lambda i: (i, 0)) for c0, c1 in cols],
        out_shape=[jax.ShapeDtypeStruct((M, c1 - c0), dt) for c0, c1, dt, _ in outs],
        compiler_params=_params(1),
        name="norm_proj",
    )(x, g.reshape(1, D), w)


def _proj_state_body(x_ref, g_ref, w_ref, *refs, n_prompt_blocks, n_heads, hd, ts):
    a_ref, sp_ref, ss_ref = refs[-3:]
    i = pl.program_id(0)
    h = _rms(x_ref[...], g_ref[...]).astype(bf16)
    y = _dot(h, w_ref[...])
    a_ref[...] = y.astype(a_ref.dtype)

    @pl.when(i < n_prompt_blocks)
    def _():
        for hh in range(n_heads):
            sp_ref[hh] = y[:, hh * hd:(hh + 1) * hd]

    @pl.when(i >= n_prompt_blocks)
    def _():
        for b in range(ss_ref.shape[0]):
            for hh in range(n_heads):
                ss_ref[b, hh] = y[b * ts:(b + 1) * ts, hh * hd:(hh + 1) * hd]


def _proj_state(x, g, w, w_block, slot, prev, t_prompt, ts, tm, n_heads, hd):
    M, D = x.shape
    layer, col_block, width = w_block
    n_p = t_prompt // tm
    in_specs = [pl.BlockSpec((tm, D), lambda i: (i, 0)),
                pl.BlockSpec((1, D), lambda i: (0, 0)),
                pl.BlockSpec((None, D, width), lambda i: (layer, 0, col_block))]
    args = [x, g.reshape(1, D), w]
    if isinstance(prev, int):
        shapes = [jax.ShapeDtypeStruct((prev, n_heads, t_prompt, hd), f32),
                  jax.ShapeDtypeStruct((prev, (M - t_prompt) // ts, n_heads, ts, hd), f32)]
        aliases = {}
    else:
        shapes = [jax.ShapeDtypeStruct(p.shape, p.dtype) for p in prev]
        in_specs += [pl.BlockSpec(memory_space=pl.ANY)] * 2
        args += list(prev)
        aliases = {3: 1, 4: 2}
    return pl.pallas_call(
        functools.partial(_proj_state_body, n_prompt_blocks=n_p, n_heads=n_heads, hd=hd, ts=ts),
        grid=(M // tm,),
        in_specs=in_specs,
        out_specs=[pl.BlockSpec((tm, width), lambda i: (i, 0)),
                   pl.BlockSpec((None, n_heads, tm, hd), lambda i: (slot, 0, jnp.minimum(i, n_p - 1), 0)),
                   pl.BlockSpec((None, tm // ts, n_heads, ts, hd),
                                lambda i: (slot, jnp.maximum(i - n_p, 0), 0, 0, 0))],
        out_shape=[jax.ShapeDtypeStruct((M, width), bf16)] + shapes,
        input_output_aliases=aliases,
        compiler_params=_params(1),
        name="norm_proj_state",
    )(*args)


def _memkv_body(m_ref, g_ref, w_ref, k_ref, v_ref, *, mw):
    h = _rms(m_ref[...], g_ref[...]).astype(bf16)
    w = w_ref[...].astype(bf16)
    k_ref[...] = _dot(h, w[:, :mw])
    v_ref[...] = _dot(h, w[:, mw:])


def _memkv(mem, g, w):
    L, D, two_mw = w.shape
    Mm = mem.shape[0]
    mw = two_mw // 2
    return pl.pallas_call(
        functools.partial(_memkv_body, mw=mw),
        grid=(L,),
        in_specs=[pl.BlockSpec((Mm, D), lambda l: (0, 0)),
                  pl.BlockSpec((None, 1, D), lambda l: (l, 0, 0)),
                  pl.BlockSpec((None, D, two_mw), lambda l: (l, 0, 0))],
        out_specs=[pl.BlockSpec((None, Mm, mw), lambda l: (l, 0, 0))] * 2,
        out_shape=[jax.ShapeDtypeStruct((L, Mm, mw), f32)] * 2,
        compiler_params=_params(1),
        name="mem_kv",
    )(mem, g.reshape(L, 1, D), w)


def _gate_logits(x_ref, g_ref, wf_ref, bf_ref):
    h = _rms(x_ref[...], g_ref[...]).astype(bf16)
    return _log_sigmoid(_dot_nt(wf_ref[...], h) + bf_ref[...])


def _fox_gate_prompt_body(x_ref, g_ref, wf_ref, bf_ref, lf_ref, cf_ref, carry_ref):
    @pl.when(pl.program_id(0) == 0)
    def _():
        carry_ref[...] = jnp.zeros_like(carry_ref)

    logf = _gate_logits(x_ref, g_ref, wf_ref, bf_ref)
    lf_ref[...] = logf
    tri = _tri(LANES, LANES, lambda r, c: r <= c)
    outs, carry = _cumsum_lanes(logf, carry_ref[:, 0:1], tri)
    for b, c in enumerate(outs):
        cf_ref[:, b * LANES:(b + 1) * LANES] = c * LOG2E
    carry_ref[...] = jnp.broadcast_to(carry, carry_ref.shape)


def _fox_gate_prompt(x, g, wf_t, bf, t_prompt, tg):
    D = x.shape[1]
    R = wf_t.shape[0]
    return pl.pallas_call(
        _fox_gate_prompt_body,
        grid=(t_prompt // tg,),
        in_specs=[pl.BlockSpec((tg, D), lambda i: (i, 0)),
                  pl.BlockSpec((1, D), lambda i: (0, 0)),
                  pl.BlockSpec((R, D), lambda i: (0, 0)),
                  pl.BlockSpec((R, 1), lambda i: (0, 0))],
        out_specs=[pl.BlockSpec((R, tg), lambda i: (0, i))] * 2,
        out_shape=[jax.ShapeDtypeStruct((R, t_prompt), f32)] * 2,
        scratch_shapes=[pltpu.VMEM((R, LANES), f32)],
        compiler_params=_params(1),
        name="fox_gate_prompt",
    )(x, g.reshape(1, D), wf_t, bf)


def _fox_gate_sample_body(x_ref, g_ref, wf_ref, bf_ref, pl_ref, lf_ref, cf_ref, *, past, ts):
    logf = _gate_logits(x_ref, g_ref, wf_ref, bf_ref)
    lf_ref[...] = logf
    tri = _tri(LANES, LANES, lambda r, c: r <= c)
    zero = jnp.zeros((logf.shape[0], 1), f32)
    outs, carry = _cumsum_lanes(pl_ref[...], zero, tri)
    for b, c in enumerate(outs):
        cf_ref[:, b * LANES:(b + 1) * LANES] = c * LOG2E
    (new,), _ = _cumsum_lanes(logf, carry, tri)
    cf_ref[:, past:past + ts] = new * LOG2E
    cf_ref[:, past + ts:] = jnp.zeros((logf.shape[0], LANES - ts), f32)


def _fox_gate_sample(x, g, wf_t, bf, plogf_t, row0, ts):
    D = x.shape[1]
    R = wf_t.shape[0]
    B, _, past = plogf_t.shape
    blk0 = row0 // ts
    return pl.pallas_call(
        functools.partial(_fox_gate_sample_body, past=past, ts=ts),
        grid=(B,),
        in_specs=[pl.BlockSpec((ts, D), lambda b: (blk0 + b, 0)),
                  pl.BlockSpec((1, D), lambda b: (0, 0)),
                  pl.BlockSpec((R, D), lambda b: (0, 0)),
                  pl.BlockSpec((R, 1), lambda b: (0, 0)),
                  pl.BlockSpec((None, R, past), lambda b: (b, 0, 0))],
        out_specs=[pl.BlockSpec((None, R, ts), lambda b: (b, 0, 0)),
                   pl.BlockSpec((None, R, past + LANES), lambda b: (b, 0, 0))],
        out_shape=[jax.ShapeDtypeStruct((B, R, ts), f32),
                   jax.ShapeDtypeStruct((B, R, past + LANES), f32)],
        compiler_params=_params(1),
        name="fox_gate_sample",
    )(x, g.reshape(1, D), wf_t, bf, plogf_t)


def _softmax_tile(s, m, l, acc, v):
    m_new = jnp.maximum(m, jnp.max(s, axis=-1, keepdims=True))
    alpha = jnp.exp2(m - m_new)
    p = jnp.exp2(s - m_new)
    l = alpha * l + jnp.sum(p, axis=-1, keepdims=True)
    acc = alpha * acc + _dot(p.astype(bf16), v)
    return m_new, l, acc


def _causal(n, strict):
    r = lax.broadcasted_iota(jnp.int32, (n, n), 0)
    c = lax.broadcasted_iota(jnp.int32, (n, n), 1)
    return c < r if strict else c <= r


def _fox_prompt_body(q_ref, k16, v16, f_ref, o_ref, knorm_ref, *, t):
    qi = pl.program_id(1)

    @pl.when(qi == 0)
    def _():
        kf = k16[...].astype(f32)
        k_sq = jnp.max(jnp.sum(kf * kf, axis=-1, keepdims=True), axis=0, keepdims=True)
        knorm_ref[...] = jnp.broadcast_to(jnp.sqrt(k_sq), knorm_ref.shape)

    q = q_ref[...]
    qf = q.astype(f32)
    qk_bound = jnp.sqrt(jnp.sum(qf * qf, axis=-1, keepdims=True)) * knorm_ref[0:1, 0:1]
    col_minus_row = (lax.broadcasted_iota(jnp.int32, (t, t), 1) - lax.broadcasted_iota(jnp.int32, (t, t), 0))

    def tile(kb, carry, max_col_minus_row):
        rows = pl.ds(pl.multiple_of(kb * t, t), t)
        s = _dot_nt(q, k16[rows, :]) - f_ref[pl.ds(kb, 1), :]
        if max_col_minus_row is not None:
            s = jnp.where(col_minus_row <= max_col_minus_row, s, MASKED_LOGIT)
        return _softmax_tile(s, *carry, v16[rows, :])

    def rest_is_zero(a, b, kb):
        bias_max = jnp.max(-f_ref[pl.ds(kb, 1), :])
        return (jnp.max(qk_bound + bias_max - jnp.maximum(a[0], b[0])) < -SKIP_BITS).astype(jnp.int32)

    init = (jnp.full((t, 1), MASKED_LOGIT, f32), jnp.zeros((t, 1), f32), jnp.zeros((t, q.shape[1]), f32))
    odd = qi % 2
    a = tile(qi, init, 0)
    b = lax.cond(odd == 1, lambda: tile(qi - 1, init, None), lambda: init)
    first = qi - 1 - odd
    n_pairs = qi // 2

    def pair(c):
        i, a, b, _ = c
        kb = first - 2 * i
        a = tile(kb, a, None)
        b = tile(kb - 1, b, None)
        return i + 1, a, b, rest_is_zero(a, b, jnp.maximum(kb - 2, 0))

    _, (m_a, l_a, acc_a), (m_b, l_b, acc_b), _ = lax.while_loop(
        lambda c: jnp.logical_and(c[0] < n_pairs, c[3] == 0), pair,
        (jnp.int32(0), a, b, rest_is_zero(a, b, jnp.maximum(first, 0))))
    m = jnp.maximum(m_a, m_b)
    w_a = jnp.exp2(m_a - m)
    w_b = jnp.exp2(m_b - m)
    o_ref[...] = ((w_a * acc_a + w_b * acc_b) / (w_a * l_a + w_b * l_b)).astype(o_ref.dtype)


def _fox_prompt(q, k, v, cf, t_prompt, n_heads, hd, t):
    M = q.shape[0]
    nq = t_prompt // t
    return pl.pallas_call(
        functools.partial(_fox_prompt_body, t=t),
        grid=(n_heads, nq),
        in_specs=[pl.BlockSpec((t, hd), lambda h, i: (i, h)),
                  pl.BlockSpec((t_prompt, hd), lambda h, i: (0, h)),
                  pl.BlockSpec((t_prompt, hd), lambda h, i: (0, h)),
                  pl.BlockSpec((None, nq, t), lambda h, i: (h, 0, 0))],
        out_specs=pl.BlockSpec((t, hd), lambda h, i: (i, h)),
        out_shape=jax.ShapeDtypeStruct((M, n_heads * hd), bf16),
        scratch_shapes=[pltpu.VMEM((8, LANES), f32)],
        compiler_params=_params(2),
        name="fox_prompt",
    )(q, k, v, cf.reshape(cf.shape[0], nq, t))


def _fox_sample_body(q_ref, k_ref, v_ref, pk_ref, pv_ref, f_ref, _, o_ref, *, past, ts, n_heads, hd):
    for h in range(n_heads):
        cols = slice(h * hd, (h + 1) * hd)
        q = q_ref[:, cols]
        f = f_ref[h:h + 1, :]
        s_past = _dot_nt(q, pk_ref[h].astype(bf16)) - f[:, :past]
        s_new = _dot_nt(q, k_ref[:, cols].astype(bf16)) - f[:, past:past + ts]
        s_new = jnp.where(_causal(ts, strict=False), s_new, -jnp.inf)
        m = jnp.maximum(jnp.max(s_past, axis=-1, keepdims=True), jnp.max(s_new, axis=-1, keepdims=True))
        p_past = jnp.exp2(s_past - m)
        p_new = jnp.exp2(s_new - m)
        l = jnp.sum(p_past, axis=-1, keepdims=True) + jnp.sum(p_new, axis=-1, keepdims=True)
        acc = (_dot(p_past.astype(bf16), pv_ref[h].astype(bf16))
               + _dot(p_new.astype(bf16), v_ref[:, cols].astype(bf16)))
        o_ref[:, cols] = (acc / l).astype(o_ref.dtype)


def _sample_attn_call(body, name, q, k, v, pk, pv, layer, extra, o_buf, row0, ts):
    _, B, n_heads, past, hd = pk.shape
    blk0 = row0 // ts
    new_spec = pl.BlockSpec((ts, n_heads * hd), lambda b: (blk0 + b, 0))
    past_spec = pl.BlockSpec((None, None, n_heads, past, hd), lambda b: (layer, b, 0, 0, 0))
    in_specs = [new_spec, new_spec, new_spec, past_spec, past_spec]
    args = [q, k, v, pk, pv]
    if extra is not None:
        in_specs.append(pl.BlockSpec((None,) + extra.shape[1:], lambda b: (b, 0, 0)))
        args.append(extra)
    in_specs.append(pl.BlockSpec(memory_space=pl.ANY))
    args.append(o_buf)
    return pl.pallas_call(
        functools.partial(body, past=past, ts=ts, n_heads=n_heads, hd=hd),
        grid=(B,),
        in_specs=in_specs,
        out_specs=new_spec,
        out_shape=jax.ShapeDtypeStruct(o_buf.shape, o_buf.dtype),
        input_output_aliases={len(args) - 1: 0},
        compiler_params=_params(1),
        name=name,
    )(*args)


def _fox_sample(q, k, v, pk, pv, layer, cf, o_buf, row0, ts):
    return _sample_attn_call(_fox_sample_body, "fox_sample", q, k, v, pk, pv, layer, cf, o_buf, row0, ts)


def _tri2(n):
    tri = _tri(n, n, lambda r, c: r >= c)
    return jnp.concatenate([tri, tri], axis=0)


def _sb_scores(q, k, tri2, valid):
    z = _dot_nt(q, k)
    sp = jnp.maximum(z, 0.0) + jnp.log(1.0 + jnp.exp2(-jnp.abs(z))) * LOG2E
    if valid is not None:
        sp = jnp.where(valid, sp, 0.0)
    cum = _dot(jnp.concatenate(_split2(sp), axis=1), tri2)
    return z - cum, cum[:, 0:1]


def _sb_out(zc, run, v, valid):
    a = jnp.exp2(zc)
    if valid is not None:
        a = jnp.where(valid, a, 0.0)
    return jnp.exp2(-run) * _dot(a.astype(bf16), v)


def _sb_pair(q, right, left, tri2, run, acc, valid_right=None, valid_left=None):
    (k_r, v_r), (k_l, v_l) = right, left
    zc_r, tot_r = _sb_scores(q, k_r, tri2, valid_right)
    zc_l, tot_l = _sb_scores(q, k_l, tri2, valid_left)
    acc = acc + _sb_out(zc_r, run, v_r, valid_right) + _sb_out(zc_l, run + tot_r, v_l, valid_left)
    return run + tot_r + tot_l, acc


def _sb_prompt_body(q_ref, k16, v16, o_ref, tri2_ref, *, tq, tk):
    qi = pl.program_id(1)

    @pl.when(qi == 0)
    def _():
        tri2_ref[...] = _tri2(tk)

    q = q_ref[...]
    tri2 = tri2_ref[...]

    def kv(kb):
        rows = pl.ds(pl.multiple_of(kb * tk, tk), tk)
        return k16[rows, :], v16[rows, :]

    r = lax.broadcasted_iota(jnp.int32, (tq, tk), 0)
    c = lax.broadcasted_iota(jnp.int32, (tq, tk), 1)
    init = (jnp.zeros((tq, 1), f32), jnp.zeros((tq, q.shape[1]), f32))
    carry = _sb_pair(q, kv(2 * qi + 1), kv(2 * qi), tri2, *init, valid_right=c + tk < r, valid_left=c < r)

    def all_rows_spent(run):
        return (jnp.min(run) >= SKIP_BITS).astype(jnp.int32)

    def below(c):
        i, run, acc, _ = c
        k_t, v_t = kv(2 * qi - 1 - i)
        zc, tot = _sb_scores(q, k_t, tri2, None)
        acc = acc + _sb_out(zc, run, v_t, None)
        return i + 1, run + tot, acc, all_rows_spent(run + tot)

    _, _, acc, _ = lax.while_loop(lambda c: jnp.logical_and(c[0] < 2 * qi, c[3] == 0), below,
                                  (jnp.int32(0), *carry, all_rows_spent(carry[0])))
    o_ref[...] = acc.astype(o_ref.dtype)


def _sb_prompt(q, k, v, t_prompt, n_heads, hd, tq):
    M = q.shape[0]
    tk = tq // 2
    return pl.pallas_call(
        functools.partial(_sb_prompt_body, tq=tq, tk=tk),
        grid=(n_heads, t_prompt // tq),
        in_specs=[pl.BlockSpec((tq, hd), lambda h, i: (i, h)),
                  pl.BlockSpec((t_prompt, hd), lambda h, i: (0, h)),
                  pl.BlockSpec((t_prompt, hd), lambda h, i: (0, h))],
        out_specs=pl.BlockSpec((tq, hd), lambda h, i: (i, h)),
        out_shape=jax.ShapeDtypeStruct((M, n_heads * hd), bf16),
        scratch_shapes=[pltpu.VMEM((2 * tk, tk), bf16)],
        compiler_params=_params(2),
        name="sb_prompt",
    )(q, k, v)


def _sb_sample_body(q_ref, k_ref, v_ref, pk_ref, pv_ref, _, o_ref, *, past, ts, n_heads, hd, tp):
    tri2_new = _tri2(ts)
    tri2 = _tri2(tp)
    strict = _causal(ts, strict=True)
    for h in range(n_heads):
        cols = slice(h * hd, (h + 1) * hd)
        q = q_ref[:, cols]
        zc, run = _sb_scores(q, k_ref[:, cols].astype(bf16), tri2_new, strict)
        acc = _sb_out(zc, jnp.zeros((ts, 1), f32), v_ref[:, cols].astype(bf16), strict)

        def past_kv(p0):
            return pk_ref[h, p0:p0 + tp, :].astype(bf16), pv_ref[h, p0:p0 + tp, :].astype(bf16)

        for p0 in range(past - tp, -1, -2 * tp):
            run, acc = _sb_pair(q, past_kv(p0), past_kv(p0 - tp), tri2, run, acc)
        o_ref[:, cols] = acc.astype(o_ref.dtype)


def _sb_sample(q, k, v, pk, pv, layer, o_buf, row0, ts, tp):
    return _sample_attn_call(functools.partial(_sb_sample_body, tp=tp), "sb_sample",
                             q, k, v, pk, pv, layer, None, o_buf, row0, ts)


def _mem_attn_body(q_ref, k_ref, v_ref, *rest, n_heads, hd):
    o_ref = rest[-1]
    for h in range(n_heads):
        sl = slice(h * hd, (h + 1) * hd)
        head = (lambda ref: ref[:, h, :]) if len(k_ref.shape) == 3 else (lambda ref: ref[:, sl])
        s = _dot_nt(q_ref[:, sl], head(k_ref).astype(bf16))
        p = jnp.exp(s - jnp.max(s, axis=-1, keepdims=True))
        acc = _dot(p.astype(bf16), head(v_ref).astype(bf16))
        o_ref[:, sl] = (acc / jnp.sum(p, axis=-1, keepdims=True)).astype(o_ref.dtype)


def _mem_attn_prompt(qm, mk, mv, t_prompt, tq, n_heads, hd):
    M, W = qm.shape
    Mm = mk.shape[0]
    return pl.pallas_call(
        functools.partial(_mem_attn_body, n_heads=n_heads, hd=hd),
        grid=(t_prompt // tq,),
        in_specs=[pl.BlockSpec((tq, W), lambda i: (i, 0)),
                  pl.BlockSpec((Mm, W), lambda i: (0, 0)),
                  pl.BlockSpec((Mm, W), lambda i: (0, 0))],
        out_specs=pl.BlockSpec((tq, W), lambda i: (i, 0)),
        out_shape=jax.ShapeDtypeStruct((M, W), bf16),
        compiler_params=_params(1),
        name="mem_attn_prompt",
    )(qm, mk, mv)


def _mem_attn_sample(qm, mk, mv, layer, o_buf, row0, ts):
    M, W = qm.shape
    _, B, Mm, n_heads, hd = mk.shape
    blk0 = row0 // ts
    row_spec = pl.BlockSpec((ts, W), lambda b: (blk0 + b, 0))
    mem_spec = pl.BlockSpec((None, None, Mm, n_heads, hd), lambda b: (layer, b, 0, 0, 0))
    return pl.pallas_call(
        functools.partial(_mem_attn_body, n_heads=n_heads, hd=hd),
        grid=(B,),
        in_specs=[row_spec, mem_spec, mem_spec, pl.BlockSpec(memory_space=pl.ANY)],
        out_specs=row_spec,
        out_shape=jax.ShapeDtypeStruct((M, W), bf16),
        input_output_aliases={3: 0},
        compiler_params=_params(1),
        name="mem_attn_sample",
    )(qm, mk, mv, o_buf)


def _gla_chunk(rows, h, log_a, tril, q_ref, k_ref, v_ref, r_ref, br_ref, ng_ref, o_ref, S_h, *, dkp, dv):
    C, U = GLA_CHUNK, GLA_SUB
    ks = slice(h * dkp, (h + 1) * dkp)
    vs = slice(h * dv, (h + 1) * dv)
    row_in_sub = lax.broadcasted_iota(jnp.int32, (U, 1), 0)
    hi, mid, lo = _split3(log_a[rows, ks])
    G = _dot(tril, hi) + _dot(tril, mid) + _dot(tril, lo)
    q = q_ref[rows, ks]
    k = k_ref[rows, ks]
    v = v_ref[rows, vs]
    v16 = v.astype(bf16)
    o_inter = _dot((q * jnp.exp(G)).astype(bf16), S_h.astype(bf16))

    o_rows = []
    for b in range(C // U):
        r0 = b * U
        Gb, qb = G[r0:r0 + U], q[r0:r0 + U]
        o_b = o_inter[r0:r0 + U]
        if b > 0:
            ref = G[r0 - 1:r0]
            qg = qb * jnp.exp(Gb - ref)
            kg = k[:r0] * jnp.exp(ref - G[:r0])
            a = _dot_nt(qg.astype(bf16), kg.astype(bf16))
            o_b = o_b + _dot(a.astype(bf16), v16[:r0])
        for j in range(U):
            s = r0 + j
            e = jnp.exp(jnp.minimum(Gb - G[s:s + 1], 0.0))
            a = jnp.sum(qb * e * k[s:s + 1], axis=-1, keepdims=True)
            a = jnp.where(row_in_sub >= j, a, 0.0)
            o_b = o_b + a * v[s:s + 1]
        o_rows.append(o_b)
    o = jnp.concatenate(o_rows, axis=0)

    on = o * lax.rsqrt(jnp.mean(o * o, axis=-1, keepdims=True) + RMS_EPS) * ng_ref[:, vs]
    x = r_ref[rows, vs] + br_ref[:, vs]
    o_ref[rows, vs] = (on * (x / (1.0 + jnp.exp(-x)))).astype(o_ref.dtype)

    kl = k * jnp.exp(G[C - 1:C] - G)
    decay_col = jnp.exp(G.T[:, C - 1:C])
    return decay_col * S_h + _dot_tn(kl.astype(bf16), v16)


def _gla_body(q_ref, k_ref, v_ref, r_ref, gl_ref, w2_ref, bg_ref, br_ref, ng_ref, s0_ref,
              o_ref, sp_ref, ss_ref, S, *, n_prompt_steps, n_heads, dk, dkp, dv):
    c = pl.program_id(0)
    C = GLA_CHUNK
    is_sample = c >= n_prompt_steps

    @pl.when(c == 0)
    def _():
        S[...] = jnp.zeros_like(S)

    log_a = _log_sigmoid(_dot(gl_ref[...].astype(bf16), w2_ref[...]) + bg_ref[...]) * (1.0 / GLA_GATE_TEMP)
    tril = _tri(C, C, lambda r, cc: r >= cc)
    refs = (q_ref, k_ref, v_ref, r_ref, br_ref, ng_ref, o_ref)

    for h in range(n_heads):
        S_h = S_prompt = S[h]
        for ci in range(GLA_STEP_CHUNKS):
            s0 = jnp.concatenate([s0_ref[ci, h], jnp.zeros((dkp - dk, dv), f32)], axis=0)
            S_h = _gla_chunk(slice(ci * C, (ci + 1) * C), h, log_a, tril, *refs,
                             jnp.where(is_sample, s0, S_h), dkp=dkp, dv=dv)
            ss_ref[ci, h] = S_h[:dk]
        S_prompt = jnp.where(is_sample, S_prompt, S_h)
        S[h] = S_prompt
        sp_ref[h] = S_prompt[:dk]


def _gla(q, k, v, r, glow, w2, bg, br, ng, s0, t_prompt, n_heads, dk, dkp, dv):
    M = q.shape[0]
    rows = GLA_CHUNK * GLA_STEP_CHUNKS
    B = s0.shape[0]
    n_p = t_prompt // rows
    row = lambda width: pl.BlockSpec((rows, width), lambda c: (c, 0))
    const = lambda a: pl.BlockSpec(a.shape, lambda c: (0,) * a.ndim)
    sample_state = pl.BlockSpec((GLA_STEP_CHUNKS, n_heads, dk, dv), lambda c: (jnp.maximum(c - n_p, 0), 0, 0, 0))
    return pl.pallas_call(
        functools.partial(_gla_body, n_prompt_steps=n_p, n_heads=n_heads, dk=dk, dkp=dkp, dv=dv),
        grid=(M // rows,),
        in_specs=[row(n_heads * dkp), row(n_heads * dkp), row(n_heads * dv), row(n_heads * dv),
                  row(glow.shape[1]), const(w2), const(bg), const(br), const(ng), sample_state],
        out_specs=[row(n_heads * dv), pl.BlockSpec((n_heads, dk, dv), lambda c: (0, 0, 0)), sample_state],
        out_shape=[jax.ShapeDtypeStruct((M, n_heads * dv), bf16),
                   jax.ShapeDtypeStruct((n_heads, dk, dv), f32),
                   jax.ShapeDtypeStruct((B, n_heads, dk, dv), f32)],
        scratch_shapes=[pltpu.VMEM((n_heads, dkp, dv), f32)],
        compiler_params=_params(1),
        name="gla",
    )(q, k, v, r, glow, w2, bg, br, ng, s0)


def _out_proj_body(o_ref, om_ref, x_ref, w_ref, y_ref, *, wo):
    y_ref[...] = x_ref[...] + _dot(o_ref[...], w_ref[:wo, :]) + _dot(om_ref[...], w_ref[wo:, :])


def _out_proj(o, om, x, w, tm):
    M, D = x.shape
    wo, wm = o.shape[1], om.shape[1]
    return pl.pallas_call(
        functools.partial(_out_proj_body, wo=wo),
        grid=(M // tm,),
        in_specs=[pl.BlockSpec((tm, wo), lambda i: (i, 0)),
                  pl.BlockSpec((tm, wm), lambda i: (i, 0)),
                  pl.BlockSpec((tm, D), lambda i: (i, 0)),
                  pl.BlockSpec((wo + wm, D), lambda i: (0, 0))],
        out_specs=pl.BlockSpec((tm, D), lambda i: (i, 0)),
        out_shape=jax.ShapeDtypeStruct((M, D), f32),
        compiler_params=_params(1),
        name="out_proj",
    )(o, om, x, w)


def _mlp_body(x_ref, g_ref, wu_ref, wd_ref, gf_ref, y_ref, h_ref, *, final_norm):
    j = pl.program_id(1)

    @pl.when(j == 0)
    def _():
        x = x_ref[...]
        h_ref[...] = _rms(x, g_ref[...]).astype(bf16)
        y_ref[...] = x

    u = jnp.maximum(_dot(h_ref[...], wu_ref[...].astype(bf16)), 0.0)
    y_ref[...] += _dot((u * u).astype(bf16), wd_ref[...].astype(bf16))

    if final_norm:
        @pl.when(j == pl.num_programs(1) - 1)
        def _():
            y_ref[...] = _rms(y_ref[...], gf_ref[...])


def _mlp(x, g, wu, wd, layer, gf, final_norm, tm, tf):
    M, D = x.shape
    F = wu.shape[2]
    return pl.pallas_call(
        functools.partial(_mlp_body, final_norm=final_norm),
        grid=(M // tm, F // tf),
        in_specs=[pl.BlockSpec((tm, D), lambda i, j: (i, 0), pipeline_mode=pl.Buffered(1)),
                  pl.BlockSpec((1, D), lambda i, j: (0, 0)),
                  pl.BlockSpec((None, D, tf), lambda i, j: (layer, 0, j)),
                  pl.BlockSpec((None, tf, D), lambda i, j: (layer, j, 0)),
                  pl.BlockSpec((1, D), lambda i, j: (0, 0))],
        out_specs=pl.BlockSpec((tm, D), lambda i, j: (i, 0)),
        out_shape=jax.ShapeDtypeStruct((M, D), f32),
        scratch_shapes=[pltpu.VMEM((tm, D), bf16)],
        compiler_params=_params(2),
        name="mlp",
    )(x, g.reshape(1, D), wu, wd, gf.reshape(1, D))


def _largest_divisor(n, cap, mult):
    best = None
    for d in range(mult, min(n, cap) + 1, mult):
        if n % d == 0:
            best = d
    assert best is not None, (n, cap, mult)
    return best


def kernel(x_prompt, x_sample, cache_fox_k, cache_fox_v, cache_fox_logf, cache_sb_k, cache_sb_v, state_gla,
           cache_mem_k, cache_mem_v, mem_prompt, norm_mix_g, norm_mlp_g, norm_mem_g, norm_final_g, w_mem_kv,
           w_in_fox, b_forget, w_out_fox, w_in_sb, w_out_sb, w_in_gla, w_gate2_gla, b_gate_gla, b_outgate_gla,
           norm_gla_g, w_out_gla, w_up, w_down):
    Bp, Tp0, D = x_prompt.shape
    Bs, Ts, _ = x_sample.shape
    assert Bp == 1, "the prompt group is handled as one sequence"
    Tp = Bp * Tp0
    M = Tp + Bs * Ts
    depth = norm_mix_g.shape[0]
    H, hd = cache_fox_k.shape[-2:]
    aw = H * hd
    past = cache_fox_k.shape[2]
    Mm, MH = cache_mem_k.shape[2], cache_mem_k.shape[3]
    mw = MH * hd
    GH, dk, dv = state_gla.shape[-3:]
    dkp = -(-dk // LANES) * LANES
    rank = w_gate2_gla.shape[1]
    C = GLA_CHUNK
    assert Ts == C and Tp % (C * GLA_STEP_CHUNKS) == 0 and Bs % GLA_STEP_CHUNKS == 0 and M % 16 == 0

    tm = _largest_divisor(M, 512, 16)
    tm_st = _largest_divisor(math.gcd(Tp, M - Tp), 512, Ts)
    tm_mlp = _largest_divisor(M, 1152, 16)
    tf = _largest_divisor(w_up.shape[2], 512, LANES)
    ta = _largest_divisor(Tp, 512, 2 * LANES)
    tg = _largest_divisor(Tp, 512, LANES)
    tq_mem = _largest_divisor(Tp, 512, 16)
    tp_sb = _largest_divisor(past // 2, 256, LANES)
    att_scale = hd ** -0.5
    att_scale2 = att_scale * LOG2E

    x = jnp.concatenate([x_prompt.reshape(Tp, D), x_sample.reshape(Bs * Ts, D)], axis=0)
    mk_p, mv_p = _memkv(mem_prompt.reshape(Mm, D), norm_mem_g, w_mem_kv)
    w16_fox, w16_sb = w_in_fox.astype(bf16), w_in_sb.astype(bf16)

    n_fox, n_sb = w_in_fox.shape[0], w_in_sb.shape[0]
    kv_state = {0: None, 1: None}
    logf_p, logf_s, gla_p, gla_s = [], [], [], []
    for i in range(depth):
        kind, j = i % 3, i // 3
        g = norm_mix_g[i]
        if kind in (0, 1):
            w = (w_in_fox if kind == 0 else w_in_sb)[j]
            w16 = w16_fox if kind == 0 else w16_sb
            n_gate = H if kind == 0 else 0
            (q,) = _proj(x, g, w16, [(0, aw, bf16, att_scale2)], tm, w_block=(j, 0, aw))
            n_layers = n_fox if kind == 0 else n_sb
            st = kv_state[kind]
            k, *st_k = _proj_state(x, g, w16, (j, 1, aw), j, st[0] if st else n_layers, Tp, Ts, tm_st, H, hd)
            v, *st_v = _proj_state(x, g, w16, (j, 2, aw), j, st[1] if st else n_layers, Tp, Ts, tm_st, H, hd)
            kv_state[kind] = (st_k, st_v)
            (qm,) = _proj(x, g, w[:, 3 * aw + n_gate:].astype(bf16), [(0, mw, bf16, att_scale)], tm)
            if kind == 0:
                wf_t = jnp.pad(w[:, 3 * aw:3 * aw + H].T, ((0, FOX_GATE_ROWS - H), (0, 0))).astype(bf16)
                bf_col = jnp.pad(b_forget[j], (0, FOX_GATE_ROWS - H)).reshape(FOX_GATE_ROWS, 1)
                lf_p, cf_p = _fox_gate_prompt(x, g, wf_t, bf_col, Tp, tg)
                plogf_t = jnp.pad(jnp.swapaxes(cache_fox_logf[j], 1, 2), ((0, 0), (0, FOX_GATE_ROWS - H), (0, 0)))
                lf_s, cf_s = _fox_gate_sample(x, g, wf_t, bf_col, plogf_t, Tp, Ts)
                o = _fox_prompt(q, k, v, cf_p, Tp, H, hd, ta)
                o = _fox_sample(q, k, v, jnp.swapaxes(cache_fox_k, 2, 3), jnp.swapaxes(cache_fox_v, 2, 3), j,
                                cf_s, o, Tp, Ts)
                logf_p.append(lf_p[:H].T.reshape(Bp, Tp0, H))
                logf_s.append(jnp.swapaxes(lf_s[:, :H], 1, 2))
                w_out = w_out_fox[j]
            else:
                o = _sb_prompt(q, k, v, Tp, H, hd, ta)
                o = _sb_sample(q, k, v, jnp.swapaxes(cache_sb_k, 2, 3), jnp.swapaxes(cache_sb_v, 2, 3), j,
                               o, Tp, Ts, tp_sb)
                w_out = w_out_sb[j]
        else:
            w = w_in_gla[j]
            kw, vw = GH * dk, GH * dv

            def pad_heads(a):
                a = a.reshape(a.shape[:-1] + (GH, dk))
                a = jnp.pad(a, [(0, 0)] * (a.ndim - 1) + [(0, dkp - dk)])
                return a.reshape(a.shape[:-2] + (GH * dkp,))

            wqk = jnp.concatenate([pad_heads(w[:, :kw]), pad_heads(w[:, kw:2 * kw])], axis=1).astype(bf16)
            wv = w[:, 2 * kw:2 * kw + vw].astype(bf16)
            wr = w[:, 2 * kw + vw:2 * kw + 2 * vw].astype(bf16)
            c0 = 2 * kw + 2 * vw
            wmg = jnp.concatenate([w[:, c0 + rank:], jnp.pad(w[:, c0:c0 + rank], ((0, 0), (0, LANES - rank)))],
                                  axis=1).astype(bf16)
            q, k = _proj(x, g, wqk, [(0, GH * dkp, f32, dk ** -0.5), (GH * dkp, 2 * GH * dkp, f32, 1.0)], tm)
            (v,) = _proj(x, g, wv, [(0, vw, f32, 1.0)], tm)
            (r,) = _proj(x, g, wr, [(0, vw, f32, 1.0)], tm)
            qm, glow = _proj(x, g, wmg, [(0, mw, bf16, att_scale), (mw, mw + LANES, f32, 1.0)], tm)
            w2 = jnp.pad(pad_heads(w_gate2_gla[j]), ((0, LANES - rank), (0, 0))).astype(bf16)
            bg = pad_heads(b_gate_gla[j]).reshape(1, GH * dkp)
            o, st_p, st_s = _gla(q, k, v, r, glow, w2, bg, b_outgate_gla[j].reshape(1, vw),
                                 norm_gla_g[j].reshape(1, vw), state_gla[j], Tp, GH, dk, dkp, dv)
            gla_p.append(st_p[None])
            gla_s.append(st_s)
            w_out = w_out_gla[j]

        om = _mem_attn_prompt(qm, mk_p[i], mv_p[i], Tp, tq_mem, MH, hd)
        om = _mem_attn_sample(qm, cache_mem_k, cache_mem_v, i, om, Tp, Ts)
        x = _out_proj(o, om, x, w_out.astype(bf16), tm)
        x = _mlp(x, norm_mlp_g[i], w_up, w_down, i, norm_final_g, i == depth - 1, tm_mlp, tf)

    def prompt_state(a):
        return jnp.swapaxes(a.reshape((a.shape[0], Bp, H, Tp0, hd)), 2, 3)

    def sample_state(a):
        return jnp.swapaxes(a, 2, 3)

    (fox_k, fox_v), (sb_k, sb_v) = kv_state[0], kv_state[1]
    return (x[:Tp].reshape(Bp, Tp0, D), x[Tp:].reshape(Bs, Ts, D),
            prompt_state(fox_k[0]), prompt_state(fox_v[0]), jnp.stack(logf_p),
            prompt_state(sb_k[0]), prompt_state(sb_v[0]),
            jnp.stack(gla_p),
            mk_p.reshape(depth, Bp, Mm, MH, hd), mv_p.reshape(depth, Bp, Mm, MH, hd),
            sample_state(fox_k[1]), sample_state(fox_v[1]), jnp.stack(logf_s),
            sample_state(sb_k[1]), sample_state(sb_v[1]),
            jnp.stack(gla_s))
```

```python
import functools
import math

import jax
import jax.numpy as jnp
from jax import lax
from jax.experimental import pallas as pl
from jax.experimental.pallas import tpu as pltpu

f32 = jnp.float32
bf16 = jnp.bfloat16

RMS_EPS = 1e-6
GLA_GATE_TEMP = 16.0
GLA_CHUNK = 64
GLA_SUB = 8
GLA_STEP_CHUNKS = 2
LOG2E = 1.4426950408889634
MASKED_LOGIT = -1e30
SKIP_BITS = 160.0
LANES = 128
FOX_GATE_ROWS = 16
VMEM_LIMIT_BYTES = 56 * 1024 * 1024


def _params(n_axes):
    return pltpu.CompilerParams(dimension_semantics=("arbitrary",) * n_axes,
                                vmem_limit_bytes=VMEM_LIMIT_BYTES)


def _rms(x, g):
    return x * lax.rsqrt(jnp.mean(x * x, axis=-1, keepdims=True) + RMS_EPS) * g


def _log_sigmoid(z):
    return jnp.minimum(z, 0.0) - jnp.log1p(jnp.exp(-jnp.abs(z)))


def _split3(x):
    hi = x.astype(bf16)
    r = x - hi.astype(f32)
    mid = r.astype(bf16)
    lo = (r - mid.astype(f32)).astype(bf16)
    return hi, mid, lo


def _split2(x):
    hi = x.astype(bf16)
    lo = (x - hi.astype(f32)).astype(bf16)
    return hi, lo


def _tri(n, m, fn):
    r = lax.broadcasted_iota(jnp.int32, (n, m), 0)
    c = lax.broadcasted_iota(jnp.int32, (n, m), 1)
    return jnp.where(fn(r, c), 1.0, 0.0).astype(bf16)


def _dot(a, b):
    return jnp.dot(a, b, preferred_element_type=f32)


def _dot_nt(a, b):
    return lax.dot_general(a, b, (((1,), (1,)), ((), ())), preferred_element_type=f32)


def _dot_tn(a, b):
    return lax.dot_general(a, b, (((0,), (0,)), ((), ())), preferred_element_type=f32)


def _cumsum_lanes(x, carry, tri):
    n = x.shape[1]
    outs = []
    for b0 in range(0, n, LANES):
        w = min(LANES, n - b0)
        hi, mid, lo = _split3(x[:, b0:b0 + w])
        t = tri[:w, :w]
        c = _dot(hi, t) + _dot(mid, t) + _dot(lo, t) + carry
        carry = c[:, w - 1:w]
        outs.append(c)
    return outs, carry


def _proj_body(x_ref, g_ref, w_ref, *o_refs, cols, scales):
    h = _rms(x_ref[...], g_ref[...]).astype(bf16)
    y = _dot(h, w_ref[...].astype(bf16))
    for o_ref, (c0, c1), sc in zip(o_refs, cols, scales):
        part = y[:, c0:c1]
        if sc != 1.0:
            part = part * sc
        o_ref[...] = part.astype(o_ref.dtype)


def _stacked_weight_spec(D, width, layer, col_block):
    return pl.BlockSpec((None, D, width), lambda i: (layer, 0, col_block), pipeline_mode=pl.Buffered(1))


def _proj(x, g, w, outs, tm, w_block=None):
    M, D = x.shape
    if w_block is None:
        w_spec = pl.BlockSpec(w.shape, lambda i: (0, 0))
    else:
        layer, col_block, width = w_block
        w_spec = _stacked_weight_spec(D, width, layer, col_block)
    cols = tuple((c0, c1) for c0, c1, _, _ in outs)
    scales = tuple(float(s) for _, _, _, s in outs)
    return pl.pallas_call(
        functools.partial(_proj_body, cols=cols, scales=scales),
        grid=(M // tm,),
        in_specs=[pl.BlockSpec((tm, D), lambda i: (i, 0)),
                  pl.BlockSpec((1, D), lambda i: (0, 0)),
                  w_spec],
        out_specs=[pl.BlockSpec((tm, c1 - c0), lambda i: (i, 0)) for c0, c1 in cols],
        out_shape=[jax.ShapeDtypeStruct((M, c1 - c0), dt) for c0, c1, dt, _ in outs],
        compiler_params=_params(1),
        name="norm_proj",
    )(x, g.reshape(1, D), w)


def _proj_state_body(x_ref, g_ref, w_ref, *refs, n_prompt_blocks, n_heads, hd, ts):
    a_ref, sp_ref, ss_ref = refs[-3:]
    i = pl.program_id(0)
    h = _rms(x_ref[...], g_ref[...]).astype(bf16)
    y = _dot(h, w_ref[...].astype(bf16))
    a_ref[...] = y.astype(a_ref.dtype)

    @pl.when(i < n_prompt_blocks)
    def _():
        for hh in range(n_heads):
            sp_ref[hh] = y[:, hh * hd:(hh + 1) * hd]

    @pl.when(i >= n_prompt_blocks)
    def _():
        for b in range(ss_ref.shape[0]):
            for hh in range(n_heads):
                ss_ref[b, hh] = y[b * ts:(b + 1) * ts, hh * hd:(hh + 1) * hd]


def _proj_state(x, g, w, w_block, slot, prev, t_prompt, ts, tm, n_heads, hd):
    M, D = x.shape
    layer, col_block, width = w_block
    n_p = t_prompt // tm
    in_specs = [pl.BlockSpec((tm, D), lambda i: (i, 0)),
                pl.BlockSpec((1, D), lambda i: (0, 0)),
                _stacked_weight_spec(D, width, layer, col_block)]
    args = [x, g.reshape(1, D), w]
    if isinstance(prev, int):
        shapes = [jax.ShapeDtypeStruct((prev, n_heads, t_prompt, hd), f32),
                  jax.ShapeDtypeStruct((prev, (M - t_prompt) // ts, n_heads, ts, hd), f32)]
        aliases = {}
    else:
        shapes = [jax.ShapeDtypeStruct(p.shape, p.dtype) for p in prev]
        in_specs += [pl.BlockSpec(memory_space=pl.ANY)] * 2
        args += list(prev)
        aliases = {3: 1, 4: 2}
    return pl.pallas_call(
        functools.partial(_proj_state_body, n_prompt_blocks=n_p, n_heads=n_heads, hd=hd, ts=ts),
        grid=(M // tm,),
        in_specs=in_specs,
        out_specs=[pl.BlockSpec((tm, width), lambda i: (i, 0)),
                   pl.BlockSpec((None, n_heads, tm, hd), lambda i: (slot, 0, jnp.minimum(i, n_p - 1), 0)),
                   pl.BlockSpec((None, tm // ts, n_heads, ts, hd),
                                lambda i: (slot, jnp.maximum(i - n_p, 0), 0, 0, 0))],
        out_shape=[jax.ShapeDtypeStruct((M, width), bf16)] + shapes,
        input_output_aliases=aliases,
        compiler_params=_params(1),
        name="norm_proj_state",
    )(*args)


def _memkv_body(m_ref, g_ref, w_ref, k_ref, v_ref, *, mw):
    h = _rms(m_ref[...], g_ref[...]).astype(bf16)
    w = w_ref[...].astype(bf16)
    k_ref[...] = _dot(h, w[:, :mw])
    v_ref[...] = _dot(h, w[:, mw:])


def _memkv(mem, g, w):
    L, D, two_mw = w.shape
    Mm = mem.shape[0]
    mw = two_mw // 2
    return pl.pallas_call(
        functools.partial(_memkv_body, mw=mw),
        grid=(L,),
        in_specs=[pl.BlockSpec((Mm, D), lambda l: (0, 0)),
                  pl.BlockSpec((None, 1, D), lambda l: (l, 0, 0)),
                  pl.BlockSpec((None, D, two_mw), lambda l: (l, 0, 0))],
        out_specs=[pl.BlockSpec((None, Mm, mw), lambda l: (l, 0, 0))] * 2,
        out_shape=[jax.ShapeDtypeStruct((L, Mm, mw), f32)] * 2,
        compiler_params=_params(1),
        name="mem_kv",
    )(mem, g.reshape(L, 1, D), w)


def _gate_logits(x_ref, g_ref, wf_ref, bf_ref):
    h = _rms(x_ref[...], g_ref[...]).astype(bf16)
    return _log_sigmoid(_dot_nt(wf_ref[...], h) + bf_ref[...])


def _fox_gate_prompt_body(x_ref, g_ref, wf_ref, bf_ref, lf_ref, cf_ref, carry_ref):
    @pl.when(pl.program_id(0) == 0)
    def _():
        carry_ref[...] = jnp.zeros_like(carry_ref)

    logf = _gate_logits(x_ref, g_ref, wf_ref, bf_ref)
    lf_ref[...] = logf
    tri = _tri(LANES, LANES, lambda r, c: r <= c)
    outs, carry = _cumsum_lanes(logf, carry_ref[:, 0:1], tri)
    for b, c in enumerate(outs):
        cf_ref[:, b * LANES:(b + 1) * LANES] = c * LOG2E
    carry_ref[...] = jnp.broadcast_to(carry, carry_ref.shape)


def _fox_gate_prompt(x, g, wf_t, bf, t_prompt, tg):
    D = x.shape[1]
    R = wf_t.shape[0]
    return pl.pallas_call(
        _fox_gate_prompt_body,
        grid=(t_prompt // tg,),
        in_specs=[pl.BlockSpec((tg, D), lambda i: (i, 0)),
                  pl.BlockSpec((1, D), lambda i: (0, 0)),
                  pl.BlockSpec((R, D), lambda i: (0, 0)),
                  pl.BlockSpec((R, 1), lambda i: (0, 0))],
        out_specs=[pl.BlockSpec((R, tg), lambda i: (0, i))] * 2,
        out_shape=[jax.ShapeDtypeStruct((R, t_prompt), f32)] * 2,
        scratch_shapes=[pltpu.VMEM((R, LANES), f32)],
        compiler_params=_params(1),
        name="fox_gate_prompt",
    )(x, g.reshape(1, D), wf_t, bf)


def _fox_gate_sample_body(x_ref, g_ref, wf_ref, bf_ref, pl_ref, lf_ref, cf_ref, *, past, ts):
    logf = _gate_logits(x_ref, g_ref, wf_ref, bf_ref)
    lf_ref[...] = logf
    tri = _tri(LANES, LANES, lambda r, c: r <= c)
    zero = jnp.zeros((logf.shape[0], 1), f32)
    outs, carry = _cumsum_lanes(pl_ref[...], zero, tri)
    for b, c in enumerate(outs):
        cf_ref[:, b * LANES:(b + 1) * LANES] = c * LOG2E
    (new,), _ = _cumsum_lanes(logf, carry, tri)
    cf_ref[:, past:past + ts] = new * LOG2E
    cf_ref[:, past + ts:] = jnp.zeros((logf.shape[0], LANES - ts), f32)


def _fox_gate_sample(x, g, wf_t, bf, plogf_t, row0, ts):
    D = x.shape[1]
    R = wf_t.shape[0]
    B, _, past = plogf_t.shape
    blk0 = row0 // ts
    return pl.pallas_call(
        functools.partial(_fox_gate_sample_body, past=past, ts=ts),
        grid=(B,),
        in_specs=[pl.BlockSpec((ts, D), lambda b: (blk0 + b, 0)),
                  pl.BlockSpec((1, D), lambda b: (0, 0)),
                  pl.BlockSpec((R, D), lambda b: (0, 0)),
                  pl.BlockSpec((R, 1), lambda b: (0, 0)),
                  pl.BlockSpec((None, R, past), lambda b: (b, 0, 0))],
        out_specs=[pl.BlockSpec((None, R, ts), lambda b: (b, 0, 0)),
                   pl.BlockSpec((None, R, past + LANES), lambda b: (b, 0, 0))],
        out_shape=[jax.ShapeDtypeStruct((B, R, ts), f32),
                   jax.ShapeDtypeStruct((B, R, past + LANES), f32)],
        compiler_params=_params(1),
        name="fox_gate_sample",
    )(x, g.reshape(1, D), wf_t, bf, plogf_t)


def _softmax_tile(s, m, l, acc, v):
    m_new = jnp.maximum(m, jnp.max(s, axis=-1, keepdims=True))
    alpha = jnp.exp2(m - m_new)
    p = jnp.exp2(s - m_new)
    l = alpha * l + jnp.sum(p, axis=-1, keepdims=True)
    acc = alpha * acc + _dot(p.astype(bf16), v)
    return m_new, l, acc


def _softmax_tile_wide(s, m, acc, v):
    m_new = jnp.maximum(m, jnp.max(s, axis=-1, keepdims=True))
    p = jnp.exp2((s - m_new).astype(bf16))
    return m_new, jnp.exp2(m - m_new) * acc + _dot(p, v)


def _causal(n, strict):
    r = lax.broadcasted_iota(jnp.int32, (n, n), 0)
    c = lax.broadcasted_iota(jnp.int32, (n, n), 1)
    return c < r if strict else c <= r


def _fox_prompt_body(q_ref, k16, v16, f_ref, o_ref, knorm_ref, *, t):
    qi = pl.program_id(1)

    @pl.when(qi == 0)
    def _():
        kf = k16[...].astype(f32)
        k_sq = jnp.max(jnp.sum(kf * kf, axis=-1, keepdims=True), axis=0, keepdims=True)
        knorm_ref[...] = jnp.broadcast_to(jnp.sqrt(k_sq), knorm_ref.shape)

    q = q_ref[...]
    qf = q.astype(f32)
    qk_bound = jnp.sqrt(jnp.sum(qf * qf, axis=-1, keepdims=True)) * knorm_ref[0:1, 0:1]
    col_minus_row = (lax.broadcasted_iota(jnp.int32, (t, t), 1) - lax.broadcasted_iota(jnp.int32, (t, t), 0))

    def scores(kb, max_col_minus_row):
        rows = pl.ds(pl.multiple_of(kb * t, t), t)
        s = _dot_nt(q, k16[rows, :]) - f_ref[pl.ds(kb, 1), :]
        if max_col_minus_row is not None:
            s = jnp.where(col_minus_row <= max_col_minus_row, s, MASKED_LOGIT)
        return s, jnp.concatenate([v16[rows, :], ones], axis=1)

    def tile(kb, carry, max_col_minus_row):
        s, v = scores(kb, max_col_minus_row)
        return _softmax_tile_wide(s, *carry, v)

    def rest_is_zero(a, b, kb):
        bias_max = jnp.max(-f_ref[pl.ds(kb, 1), :])
        return (jnp.max(qk_bound + bias_max - jnp.maximum(a[0], b[0])) < -SKIP_BITS).astype(jnp.int32)

    hd = q.shape[1]
    ones = jnp.ones((t, hd), bf16)
    init = (jnp.full((t, 1), MASKED_LOGIT, f32), jnp.zeros((t, 2 * hd), f32))
    odd = qi % 2
    a = tile(qi, init, 0)
    b = lax.cond(odd == 1, lambda: tile(qi - 1, init, None), lambda: init)
    first = qi - 1 - odd
    n_pairs = qi // 2

    def pair(c):
        i, a, b, _ = c
        kb = first - 2 * i
        s_a, v_a = scores(kb, None)
        s_b, v_b = scores(kb - 1, None)
        a = _softmax_tile_wide(s_a, *a, v_a)
        b = _softmax_tile_wide(s_b, *b, v_b)
        return i + 1, a, b, rest_is_zero(a, b, jnp.maximum(kb - 2, 0))

    _, (m_a, acc_a), (m_b, acc_b), _ = lax.while_loop(
        lambda c: jnp.logical_and(c[0] < n_pairs, c[3] == 0), pair,
        (jnp.int32(0), a, b, rest_is_zero(a, b, jnp.maximum(first, 0))))
    m = jnp.maximum(m_a, m_b)
    acc = jnp.exp2(m_a - m) * acc_a + jnp.exp2(m_b - m) * acc_b
    o_ref[...] = (acc[:, :hd] / acc[:, hd:]).astype(o_ref.dtype)


def _fox_prompt(q, k, v, cf, t_prompt, n_heads, hd, t):
    M = q.shape[0]
    nq = t_prompt // t
    return pl.pallas_call(
        functools.partial(_fox_prompt_body, t=t),
        grid=(n_heads, nq),
        in_specs=[pl.BlockSpec((t, hd), lambda h, i: (i, h)),
                  pl.BlockSpec((t_prompt, hd), lambda h, i: (0, h)),
                  pl.BlockSpec((t_prompt, hd), lambda h, i: (0, h)),
                  pl.BlockSpec((None, nq, t), lambda h, i: (h, 0, 0))],
        out_specs=pl.BlockSpec((t, hd), lambda h, i: (i, h)),
        out_shape=jax.ShapeDtypeStruct((M, n_heads * hd), bf16),
        scratch_shapes=[pltpu.VMEM((8, LANES), f32)],
        compiler_params=_params(2),
        name="fox_prompt",
    )(q, k, v, cf.reshape(cf.shape[0], nq, t))


def _fox_sample_body(q_ref, k_ref, v_ref, pk_ref, pv_ref, f_ref, _, o_ref, *, past, ts, n_heads, hd):
    for h in range(n_heads):
        cols = slice(h * hd, (h + 1) * hd)
        q = q_ref[:, cols]
        f = f_ref[h:h + 1, :]
        s_past = _dot_nt(q, pk_ref[h].astype(bf16)) - f[:, :past]
        s_new = _dot_nt(q, k_ref[:, cols].astype(bf16)) - f[:, past:past + ts]
        s_new = jnp.where(_causal(ts, strict=False), s_new, -jnp.inf)
        m = jnp.maximum(jnp.max(s_past, axis=-1, keepdims=True), jnp.max(s_new, axis=-1, keepdims=True))
        p_past = jnp.exp2(s_past - m)
        p_new = jnp.exp2(s_new - m)
        l = jnp.sum(p_past, axis=-1, keepdims=True) + jnp.sum(p_new, axis=-1, keepdims=True)
        acc = (_dot(p_past.astype(bf16), pv_ref[h].astype(bf16))
               + _dot(p_new.astype(bf16), v_ref[:, cols].astype(bf16)))
        o_ref[:, cols] = (acc / l).astype(o_ref.dtype)


def _sample_attn_call(body, name, q, k, v, pk, pv, layer, extra, o_buf, row0, ts):
    _, B, n_heads, past, hd = pk.shape
    blk0 = row0 // ts
    new_spec = pl.BlockSpec((ts, n_heads * hd), lambda b: (blk0 + b, 0))
    past_spec = pl.BlockSpec((None, None, n_heads, past, hd), lambda b: (layer, b, 0, 0, 0))
    in_specs = [new_spec, new_spec, new_spec, past_spec, past_spec]
    args = [q, k, v, pk, pv]
    if extra is not None:
        in_specs.append(pl.BlockSpec((None,) + extra.shape[1:], lambda b: (b, 0, 0)))
        args.append(extra)
    in_specs.append(pl.BlockSpec(memory_space=pl.ANY))
    args.append(o_buf)
    return pl.pallas_call(
        functools.partial(body, past=past, ts=ts, n_heads=n_heads, hd=hd),
        grid=(B,),
        in_specs=in_specs,
        out_specs=new_spec,
        out_shape=jax.ShapeDtypeStruct(o_buf.shape, o_buf.dtype),
        input_output_aliases={len(args) - 1: 0},
        compiler_params=_params(1),
        name=name,
    )(*args)


def _fox_sample(q, k, v, pk, pv, layer, cf, o_buf, row0, ts):
    return _sample_attn_call(_fox_sample_body, "fox_sample", q, k, v, pk, pv, layer, cf, o_buf, row0, ts)


def _tri2(n):
    tri = _tri(n, n, lambda r, c: r >= c)
    return jnp.concatenate([tri, tri], axis=0)


def _sb_scores(q, k, tri2, valid):
    z = _dot_nt(q, k)
    sp = jnp.maximum(z, 0.0) + jnp.log(1.0 + jnp.exp2(-jnp.abs(z))) * LOG2E
    if valid is not None:
        sp = jnp.where(valid, sp, 0.0)
    cum = _dot(jnp.concatenate(_split2(sp), axis=1), tri2)
    return z - cum, cum[:, 0:1]


def _sb_out(zc, run, v, valid):
    a = jnp.exp2(zc)
    if valid is not None:
        a = jnp.where(valid, a, 0.0)
    return jnp.exp2(-run) * _dot(a.astype(bf16), v)


def _sb_pair(q, right, left, tri2, run, acc, valid_right=None, valid_left=None):
    (k_r, v_r), (k_l, v_l) = right, left
    zc_r, tot_r = _sb_scores(q, k_r, tri2, valid_right)
    zc_l, tot_l = _sb_scores(q, k_l, tri2, valid_left)
    acc = acc + _sb_out(zc_r, run, v_r, valid_right) + _sb_out(zc_l, run + tot_r, v_l, valid_left)
    return run + tot_r + tot_l, acc


def _sb_prompt_body(q_ref, k16, v16, o_ref, tri2_ref, *, tq, tk):
    qi = pl.program_id(1)

    @pl.when(qi == 0)
    def _():
        tri2_ref[...] = _tri2(tk)

    q = q_ref[...]
    tri2 = tri2_ref[...]

    def kv(kb):
        rows = pl.ds(pl.multiple_of(kb * tk, tk), tk)
        return k16[rows, :], v16[rows, :]

    r = lax.broadcasted_iota(jnp.int32, (tq, tk), 0)
    c = lax.broadcasted_iota(jnp.int32, (tq, tk), 1)
    init = (jnp.zeros((tq, 1), f32), jnp.zeros((tq, q.shape[1]), f32))
    carry = _sb_pair(q, kv(2 * qi + 1), kv(2 * qi), tri2, *init, valid_right=c + tk < r, valid_left=c < r)

    def all_rows_spent(run):
        return (jnp.min(run) >= SKIP_BITS).astype(jnp.int32)

    def below(c):
        i, run, acc, _ = c
        k_t, v_t = kv(2 * qi - 1 - i)
        zc, tot = _sb_scores(q, k_t, tri2, None)
        acc = acc + _sb_out(zc, run, v_t, None)
        return i + 1, run + tot, acc, all_rows_spent(run + tot)

    _, _, acc, _ = lax.while_loop(lambda c: jnp.logical_and(c[0] < 2 * qi, c[3] == 0), below,
                                  (jnp.int32(0), *carry, all_rows_spent(carry[0])))
    o_ref[...] = acc.astype(o_ref.dtype)


def _sb_prompt(q, k, v, t_prompt, n_heads, hd, tq):
    M = q.shape[0]
    tk = tq // 2
    return pl.pallas_call(
        functools.partial(_sb_prompt_body, tq=tq, tk=tk),
        grid=(n_heads, t_prompt // tq),
        in_specs=[pl.BlockSpec((tq, hd), lambda h, i: (i, h)),
                  pl.BlockSpec((t_prompt, hd), lambda h, i: (0, h)),
                  pl.BlockSpec((t_prompt, hd), lambda h, i: (0, h))],
        out_specs=pl.BlockSpec((tq, hd), lambda h, i: (i, h)),
        out_shape=jax.ShapeDtypeStruct((M, n_heads * hd), bf16),
        scratch_shapes=[pltpu.VMEM((2 * tk, tk), bf16)],
        compiler_params=_params(2),
        name="sb_prompt",
    )(q, k, v)


def _sb_sample_body(q_ref, k_ref, v_ref, pk_ref, pv_ref, _, o_ref, *, past, ts, n_heads, hd, tp):
    tri2_new = _tri2(ts)
    tri2 = _tri2(tp)
    strict = _causal(ts, strict=True)
    for h in range(n_heads):
        cols = slice(h * hd, (h + 1) * hd)
        q = q_ref[:, cols]
        zc, run = _sb_scores(q, k_ref[:, cols].astype(bf16), tri2_new, strict)
        acc = _sb_out(zc, jnp.zeros((ts, 1), f32), v_ref[:, cols].astype(bf16), strict)

        def past_kv(p0):
            return pk_ref[h, p0:p0 + tp, :].astype(bf16), pv_ref[h, p0:p0 + tp, :].astype(bf16)

        for p0 in range(past - tp, -1, -2 * tp):
            run, acc = _sb_pair(q, past_kv(p0), past_kv(p0 - tp), tri2, run, acc)
        o_ref[:, cols] = acc.astype(o_ref.dtype)


def _sb_sample(q, k, v, pk, pv, layer, o_buf, row0, ts, tp):
    return _sample_attn_call(functools.partial(_sb_sample_body, tp=tp), "sb_sample",
                             q, k, v, pk, pv, layer, None, o_buf, row0, ts)


def _mem_attn_body(q_ref, k_ref, v_ref, *rest, n_heads, hd):
    o_ref = rest[-1]
    for h in range(n_heads):
        sl = slice(h * hd, (h + 1) * hd)
        head = (lambda ref: ref[:, h, :]) if len(k_ref.shape) == 3 else (lambda ref: ref[:, sl])
        s = _dot_nt(q_ref[:, sl], head(k_ref).astype(bf16))
        p = jnp.exp(s - jnp.max(s, axis=-1, keepdims=True))
        acc = _dot(p.astype(bf16), head(v_ref).astype(bf16))
        o_ref[:, sl] = (acc / jnp.sum(p, axis=-1, keepdims=True)).astype(o_ref.dtype)


def _mem_attn_prompt(qm, mk, mv, t_prompt, tq, n_heads, hd):
    M, W = qm.shape
    Mm = mk.shape[0]
    return pl.pallas_call(
        functools.partial(_mem_attn_body, n_heads=n_heads, hd=hd),
        grid=(t_prompt // tq,),
        in_specs=[pl.BlockSpec((tq, W), lambda i: (i, 0)),
                  pl.BlockSpec((Mm, W), lambda i: (0, 0)),
                  pl.BlockSpec((Mm, W), lambda i: (0, 0))],
        out_specs=pl.BlockSpec((tq, W), lambda i: (i, 0)),
        out_shape=jax.ShapeDtypeStruct((M, W), bf16),
        compiler_params=_params(1),
        name="mem_attn_prompt",
    )(qm, mk, mv)


def _mem_attn_sample(qm, mk, mv, layer, o_buf, row0, ts):
    M, W = qm.shape
    _, B, Mm, n_heads, hd = mk.shape
    blk0 = row0 // ts
    row_spec = pl.BlockSpec((ts, W), lambda b: (blk0 + b, 0))
    mem_spec = pl.BlockSpec((None, None, Mm, n_heads, hd), lambda b: (layer, b, 0, 0, 0))
    return pl.pallas_call(
        functools.partial(_mem_attn_body, n_heads=n_heads, hd=hd),
        grid=(B,),
        in_specs=[row_spec, mem_spec, mem_spec, pl.BlockSpec(memory_space=pl.ANY)],
        out_specs=row_spec,
        out_shape=jax.ShapeDtypeStruct((M, W), bf16),
        input_output_aliases={3: 0},
        compiler_params=_params(1),
        name="mem_attn_sample",
    )(qm, mk, mv, o_buf)


def _gla_chunk(rows, h, log_a, tril, q_ref, k_ref, v_ref, r_ref, br_ref, ng_ref, o_ref, S_h, *, dkp, dv):
    C, U = GLA_CHUNK, GLA_SUB
    ks = slice(h * dkp, (h + 1) * dkp)
    vs = slice(h * dv, (h + 1) * dv)
    row_in_sub = lax.broadcasted_iota(jnp.int32, (U, 1), 0)
    hi, mid, lo = _split3(log_a[rows, ks])
    G = _dot(tril, hi) + _dot(tril, mid) + _dot(tril, lo)
    q = q_ref[rows, ks]
    k = k_ref[rows, ks]
    v = v_ref[rows, vs]
    v16 = v.astype(bf16)
    o_inter = _dot((q * jnp.exp(G)).astype(bf16), S_h.astype(bf16))

    o_rows = []
    for b in range(C // U):
        r0 = b * U
        Gb, qb = G[r0:r0 + U], q[r0:r0 + U]
        o_b = o_inter[r0:r0 + U]
        if b > 0:
            ref = G[r0 - 1:r0]
            qg = qb * jnp.exp(Gb - ref)
            kg = k[:r0] * jnp.exp(ref - G[:r0])
            a = _dot_nt(qg.astype(bf16), kg.astype(bf16))
            o_b = o_b + _dot(a.astype(bf16), v16[:r0])
        for j in range(U):
            s = r0 + j
            e = jnp.exp(jnp.minimum(Gb - G[s:s + 1], 0.0))
            a = jnp.sum(qb * e * k[s:s + 1], axis=-1, keepdims=True)
            a = jnp.where(row_in_sub >= j, a, 0.0)
            o_b = o_b + a * v[s:s + 1]
        o_rows.append(o_b)
    o = jnp.concatenate(o_rows, axis=0)

    on = o * lax.rsqrt(jnp.mean(o * o, axis=-1, keepdims=True) + RMS_EPS) * ng_ref[:, vs]
    x = r_ref[rows, vs] + br_ref[:, vs]
    o_ref[rows, vs] = (on * (x / (1.0 + jnp.exp(-x)))).astype(o_ref.dtype)

    kl = k * jnp.exp(G[C - 1:C] - G)
    decay_col = jnp.exp(G.T[:, C - 1:C])
    return decay_col * S_h + _dot_tn(kl.astype(bf16), v16)


def _gla_body(q_ref, k_ref, v_ref, r_ref, gl_ref, w2_ref, bg_ref, br_ref, ng_ref, s0_ref,
              o_ref, sp_ref, ss_ref, S, *, n_prompt_steps, n_heads, dk, dkp, dv):
    c = pl.program_id(0)
    C = GLA_CHUNK
    is_sample = c >= n_prompt_steps

    @pl.when(c == 0)
    def _():
        S[...] = jnp.zeros_like(S)

    log_a = _log_sigmoid(_dot(gl_ref[...].astype(bf16), w2_ref[...]) + bg_ref[...]) * (1.0 / GLA_GATE_TEMP)
    tril = _tri(C, C, lambda r, cc: r >= cc)
    refs = (q_ref, k_ref, v_ref, r_ref, br_ref, ng_ref, o_ref)

    for h in range(n_heads):
        S_h = S_prompt = S[h]
        for ci in range(GLA_STEP_CHUNKS):
            s0 = jnp.concatenate([s0_ref[ci, h], jnp.zeros((dkp - dk, dv), f32)], axis=0)
            S_h = _gla_chunk(slice(ci * C, (ci + 1) * C), h, log_a, tril, *refs,
                             jnp.where(is_sample, s0, S_h), dkp=dkp, dv=dv)
            ss_ref[ci, h] = S_h[:dk]
        S_prompt = jnp.where(is_sample, S_prompt, S_h)
        S[h] = S_prompt
        sp_ref[h] = S_prompt[:dk]


def _gla(q, k, v, r, glow, w2, bg, br, ng, s0, t_prompt, n_heads, dk, dkp, dv):
    M = q.shape[0]
    rows = GLA_CHUNK * GLA_STEP_CHUNKS
    B = s0.shape[0]
    n_p = t_prompt // rows
    row = lambda width: pl.BlockSpec((rows, width), lambda c: (c, 0))
    const = lambda a: pl.BlockSpec(a.shape, lambda c: (0,) * a.ndim)
    sample_state = pl.BlockSpec((GLA_STEP_CHUNKS, n_heads, dk, dv), lambda c: (jnp.maximum(c - n_p, 0), 0, 0, 0))
    return pl.pallas_call(
        functools.partial(_gla_body, n_prompt_steps=n_p, n_heads=n_heads, dk=dk, dkp=dkp, dv=dv),
        grid=(M // rows,),
        in_specs=[row(n_heads * dkp), row(n_heads * dkp), row(n_heads * dv), row(n_heads * dv),
                  row(glow.shape[1]), const(w2), const(bg), const(br), const(ng), sample_state],
        out_specs=[row(n_heads * dv), pl.BlockSpec((n_heads, dk, dv), lambda c: (0, 0, 0)), sample_state],
        out_shape=[jax.ShapeDtypeStruct((M, n_heads * dv), bf16),
                   jax.ShapeDtypeStruct((n_heads, dk, dv), f32),
                   jax.ShapeDtypeStruct((B, n_heads, dk, dv), f32)],
        scratch_shapes=[pltpu.VMEM((n_heads, dkp, dv), f32)],
        compiler_params=_params(1),
        name="gla",
    )(q, k, v, r, glow, w2, bg, br, ng, s0)


def _out_proj_body(o_ref, om_ref, x_ref, w_ref, y_ref, *, wo):
    y_ref[...] = x_ref[...] + _dot(o_ref[...], w_ref[:wo, :]) + _dot(om_ref[...], w_ref[wo:, :])


def _out_proj(o, om, x, w, tm):
    M, D = x.shape
    wo, wm = o.shape[1], om.shape[1]
    return pl.pallas_call(
        functools.partial(_out_proj_body, wo=wo),
        grid=(M // tm,),
        in_specs=[pl.BlockSpec((tm, wo), lambda i: (i, 0)),
                  pl.BlockSpec((tm, wm), lambda i: (i, 0)),
                  pl.BlockSpec((tm, D), lambda i: (i, 0)),
                  pl.BlockSpec((wo + wm, D), lambda i: (0, 0))],
        out_specs=pl.BlockSpec((tm, D), lambda i: (i, 0)),
        out_shape=jax.ShapeDtypeStruct((M, D), f32),
        compiler_params=_params(1),
        name="out_proj",
    )(o, om, x, w)


def _mlp_body(x_ref, g_ref, wu_ref, wd_ref, gf_ref, y_ref, h_ref, *, final_norm):
    j = pl.program_id(1)

    @pl.when(j == 0)
    def _():
        x = x_ref[...]
        h_ref[...] = _rms(x, g_ref[...]).astype(bf16)
        y_ref[...] = x

    u = jnp.maximum(_dot(h_ref[...], wu_ref[...].astype(bf16)), 0.0)
    y_ref[...] += _dot((u * u).astype(bf16), wd_ref[...].astype(bf16))

    if final_norm:
        @pl.when(j == pl.num_programs(1) - 1)
        def _():
            y_ref[...] = _rms(y_ref[...], gf_ref[...])


def _mlp(x, g, wu, wd, layer, gf, final_norm, tm, tf):
    M, D = x.shape
    F = wu.shape[2]
    return pl.pallas_call(
        functools.partial(_mlp_body, final_norm=final_norm),
        grid=(M // tm, F // tf),
        in_specs=[pl.BlockSpec((tm, D), lambda i, j: (i, 0), pipeline_mode=pl.Buffered(1)),
                  pl.BlockSpec((1, D), lambda i, j: (0, 0)),
                  pl.BlockSpec((None, D, tf), lambda i, j: (layer, 0, j)),
                  pl.BlockSpec((None, tf, D), lambda i, j: (layer, j, 0)),
                  pl.BlockSpec((1, D), lambda i, j: (0, 0))],
        out_specs=pl.BlockSpec((tm, D), lambda i, j: (i, 0)),
        out_shape=jax.ShapeDtypeStruct((M, D), f32),
        scratch_shapes=[pltpu.VMEM((tm, D), bf16)],
        compiler_params=_params(2),
        name="mlp",
    )(x, g.reshape(1, D), wu, wd, gf.reshape(1, D))


def _largest_divisor(n, cap, mult):
    best = None
    for d in range(mult, min(n, cap) + 1, mult):
        if n % d == 0:
            best = d
    assert best is not None, (n, cap, mult)
    return best


def kernel(x_prompt, x_sample, cache_fox_k, cache_fox_v, cache_fox_logf, cache_sb_k, cache_sb_v, state_gla,
           cache_mem_k, cache_mem_v, mem_prompt, norm_mix_g, norm_mlp_g, norm_mem_g, norm_final_g, w_mem_kv,
           w_in_fox, b_forget, w_out_fox, w_in_sb, w_out_sb, w_in_gla, w_gate2_gla, b_gate_gla, b_outgate_gla,
           norm_gla_g, w_out_gla, w_up, w_down):
    Bp, Tp0, D = x_prompt.shape
    Bs, Ts, _ = x_sample.shape
    assert Bp == 1, "the prompt group is handled as one sequence"
    Tp = Bp * Tp0
    M = Tp + Bs * Ts
    depth = norm_mix_g.shape[0]
    H, hd = cache_fox_k.shape[-2:]
    aw = H * hd
    past = cache_fox_k.shape[2]
    Mm, MH = cache_mem_k.shape[2], cache_mem_k.shape[3]
    mw = MH * hd
    GH, dk, dv = state_gla.shape[-3:]
    dkp = -(-dk // LANES) * LANES
    rank = w_gate2_gla.shape[1]
    C = GLA_CHUNK
    assert Ts == C and Tp % (C * GLA_STEP_CHUNKS) == 0 and Bs % GLA_STEP_CHUNKS == 0 and M % 16 == 0

    tm = _largest_divisor(M, 512, 16)
    tm_st = _largest_divisor(math.gcd(Tp, M - Tp), 512, Ts)
    tm_mlp = _largest_divisor(M, 1152, 16)
    tf = _largest_divisor(w_up.shape[2], 512, LANES)
    ta = _largest_divisor(Tp, 512, 2 * LANES)
    tg = _largest_divisor(Tp, 512, LANES)
    tq_mem = _largest_divisor(Tp, 512, 16)
    tp_sb = _largest_divisor(past // 2, 256, LANES)
    att_scale = hd ** -0.5
    att_scale2 = att_scale * LOG2E

    x = jnp.concatenate([x_prompt.reshape(Tp, D), x_sample.reshape(Bs * Ts, D)], axis=0)
    mk_p, mv_p = _memkv(mem_prompt.reshape(Mm, D), norm_mem_g, w_mem_kv)

    w16_fox = w_in_fox.astype(bf16)
    n_fox, n_sb = w_in_fox.shape[0], w_in_sb.shape[0]
    kv_state = {0: None, 1: None}
    logf_p, logf_s, gla_p, gla_s = [], [], [], []
    for i in range(depth):
        kind, j = i % 3, i // 3
        g = norm_mix_g[i]
        if kind in (0, 1):
            w = (w_in_fox if kind == 0 else w_in_sb)[j]
            w_all = w16_fox if kind == 0 else w_in_sb
            n_gate = H if kind == 0 else 0
            (q,) = _proj(x, g, w_all, [(0, aw, bf16, att_scale2)], tm, w_block=(j, 0, aw))
            n_layers = n_fox if kind == 0 else n_sb
            st = kv_state[kind]
            k, *st_k = _proj_state(x, g, w_all, (j, 1, aw), j, st[0] if st else n_layers, Tp, Ts, tm_st, H, hd)
            v, *st_v = _proj_state(x, g, w_all, (j, 2, aw), j, st[1] if st else n_layers, Tp, Ts, tm_st, H, hd)
            kv_state[kind] = (st_k, st_v)
            (qm,) = _proj(x, g, w[:, 3 * aw + n_gate:].astype(bf16), [(0, mw, bf16, att_scale)], tm)
            if kind == 0:
                wf_t = jnp.pad(w[:, 3 * aw:3 * aw + H].T, ((0, FOX_GATE_ROWS - H), (0, 0))).astype(bf16)
                bf_col = jnp.pad(b_forget[j], (0, FOX_GATE_ROWS - H)).reshape(FOX_GATE_ROWS, 1)
                lf_p, cf_p = _fox_gate_prompt(x, g, wf_t, bf_col, Tp, tg)
                plogf_t = jnp.pad(jnp.swapaxes(cache_fox_logf[j], 1, 2), ((0, 0), (0, FOX_GATE_ROWS - H), (0, 0)))
                lf_s, cf_s = _fox_gate_sample(x, g, wf_t, bf_col, plogf_t, Tp, Ts)
                o = _fox_prompt(q, k, v, cf_p, Tp, H, hd, ta)
                o = _fox_sample(q, k, v, jnp.swapaxes(cache_fox_k, 2, 3), jnp.swapaxes(cache_fox_v, 2, 3), j,
                                cf_s, o, Tp, Ts)
                logf_p.append(lf_p[:H].T.reshape(Bp, Tp0, H))
                logf_s.append(jnp.swapaxes(lf_s[:, :H], 1, 2))
                w_out = w_out_fox[j]
            else:
                o = _sb_prompt(q, k, v, Tp, H, hd, ta)
                o = _sb_sample(q, k, v, jnp.swapaxes(cache_sb_k, 2, 3), jnp.swapaxes(cache_sb_v, 2, 3), j,
                               o, Tp, Ts, tp_sb)
                w_out = w_out_sb[j]
        else:
            w = w_in_gla[j]
            kw, vw = GH * dk, GH * dv

            def pad_heads(a):
                a = a.reshape(a.shape[:-1] + (GH, dk))
                a = jnp.pad(a, [(0, 0)] * (a.ndim - 1) + [(0, dkp - dk)])
                return a.reshape(a.shape[:-2] + (GH * dkp,))

            wqk = jnp.concatenate([pad_heads(w[:, :kw]), pad_heads(w[:, kw:2 * kw])], axis=1).astype(bf16)
            wv = w[:, 2 * kw:2 * kw + vw].astype(bf16)
            wr = w[:, 2 * kw + vw:2 * kw + 2 * vw].astype(bf16)
            c0 = 2 * kw + 2 * vw
            wmg = jnp.concatenate([w[:, c0 + rank:], jnp.pad(w[:, c0:c0 + rank], ((0, 0), (0, LANES - rank)))],
                                  axis=1).astype(bf16)
            q, k = _proj(x, g, wqk, [(0, GH * dkp, f32, dk ** -0.5), (GH * dkp, 2 * GH * dkp, f32, 1.0)], tm)
            (v,) = _proj(x, g, wv, [(0, vw, f32, 1.0)], tm)
            (r,) = _proj(x, g, wr, [(0, vw, f32, 1.0)], tm)
            qm, glow = _proj(x, g, wmg, [(0, mw, bf16, att_scale), (mw, mw + LANES, f32, 1.0)], tm)
            w2 = jnp.pad(pad_heads(w_gate2_gla[j]), ((0, LANES - rank), (0, 0))).astype(bf16)
            bg = pad_heads(b_gate_gla[j]).reshape(1, GH * dkp)
            o, st_p, st_s = _gla(q, k, v, r, glow, w2, bg, b_outgate_gla[j].reshape(1, vw),
                                 norm_gla_g[j].reshape(1, vw), state_gla[j], Tp, GH, dk, dkp, dv)
            gla_p.append(st_p[None])
            gla_s.append(st_s)
            w_out = w_out_gla[j]

        om = _mem_attn_prompt(qm, mk_p[i], mv_p[i], Tp, tq_mem, MH, hd)
        om = _mem_attn_sample(qm, cache_mem_k, cache_mem_v, i, om, Tp, Ts)
        x = _out_proj(o, om, x, w_out.astype(bf16), tm)
        x = _mlp(x, norm_mlp_g[i], w_up, w_down, i, norm_final_g, i == depth - 1, tm_mlp, tf)

    def prompt_state(a):
        return jnp.swapaxes(a.reshape((a.shape[0], Bp, H, Tp0, hd)), 2, 3)

    def sample_state(a):
        return jnp.swapaxes(a, 2, 3)

    (fox_k, fox_v), (sb_k, sb_v) = kv_state[0], kv_state[1]
    return (x[:Tp].reshape(Bp, Tp0, D), x[Tp:].reshape(Bs, Ts, D),
            prompt_state(fox_k[0]), prompt_state(fox_v[0]), jnp.stack(logf_p),
            prompt_state(sb_k[0]), prompt_state(sb_v[0]),
            jnp.stack(gla_p),
            mk_p.reshape(depth, Bp, Mm, MH, hd), mv_p.reshape(depth, Bp, Mm, MH, hd),
            sample_state(fox_k[1]), sample_state(fox_v[1]), jnp.stack(logf_s),
            sample_state(sb_k[1]), sample_state(sb_v[1]),
            jnp.stack(gla_s))
```

```python
import functools
import math

import jax
import jax.numpy as jnp
from jax import lax
from jax.experimental import pallas as pl
from jax.experimental.pallas import tpu as pltpu

f32 = jnp.float32
bf16 = jnp.bfloat16

RMS_EPS = 1e-6
GLA_GATE_TEMP = 16.0
GLA_CHUNK = 64
GLA_SUB = 8
GLA_STEP_CHUNKS = 2
LOG2E = 1.4426950408889634
MASKED_LOGIT = -1e30
SKIP_BITS = 160.0
LANES = 128
FOX_GATE_ROWS = 16
VMEM_LIMIT_BYTES = 56 * 1024 * 1024


def _params(n_axes):
    return pltpu.CompilerParams(dimension_semantics=("arbitrary",) * n_axes,
                                vmem_limit_bytes=VMEM_LIMIT_BYTES)


def _rms(x, g):
    return x * lax.rsqrt(jnp.mean(x * x, axis=-1, keepdims=True) + RMS_EPS) * g


def _log_sigmoid(z):
    return jnp.minimum(z, 0.0) - jnp.log1p(jnp.exp(-jnp.abs(z)))


def _split3(x):
    hi = x.astype(bf16)
    r = x - hi.astype(f32)
    mid = r.astype(bf16)
    lo = (r - mid.astype(f32)).astype(bf16)
    return hi, mid, lo


def _split2(x):
    hi = x.astype(bf16)
    lo = (x - hi.astype(f32)).astype(bf16)
    return hi, lo


def _tri(n, m, fn):
    r = lax.broadcasted_iota(jnp.int32, (n, m), 0)
    c = lax.broadcasted_iota(jnp.int32, (n, m), 1)
    return jnp.where(fn(r, c), 1.0, 0.0).astype(bf16)


def _dot(a, b):
    return jnp.dot(a, b, preferred_element_type=f32)


def _dot_nt(a, b):
    return lax.dot_general(a, b, (((1,), (1,)), ((), ())), preferred_element_type=f32)


def _dot_tn(a, b):
    return lax.dot_general(a, b, (((0,), (0,)), ((), ())), preferred_element_type=f32)


def _cumsum_lanes(x, carry, tri):
    n = x.shape[1]
    outs = []
    for b0 in range(0, n, LANES):
        w = min(LANES, n - b0)
        hi, mid, lo = _split3(x[:, b0:b0 + w])
        t = tri[:w, :w]
        c = _dot(hi, t) + _dot(mid, t) + _dot(lo, t) + carry
        carry = c[:, w - 1:w]
        outs.append(c)
    return outs, carry


def _proj_body(x_ref, g_ref, w_ref, *o_refs, cols, scales):
    h = _rms(x_ref[...], g_ref[...]).astype(bf16)
    y = _dot(h, w_ref[...].astype(bf16))
    for o_ref, (c0, c1), sc in zip(o_refs, cols, scales):
        part = y[:, c0:c1]
        if sc != 1.0:
            part = part * sc
        o_ref[...] = part.astype(o_ref.dtype)


def _stacked_weight_spec(D, width, layer, col_block):
    return pl.BlockSpec((None, D, width), lambda i: (layer, 0, col_block), pipeline_mode=pl.Buffered(1))


def _proj(x, g, w, outs, tm, w_block=None):
    M, D = x.shape
    if w_block is None:
        w_spec = pl.BlockSpec(w.shape, lambda i: (0, 0))
    else:
        layer, col_block, width = w_block
        w_spec = _stacked_weight_spec(D, width, layer, col_block)
    cols = tuple((c0, c1) for c0, c1, _, _ in outs)
    scales = tuple(float(s) for _, _, _, s in outs)
    return pl.pallas_call(
        functools.partial(_proj_body, cols=cols, scales=scales),
        grid=(M // tm,),
        in_specs=[pl.BlockSpec((tm, D), lambda i: (i, 0)),
                  pl.BlockSpec((1, D), lambda i: (0, 0)),
                  w_spec],
        out_specs=[pl.BlockSpec((tm, c1 - c0), lambda i: (i, 0)) for c0, c1 in cols],
        out_shape=[jax.ShapeDtypeStruct((M, c1 - c0), dt) for c0, c1, dt, _ in outs],
        compiler_params=_params(1),
        name="norm_proj",
    )(x, g.reshape(1, D), w)


def _proj_state_body(x_ref, g_ref, w_ref, *refs, n_prompt_blocks, n_heads, hd, ts):
    a_ref, sp_ref, ss_ref = refs[-3:]
    i = pl.program_id(0)
    h = _rms(x_ref[...], g_ref[...]).astype(bf16)
    y = _dot(h, w_ref[...].astype(bf16))
    a_ref[...] = y.astype(a_ref.dtype)

    @pl.when(i < n_prompt_blocks)
    def _():
        for hh in range(n_heads):
            sp_ref[hh] = y[:, hh * hd:(hh + 1) * hd]

    @pl.when(i >= n_prompt_blocks)
    def _():
        for b in range(ss_ref.shape[0]):
            for hh in range(n_heads):
                ss_ref[b, hh] = y[b * ts:(b + 1) * ts, hh * hd:(hh + 1) * hd]


def _proj_state(x, g, w, w_block, slot, prev, t_prompt, ts, tm, n_heads, hd):
    M, D = x.shape
    layer, col_block, width = w_block
    n_p = t_prompt // tm
    in_specs = [pl.BlockSpec((tm, D), lambda i: (i, 0)),
                pl.BlockSpec((1, D), lambda i: (0, 0)),
                _stacked_weight_spec(D, width, layer, col_block)]
    args = [x, g.reshape(1, D), w]
    if isinstance(prev, int):
        shapes = [jax.ShapeDtypeStruct((prev, n_heads, t_prompt, hd), f32),
                  jax.ShapeDtypeStruct((prev, (M - t_prompt) // ts, n_heads, ts, hd), f32)]
        aliases = {}
    else:
        shapes = [jax.ShapeDtypeStruct(p.shape, p.dtype) for p in prev]
        in_specs += [pl.BlockSpec(memory_space=pl.ANY)] * 2
        args += list(prev)
        aliases = {3: 1, 4: 2}
    return pl.pallas_call(
        functools.partial(_proj_state_body, n_prompt_blocks=n_p, n_heads=n_heads, hd=hd, ts=ts),
        grid=(M // tm,),
        in_specs=in_specs,
        out_specs=[pl.BlockSpec((tm, width), lambda i: (i, 0)),
                   pl.BlockSpec((None, n_heads, tm, hd), lambda i: (slot, 0, jnp.minimum(i, n_p - 1), 0)),
                   pl.BlockSpec((None, tm // ts, n_heads, ts, hd),
                                lambda i: (slot, jnp.maximum(i - n_p, 0), 0, 0, 0))],
        out_shape=[jax.ShapeDtypeStruct((M, width), bf16)] + shapes,
        input_output_aliases=aliases,
        compiler_params=_params(1),
        name="norm_proj_state",
    )(*args)


def _memkv_body(m_ref, g_ref, w_ref, k_ref, v_ref, *, mw):
    h = _rms(m_ref[...], g_ref[...]).astype(bf16)
    w = w_ref[...].astype(bf16)
    k_ref[...] = _dot(h, w[:, :mw])
    v_ref[...] = _dot(h, w[:, mw:])


def _memkv(mem, g, w):
    L, D, two_mw = w.shape
    Mm = mem.shape[0]
    mw = two_mw // 2
    return pl.pallas_call(
        functools.partial(_memkv_body, mw=mw),
        grid=(L,),
        in_specs=[pl.BlockSpec((Mm, D), lambda l: (0, 0)),
                  pl.BlockSpec((None, 1, D), lambda l: (l, 0, 0)),
                  pl.BlockSpec((None, D, two_mw), lambda l: (l, 0, 0))],
        out_specs=[pl.BlockSpec((None, Mm, mw), lambda l: (l, 0, 0))] * 2,
        out_shape=[jax.ShapeDtypeStruct((L, Mm, mw), f32)] * 2,
        compiler_params=_params(1),
        name="mem_kv",
    )(mem, g.reshape(L, 1, D), w)


def _gate_logits(x_ref, g_ref, wf_ref, bf_ref):
    h = _rms(x_ref[...], g_ref[...]).astype(bf16)
    return _log_sigmoid(_dot_nt(wf_ref[...], h) + bf_ref[...])


def _fox_gate_prompt_body(x_ref, g_ref, wf_ref, bf_ref, lf_ref, cf_ref, carry_ref):
    @pl.when(pl.program_id(0) == 0)
    def _():
        carry_ref[...] = jnp.zeros_like(carry_ref)

    logf = _gate_logits(x_ref, g_ref, wf_ref, bf_ref)
    lf_ref[...] = logf
    tri = _tri(LANES, LANES, lambda r, c: r <= c)
    outs, carry = _cumsum_lanes(logf, carry_ref[:, 0:1], tri)
    for b, c in enumerate(outs):
        cf_ref[:, b * LANES:(b + 1) * LANES] = c * LOG2E
    carry_ref[...] = jnp.broadcast_to(carry, carry_ref.shape)


def _fox_gate_prompt(x, g, wf_t, bf, t_prompt, tg):
    D = x.shape[1]
    R = wf_t.shape[0]
    return pl.pallas_call(
        _fox_gate_prompt_body,
        grid=(t_prompt // tg,),
        in_specs=[pl.BlockSpec((tg, D), lambda i: (i, 0)),
                  pl.BlockSpec((1, D), lambda i: (0, 0)),
                  pl.BlockSpec((R, D), lambda i: (0, 0)),
                  pl.BlockSpec((R, 1), lambda i: (0, 0))],
        out_specs=[pl.BlockSpec((R, tg), lambda i: (0, i))] * 2,
        out_shape=[jax.ShapeDtypeStruct((R, t_prompt), f32)] * 2,
        scratch_shapes=[pltpu.VMEM((R, LANES), f32)],
        compiler_params=_params(1),
        name="fox_gate_prompt",
    )(x, g.reshape(1, D), wf_t, bf)


def _fox_gate_sample_body(x_ref, g_ref, wf_ref, bf_ref, pl_ref, lf_ref, cf_ref, *, past, ts):
    logf = _gate_logits(x_ref, g_ref, wf_ref, bf_ref)
    lf_ref[...] = logf
    tri = _tri(LANES, LANES, lambda r, c: r <= c)
    zero = jnp.zeros((logf.shape[0], 1), f32)
    outs, carry = _cumsum_lanes(pl_ref[...], zero, tri)
    for b, c in enumerate(outs):
        cf_ref[:, b * LANES:(b + 1) * LANES] = c * LOG2E
    (new,), _ = _cumsum_lanes(logf, carry, tri)
    cf_ref[:, past:past + ts] = new * LOG2E
    cf_ref[:, past + ts:] = jnp.zeros((logf.shape[0], LANES - ts), f32)


def _fox_gate_sample(x, g, wf_t, bf, plogf_t, row0, ts):
    D = x.shape[1]
    R = wf_t.shape[0]
    B, _, past = plogf_t.shape
    blk0 = row0 // ts
    return pl.pallas_call(
        functools.partial(_fox_gate_sample_body, past=past, ts=ts),
        grid=(B,),
        in_specs=[pl.BlockSpec((ts, D), lambda b: (blk0 + b, 0)),
                  pl.BlockSpec((1, D), lambda b: (0, 0)),
                  pl.BlockSpec((R, D), lambda b: (0, 0)),
                  pl.BlockSpec((R, 1), lambda b: (0, 0)),
                  pl.BlockSpec((None, R, past), lambda b: (b, 0, 0))],
        out_specs=[pl.BlockSpec((None, R, ts), lambda b: (b, 0, 0)),
                   pl.BlockSpec((None, R, past + LANES), lambda b: (b, 0, 0))],
        out_shape=[jax.ShapeDtypeStruct((B, R, ts), f32),
                   jax.ShapeDtypeStruct((B, R, past + LANES), f32)],
        compiler_params=_params(1),
        name="fox_gate_sample",
    )(x, g.reshape(1, D), wf_t, bf, plogf_t)


def _softmax_tile_wide(s, m, acc, v):
    m_new = jnp.maximum(m, jnp.max(s, axis=-1, keepdims=True))
    p = jnp.exp2((s - m_new).astype(bf16))
    return m_new, jnp.exp2(m - m_new) * acc + _dot(p, v)


def _causal(n, strict):
    r = lax.broadcasted_iota(jnp.int32, (n, n), 0)
    c = lax.broadcasted_iota(jnp.int32, (n, n), 1)
    return c < r if strict else c <= r


def _fox_prompt_body(q_ref, k16, v16, f_ref, o_ref, knorm_ref, *, t):
    qi = pl.program_id(1)

    @pl.when(qi == 0)
    def _():
        kf = k16[...].astype(f32)
        k_sq = jnp.max(jnp.sum(kf * kf, axis=-1, keepdims=True), axis=0, keepdims=True)
        knorm_ref[...] = jnp.broadcast_to(jnp.sqrt(k_sq), knorm_ref.shape)

    q = q_ref[...]
    qf = q.astype(f32)
    qk_bound = jnp.sqrt(jnp.sum(qf * qf, axis=-1, keepdims=True)) * knorm_ref[0:1, 0:1]
    col_minus_row = (lax.broadcasted_iota(jnp.int32, (t, t), 1) - lax.broadcasted_iota(jnp.int32, (t, t), 0))

    def scores(kb, max_col_minus_row):
        rows = pl.ds(pl.multiple_of(kb * t, t), t)
        s = _dot_nt(q, k16[rows, :]) - f_ref[pl.ds(kb, 1), :]
        if max_col_minus_row is not None:
            s = jnp.where(col_minus_row <= max_col_minus_row, s, MASKED_LOGIT)
        return s, jnp.concatenate([v16[rows, :], ones], axis=1)

    def tile(kb, carry, max_col_minus_row):
        s, v = scores(kb, max_col_minus_row)
        return _softmax_tile_wide(s, *carry, v)

    def rest_is_zero(a, b, kb):
        bias_max = jnp.max(-f_ref[pl.ds(kb, 1), :])
        return (jnp.max(qk_bound + bias_max - jnp.maximum(a[0], b[0])) < -SKIP_BITS).astype(jnp.int32)

    hd = q.shape[1]
    ones = jnp.ones((t, hd), bf16)
    init = (jnp.full((t, 1), MASKED_LOGIT, f32), jnp.zeros((t, 2 * hd), f32))
    odd = qi % 2
    a = tile(qi, init, 0)
    b = lax.cond(odd == 1, lambda: tile(qi - 1, init, None), lambda: init)
    first = qi - 1 - odd
    n_pairs = qi // 2

    def pair(c):
        i, a, b, _ = c
        kb = first - 2 * i
        s_a, v_a = scores(kb, None)
        s_b, v_b = scores(kb - 1, None)
        a = _softmax_tile_wide(s_a, *a, v_a)
        b = _softmax_tile_wide(s_b, *b, v_b)
        return i + 1, a, b, rest_is_zero(a, b, jnp.maximum(kb - 2, 0))

    _, (m_a, acc_a), (m_b, acc_b), _ = lax.while_loop(
        lambda c: jnp.logical_and(c[0] < n_pairs, c[3] == 0), pair,
        (jnp.int32(0), a, b, rest_is_zero(a, b, jnp.maximum(first, 0))))
    m = jnp.maximum(m_a, m_b)
    acc = jnp.exp2(m_a - m) * acc_a + jnp.exp2(m_b - m) * acc_b
    o_ref[...] = (acc[:, :hd] / acc[:, hd:]).astype(o_ref.dtype)


def _fox_prompt(q, k, v, cf, t_prompt, n_heads, hd, t):
    M = q.shape[0]
    nq = t_prompt // t
    return pl.pallas_call(
        functools.partial(_fox_prompt_body, t=t),
        grid=(n_heads, nq),
        in_specs=[pl.BlockSpec((t, hd), lambda h, i: (i, h)),
                  pl.BlockSpec((t_prompt, hd), lambda h, i: (0, h)),
                  pl.BlockSpec((t_prompt, hd), lambda h, i: (0, h)),
                  pl.BlockSpec((None, nq, t), lambda h, i: (h, 0, 0))],
        out_specs=pl.BlockSpec((t, hd), lambda h, i: (i, h)),
        out_shape=jax.ShapeDtypeStruct((M, n_heads * hd), bf16),
        scratch_shapes=[pltpu.VMEM((8, LANES), f32)],
        compiler_params=_params(2),
        name="fox_prompt",
    )(q, k, v, cf.reshape(cf.shape[0], nq, t))


def _fox_sample_body(q_ref, k_ref, v_ref, pk_ref, pv_ref, f_ref, _, o_ref, *, past, ts, n_heads, hd):
    for h in range(n_heads):
        cols = slice(h * hd, (h + 1) * hd)
        q = q_ref[:, cols]
        f = f_ref[h:h + 1, :]
        s_past = _dot_nt(q, pk_ref[h].astype(bf16)) - f[:, :past]
        s_new = _dot_nt(q, k_ref[:, cols].astype(bf16)) - f[:, past:past + ts]
        s_new = jnp.where(_causal(ts, strict=False), s_new, -jnp.inf)
        m = jnp.maximum(jnp.max(s_past, axis=-1, keepdims=True), jnp.max(s_new, axis=-1, keepdims=True))
        p_past = jnp.exp2(s_past - m)
        p_new = jnp.exp2(s_new - m)
        l = jnp.sum(p_past, axis=-1, keepdims=True) + jnp.sum(p_new, axis=-1, keepdims=True)
        acc = (_dot(p_past.astype(bf16), pv_ref[h].astype(bf16))
               + _dot(p_new.astype(bf16), v_ref[:, cols].astype(bf16)))
        o_ref[:, cols] = (acc / l).astype(o_ref.dtype)


def _sample_attn_call(body, name, q, k, v, pk, pv, layer, extra, o_buf, row0, ts):
    _, B, n_heads, past, hd = pk.shape
    blk0 = row0 // ts
    new_spec = pl.BlockSpec((ts, n_heads * hd), lambda b: (blk0 + b, 0))
    past_spec = pl.BlockSpec((None, None, n_heads, past, hd), lambda b: (layer, b, 0, 0, 0))
    in_specs = [new_spec, new_spec, new_spec, past_spec, past_spec]
    args = [q, k, v, pk, pv]
    if extra is not None:
        in_specs.append(pl.BlockSpec((None,) + extra.shape[1:], lambda b: (b, 0, 0)))
        args.append(extra)
    in_specs.append(pl.BlockSpec(memory_space=pl.ANY))
    args.append(o_buf)
    return pl.pallas_call(
        functools.partial(body, past=past, ts=ts, n_heads=n_heads, hd=hd),
        grid=(B,),
        in_specs=in_specs,
        out_specs=new_spec,
        out_shape=jax.ShapeDtypeStruct(o_buf.shape, o_buf.dtype),
        input_output_aliases={len(args) - 1: 0},
        compiler_params=_params(1),
        name=name,
    )(*args)


def _fox_sample(q, k, v, pk, pv, layer, cf, o_buf, row0, ts):
    return _sample_attn_call(_fox_sample_body, "fox_sample", q, k, v, pk, pv, layer, cf, o_buf, row0, ts)


def _tri2(n):
    tri = _tri(n, n, lambda r, c: r >= c)
    return jnp.concatenate([tri, tri], axis=0)


def _sb_scores(q, k, tri2, valid):
    z = _dot_nt(q, k)
    sp = jnp.maximum(z, 0.0) + jnp.log(1.0 + jnp.exp2(-jnp.abs(z))) * LOG2E
    if valid is not None:
        sp = jnp.where(valid, sp, 0.0)
    cum = _dot(jnp.concatenate(_split2(sp), axis=1), tri2)
    return z - cum, cum[:, 0:1]


def _sb_out(zc, run, v, valid):
    a = jnp.exp2(zc)
    if valid is not None:
        a = jnp.where(valid, a, 0.0)
    return jnp.exp2(-run) * _dot(a.astype(bf16), v)


def _sb_pair(q, right, left, tri2, run, acc, valid_right=None, valid_left=None):
    (k_r, v_r), (k_l, v_l) = right, left
    zc_r, tot_r = _sb_scores(q, k_r, tri2, valid_right)
    zc_l, tot_l = _sb_scores(q, k_l, tri2, valid_left)
    acc = acc + _sb_out(zc_r, run, v_r, valid_right) + _sb_out(zc_l, run + tot_r, v_l, valid_left)
    return run + tot_r + tot_l, acc


def _sb_prompt_body(q_ref, k16, v16, o_ref, tri2_ref, *, tq, tk):
    qi = pl.program_id(1)

    @pl.when(qi == 0)
    def _():
        tri2_ref[...] = _tri2(tk)

    q = q_ref[...]
    tri2 = tri2_ref[...]

    def kv(kb):
        rows = pl.ds(pl.multiple_of(kb * tk, tk), tk)
        return k16[rows, :], v16[rows, :]

    r = lax.broadcasted_iota(jnp.int32, (tq, tk), 0)
    c = lax.broadcasted_iota(jnp.int32, (tq, tk), 1)
    init = (jnp.zeros((tq, 1), f32), jnp.zeros((tq, q.shape[1]), f32))
    carry = _sb_pair(q, kv(2 * qi + 1), kv(2 * qi), tri2, *init, valid_right=c + tk < r, valid_left=c < r)

    def all_rows_spent(run):
        return (jnp.min(run) >= SKIP_BITS).astype(jnp.int32)

    def below(c):
        i, run, acc, _ = c
        k_t, v_t = kv(2 * qi - 1 - i)
        zc, tot = _sb_scores(q, k_t, tri2, None)
        acc = acc + _sb_out(zc, run, v_t, None)
        return i + 1, run + tot, acc, all_rows_spent(run + tot)

    _, _, acc, _ = lax.while_loop(lambda c: jnp.logical_and(c[0] < 2 * qi, c[3] == 0), below,
                                  (jnp.int32(0), *carry, all_rows_spent(carry[0])))
    o_ref[...] = acc.astype(o_ref.dtype)


def _sb_prompt(q, k, v, t_prompt, n_heads, hd, tq):
    M = q.shape[0]
    tk = tq // 2
    return pl.pallas_call(
        functools.partial(_sb_prompt_body, tq=tq, tk=tk),
        grid=(n_heads, t_prompt // tq),
        in_specs=[pl.BlockSpec((tq, hd), lambda h, i: (i, h)),
                  pl.BlockSpec((t_prompt, hd), lambda h, i: (0, h)),
                  pl.BlockSpec((t_prompt, hd), lambda h, i: (0, h))],
        out_specs=pl.BlockSpec((tq, hd), lambda h, i: (i, h)),
        out_shape=jax.ShapeDtypeStruct((M, n_heads * hd), bf16),
        scratch_shapes=[pltpu.VMEM((2 * tk, tk), bf16)],
        compiler_params=_params(2),
        name="sb_prompt",
    )(q, k, v)


def _sb_sample_body(q_ref, k_ref, v_ref, pk_ref, pv_ref, _, o_ref, *, past, ts, n_heads, hd, tp):
    tri2_new = _tri2(ts)
    tri2 = _tri2(tp)
    strict = _causal(ts, strict=True)
    for h in range(n_heads):
        cols = slice(h * hd, (h + 1) * hd)
        q = q_ref[:, cols]
        zc, run = _sb_scores(q, k_ref[:, cols].astype(bf16), tri2_new, strict)
        acc = _sb_out(zc, jnp.zeros((ts, 1), f32), v_ref[:, cols].astype(bf16), strict)

        def past_kv(p0):
            return pk_ref[h, p0:p0 + tp, :].astype(bf16), pv_ref[h, p0:p0 + tp, :].astype(bf16)

        for p0 in range(past - tp, -1, -2 * tp):
            run, acc = _sb_pair(q, past_kv(p0), past_kv(p0 - tp), tri2, run, acc)
        o_ref[:, cols] = acc.astype(o_ref.dtype)


def _sb_sample(q, k, v, pk, pv, layer, o_buf, row0, ts, tp):
    return _sample_attn_call(functools.partial(_sb_sample_body, tp=tp), "sb_sample",
                             q, k, v, pk, pv, layer, None, o_buf, row0, ts)


def _mem_attn_body(q_ref, k_ref, v_ref, *rest, n_heads, hd):
    o_ref = rest[-1]
    for h in range(n_heads):
        sl = slice(h * hd, (h + 1) * hd)
        head = (lambda ref: ref[:, h, :]) if len(k_ref.shape) == 3 else (lambda ref: ref[:, sl])
        s = _dot_nt(q_ref[:, sl], head(k_ref).astype(bf16))
        p = jnp.exp(s - jnp.max(s, axis=-1, keepdims=True))
        acc = _dot(p.astype(bf16), head(v_ref).astype(bf16))
        o_ref[:, sl] = (acc / jnp.sum(p, axis=-1, keepdims=True)).astype(o_ref.dtype)


def _mem_attn_prompt(qm, mk, mv, t_prompt, tq, n_heads, hd):
    M, W = qm.shape
    Mm = mk.shape[0]
    return pl.pallas_call(
        functools.partial(_mem_attn_body, n_heads=n_heads, hd=hd),
        grid=(t_prompt // tq,),
        in_specs=[pl.BlockSpec((tq, W), lambda i: (i, 0)),
                  pl.BlockSpec((Mm, W), lambda i: (0, 0)),
                  pl.BlockSpec((Mm, W), lambda i: (0, 0))],
        out_specs=pl.BlockSpec((tq, W), lambda i: (i, 0)),
        out_shape=jax.ShapeDtypeStruct((M, W), bf16),
        compiler_params=_params(1),
        name="mem_attn_prompt",
    )(qm, mk, mv)


def _mem_attn_sample(qm, mk, mv, layer, o_buf, row0, ts):
    M, W = qm.shape
    _, B, Mm, n_heads, hd = mk.shape
    blk0 = row0 // ts
    row_spec = pl.BlockSpec((ts, W), lambda b: (blk0 + b, 0))
    mem_spec = pl.BlockSpec((None, None, Mm, n_heads, hd), lambda b: (layer, b, 0, 0, 0))
    return pl.pallas_call(
        functools.partial(_mem_attn_body, n_heads=n_heads, hd=hd),
        grid=(B,),
        in_specs=[row_spec, mem_spec, mem_spec, pl.BlockSpec(memory_space=pl.ANY)],
        out_specs=row_spec,
        out_shape=jax.ShapeDtypeStruct((M, W), bf16),
        input_output_aliases={3: 0},
        compiler_params=_params(1),
        name="mem_attn_sample",
    )(qm, mk, mv, o_buf)


def _gla_chunk(rows, h, log_a, tril, q_ref, k_ref, v_ref, r_ref, br_ref, ng_ref, o_ref, S_h, *, dkp, dv):
    C, U = GLA_CHUNK, GLA_SUB
    ks = slice(h * dkp, (h + 1) * dkp)
    vs = slice(h * dv, (h + 1) * dv)
    row_in_sub = lax.broadcasted_iota(jnp.int32, (U, 1), 0)
    hi, mid, lo = _split3(log_a[rows, ks])
    G = _dot(tril, hi) + _dot(tril, mid) + _dot(tril, lo)
    q = q_ref[rows, ks]
    k = k_ref[rows, ks]
    v = v_ref[rows, vs]
    v16 = v.astype(bf16)
    o_inter = _dot((q * jnp.exp(G)).astype(bf16), S_h.astype(bf16))

    o_rows = []
    for b in range(C // U):
        r0 = b * U
        Gb, qb = G[r0:r0 + U], q[r0:r0 + U]
        o_b = o_inter[r0:r0 + U]
        if b > 0:
            ref = G[r0 - 1:r0]
            qg = qb * jnp.exp(Gb - ref)
            kg = k[:r0] * jnp.exp(ref - G[:r0])
            a = _dot_nt(qg.astype(bf16), kg.astype(bf16))
            o_b = o_b + _dot(a.astype(bf16), v16[:r0])
        for j in range(U):
            s = r0 + j
            e = jnp.exp(jnp.minimum(Gb - G[s:s + 1], 0.0))
            a = jnp.sum(qb * e * k[s:s + 1], axis=-1, keepdims=True)
            a = jnp.where(row_in_sub >= j, a, 0.0)
            o_b = o_b + a * v[s:s + 1]
        o_rows.append(o_b)
    o = jnp.concatenate(o_rows, axis=0)

    on = o * lax.rsqrt(jnp.mean(o * o, axis=-1, keepdims=True) + RMS_EPS) * ng_ref[:, vs]
    x = r_ref[rows, vs] + br_ref[:, vs]
    o_ref[rows, vs] = (on * (x / (1.0 + jnp.exp(-x)))).astype(o_ref.dtype)

    kl = k * jnp.exp(G[C - 1:C] - G)
    decay_col = jnp.exp(G.T[:, C - 1:C])
    return decay_col * S_h + _dot_tn(kl.astype(bf16), v16)


def _gla_body(q_ref, k_ref, v_ref, r_ref, gl_ref, w2_ref, bg_ref, br_ref, ng_ref, s0_ref,
              o_ref, sp_ref, ss_ref, S, *, n_prompt_steps, n_heads, dk, dkp, dv):
    c = pl.program_id(0)
    C = GLA_CHUNK
    is_sample = c >= n_prompt_steps

    @pl.when(c == 0)
    def _():
        S[...] = jnp.zeros_like(S)

    log_a = _log_sigmoid(_dot(gl_ref[...].astype(bf16), w2_ref[...]) + bg_ref[...]) * (1.0 / GLA_GATE_TEMP)
    tril = _tri(C, C, lambda r, cc: r >= cc)
    refs = (q_ref, k_ref, v_ref, r_ref, br_ref, ng_ref, o_ref)

    for h in range(n_heads):
        S_h = S_prompt = S[h]
        for ci in range(GLA_STEP_CHUNKS):
            s0 = jnp.concatenate([s0_ref[ci, h], jnp.zeros((dkp - dk, dv), f32)], axis=0)
            S_h = _gla_chunk(slice(ci * C, (ci + 1) * C), h, log_a, tril, *refs,
                             jnp.where(is_sample, s0, S_h), dkp=dkp, dv=dv)
            ss_ref[ci, h] = S_h[:dk]
        S_prompt = jnp.where(is_sample, S_prompt, S_h)
        S[h] = S_prompt
        sp_ref[h] = S_prompt[:dk]


def _gla(q, k, v, r, glow, w2, bg, br, ng, s0, t_prompt, n_heads, dk, dkp, dv):
    M = q.shape[0]
    rows = GLA_CHUNK * GLA_STEP_CHUNKS
    B = s0.shape[0]
    n_p = t_prompt // rows
    row = lambda width: pl.BlockSpec((rows, width), lambda c: (c, 0))
    const = lambda a: pl.BlockSpec(a.shape, lambda c: (0,) * a.ndim)
    sample_state = pl.BlockSpec((GLA_STEP_CHUNKS, n_heads, dk, dv), lambda c: (jnp.maximum(c - n_p, 0), 0, 0, 0))
    return pl.pallas_call(
        functools.partial(_gla_body, n_prompt_steps=n_p, n_heads=n_heads, dk=dk, dkp=dkp, dv=dv),
        grid=(M // rows,),
        in_specs=[row(n_heads * dkp), row(n_heads * dkp), row(n_heads * dv), row(n_heads * dv),
                  row(glow.shape[1]), const(w2), const(bg), const(br), const(ng), sample_state],
        out_specs=[row(n_heads * dv), pl.BlockSpec((n_heads, dk, dv), lambda c: (0, 0, 0)), sample_state],
        out_shape=[jax.ShapeDtypeStruct((M, n_heads * dv), bf16),
                   jax.ShapeDtypeStruct((n_heads, dk, dv), f32),
                   jax.ShapeDtypeStruct((B, n_heads, dk, dv), f32)],
        scratch_shapes=[pltpu.VMEM((n_heads, dkp, dv), f32)],
        compiler_params=_params(1),
        name="gla",
    )(q, k, v, r, glow, w2, bg, br, ng, s0)


def _out_proj_body(o_ref, om_ref, x_ref, w_ref, y_ref, *, wo):
    y_ref[...] = x_ref[...] + _dot(o_ref[...], w_ref[:wo, :]) + _dot(om_ref[...], w_ref[wo:, :])


def _out_proj(o, om, x, w, tm):
    M, D = x.shape
    wo, wm = o.shape[1], om.shape[1]
    return pl.pallas_call(
        functools.partial(_out_proj_body, wo=wo),
        grid=(M // tm,),
        in_specs=[pl.BlockSpec((tm, wo), lambda i: (i, 0)),
                  pl.BlockSpec((tm, wm), lambda i: (i, 0)),
                  pl.BlockSpec((tm, D), lambda i: (i, 0)),
                  pl.BlockSpec((wo + wm, D), lambda i: (0, 0))],
        out_specs=pl.BlockSpec((tm, D), lambda i: (i, 0)),
        out_shape=jax.ShapeDtypeStruct((M, D), f32),
        compiler_params=_params(1),
        name="out_proj",
    )(o, om, x, w)


def _mlp_body(x_ref, g_ref, wu_ref, wd_ref, gf_ref, y_ref, *rest, split_row):
    h_ref = rest[-1]
    j = pl.program_id(1)

    @pl.when(j == 0)
    def _():
        x = x_ref[...]
        h_ref[...] = _rms(x, g_ref[...]).astype(bf16)
        y_ref[...] = x

    u = jnp.maximum(_dot(h_ref[...], wu_ref[...].astype(bf16)), 0.0)
    y_ref[...] += _dot((u * u).astype(bf16), wd_ref[...].astype(bf16))

    if split_row is not None:
        @pl.when(j == pl.num_programs(1) - 1)
        def _():
            y_ref[...] = _rms(y_ref[...], gf_ref[...])

        @pl.when(jnp.logical_and(j == pl.num_programs(1) - 1, pl.program_id(0) == pl.num_programs(0) - 1))
        def _():
            rest[0][...] = y_ref[split_row:, :]


def _mlp(x, g, wu, wd, layer, gf, split, tm, tf):
    M, D = x.shape
    F = wu.shape[2]
    if split is None:
        out_specs = pl.BlockSpec((tm, D), lambda i, j: (i, 0))
        out_shape = jax.ShapeDtypeStruct((M, D), f32)
        split_row = None
    else:
        assert split // tm == M // tm - 1, "the second group must lie inside the last row block"
        out_specs = [pl.BlockSpec((tm, D), lambda i, j: (i, 0)), pl.BlockSpec((M - split, D), lambda i, j: (0, 0))]
        out_shape = [jax.ShapeDtypeStruct((split, D), f32), jax.ShapeDtypeStruct((M - split, D), f32)]
        split_row = split % tm
    return pl.pallas_call(
        functools.partial(_mlp_body, split_row=split_row),
        grid=(M // tm, F // tf),
        in_specs=[pl.BlockSpec((tm, D), lambda i, j: (i, 0), pipeline_mode=pl.Buffered(1)),
                  pl.BlockSpec((1, D), lambda i, j: (0, 0)),
                  pl.BlockSpec((None, D, tf), lambda i, j: (layer, 0, j)),
                  pl.BlockSpec((None, tf, D), lambda i, j: (layer, j, 0)),
                  pl.BlockSpec((1, D), lambda i, j: (0, 0))],
        out_specs=out_specs,
        out_shape=out_shape,
        scratch_shapes=[pltpu.VMEM((tm, D), bf16)],
        compiler_params=_params(2),
        name="mlp",
    )(x, g.reshape(1, D), wu, wd, gf.reshape(1, D))


def _largest_divisor(n, cap, mult):
    best = None
    for d in range(mult, min(n, cap) + 1, mult):
        if n % d == 0:
            best = d
    assert best is not None, (n, cap, mult)
    return best


def kernel(x_prompt, x_sample, cache_fox_k, cache_fox_v, cache_fox_logf, cache_sb_k, cache_sb_v, state_gla,
           cache_mem_k, cache_mem_v, mem_prompt, norm_mix_g, norm_mlp_g, norm_mem_g, norm_final_g, w_mem_kv,
           w_in_fox, b_forget, w_out_fox, w_in_sb, w_out_sb, w_in_gla, w_gate2_gla, b_gate_gla, b_outgate_gla,
           norm_gla_g, w_out_gla, w_up, w_down):
    Bp, Tp0, D = x_prompt.shape
    Bs, Ts, _ = x_sample.shape
    assert Bp == 1, "the prompt group is handled as one sequence"
    Tp = Bp * Tp0
    M = Tp + Bs * Ts
    depth = norm_mix_g.shape[0]
    H, hd = cache_fox_k.shape[-2:]
    aw = H * hd
    past = cache_fox_k.shape[2]
    Mm, MH = cache_mem_k.shape[2], cache_mem_k.shape[3]
    mw = MH * hd
    GH, dk, dv = state_gla.shape[-3:]
    dkp = -(-dk // LANES) * LANES
    rank = w_gate2_gla.shape[1]
    C = GLA_CHUNK
    assert Ts == C and Tp % (C * GLA_STEP_CHUNKS) == 0 and Bs % GLA_STEP_CHUNKS == 0 and M % 16 == 0

    tm = _largest_divisor(M, 512, 16)
    tm_st = _largest_divisor(math.gcd(Tp, M - Tp), 512, Ts)
    tm_mlp = _largest_divisor(M, 1152, 16)
    tf = _largest_divisor(w_up.shape[2], 512, LANES)
    ta = _largest_divisor(Tp, 512, 2 * LANES)
    tg = _largest_divisor(Tp, 512, LANES)
    tq_mem = _largest_divisor(Tp, 512, 16)
    tp_sb = _largest_divisor(past // 2, 256, LANES)
    att_scale = hd ** -0.5
    att_scale2 = att_scale * LOG2E

    x = jnp.concatenate([x_prompt.reshape(Tp, D), x_sample.reshape(Bs * Ts, D)], axis=0)
    mk_p, mv_p = _memkv(mem_prompt.reshape(Mm, D), norm_mem_g, w_mem_kv)

    w16_fox = w_in_fox.astype(bf16)
    n_fox, n_sb = w_in_fox.shape[0], w_in_sb.shape[0]
    kv_state = {0: None, 1: None}
    logf_p, logf_s, gla_p, gla_s = [], [], [], []
    for i in range(depth):
        kind, j = i % 3, i // 3
        g = norm_mix_g[i]
        if kind in (0, 1):
            w = (w_in_fox if kind == 0 else w_in_sb)[j]
            w_all = w16_fox if kind == 0 else w_in_sb
            n_gate = H if kind == 0 else 0
            (q,) = _proj(x, g, w_all, [(0, aw, bf16, att_scale2)], tm, w_block=(j, 0, aw))
            n_layers = n_fox if kind == 0 else n_sb
            st = kv_state[kind]
            k, *st_k = _proj_state(x, g, w_all, (j, 1, aw), j, st[0] if st else n_layers, Tp, Ts, tm_st, H, hd)
            v, *st_v = _proj_state(x, g, w_all, (j, 2, aw), j, st[1] if st else n_layers, Tp, Ts, tm_st, H, hd)
            kv_state[kind] = (st_k, st_v)
            (qm,) = _proj(x, g, w[:, 3 * aw + n_gate:].astype(bf16), [(0, mw, bf16, att_scale)], tm)
            if kind == 0:
                wf_t = jnp.pad(w[:, 3 * aw:3 * aw + H].T, ((0, FOX_GATE_ROWS - H), (0, 0))).astype(bf16)
                bf_col = jnp.pad(b_forget[j], (0, FOX_GATE_ROWS - H)).reshape(FOX_GATE_ROWS, 1)
                lf_p, cf_p = _fox_gate_prompt(x, g, wf_t, bf_col, Tp, tg)
                plogf_t = jnp.pad(jnp.swapaxes(cache_fox_logf[j], 1, 2), ((0, 0), (0, FOX_GATE_ROWS - H), (0, 0)))
                lf_s, cf_s = _fox_gate_sample(x, g, wf_t, bf_col, plogf_t, Tp, Ts)
                o = _fox_prompt(q, k, v, cf_p, Tp, H, hd, ta)
                o = _fox_sample(q, k, v, jnp.swapaxes(cache_fox_k, 2, 3), jnp.swapaxes(cache_fox_v, 2, 3), j,
                                cf_s, o, Tp, Ts)
                logf_p.append(lf_p[:H].T.reshape(Bp, Tp0, H))
                logf_s.append(jnp.swapaxes(lf_s[:, :H], 1, 2))
                w_out = w_out_fox[j]
            else:
                o = _sb_prompt(q, k, v, Tp, H, hd, ta)
                o = _sb_sample(q, k, v, jnp.swapaxes(cache_sb_k, 2, 3), jnp.swapaxes(cache_sb_v, 2, 3), j,
                               o, Tp, Ts, tp_sb)
                w_out = w_out_sb[j]
        else:
            w = w_in_gla[j]
            kw, vw = GH * dk, GH * dv

            def pad_heads(a):
                a = a.reshape(a.shape[:-1] + (GH, dk))
                a = jnp.pad(a, [(0, 0)] * (a.ndim - 1) + [(0, dkp - dk)])
                return a.reshape(a.shape[:-2] + (GH * dkp,))

            wqk = jnp.concatenate([pad_heads(w[:, :kw]), pad_heads(w[:, kw:2 * kw])], axis=1).astype(bf16)
            wv = w[:, 2 * kw:2 * kw + vw].astype(bf16)
            wr = w[:, 2 * kw + vw:2 * kw + 2 * vw].astype(bf16)
            c0 = 2 * kw + 2 * vw
            wmg = jnp.concatenate([w[:, c0 + rank:], jnp.pad(w[:, c0:c0 + rank], ((0, 0), (0, LANES - rank)))],
                                  axis=1).astype(bf16)
            q, k = _proj(x, g, wqk, [(0, GH * dkp, f32, dk ** -0.5), (GH * dkp, 2 * GH * dkp, f32, 1.0)], tm)
            (v,) = _proj(x, g, wv, [(0, vw, f32, 1.0)], tm)
            (r,) = _proj(x, g, wr, [(0, vw, f32, 1.0)], tm)
            qm, glow = _proj(x, g, wmg, [(0, mw, bf16, att_scale), (mw, mw + LANES, f32, 1.0)], tm)
            w2 = jnp.pad(pad_heads(w_gate2_gla[j]), ((0, LANES - rank), (0, 0))).astype(bf16)
            bg = pad_heads(b_gate_gla[j]).reshape(1, GH * dkp)
            o, st_p, st_s = _gla(q, k, v, r, glow, w2, bg, b_outgate_gla[j].reshape(1, vw),
                                 norm_gla_g[j].reshape(1, vw), state_gla[j], Tp, GH, dk, dkp, dv)
            gla_p.append(st_p[None])
            gla_s.append(st_s)
            w_out = w_out_gla[j]

        om = _mem_attn_prompt(qm, mk_p[i], mv_p[i], Tp, tq_mem, MH, hd)
        om = _mem_attn_sample(qm, cache_mem_k, cache_mem_v, i, om, Tp, Ts)
        x = _out_proj(o, om, x, w_out.astype(bf16), tm)
        x = _mlp(x, norm_mlp_g[i], w_up, w_down, i, norm_final_g, Tp if i == depth - 1 else None, tm_mlp, tf)

    def prompt_state(a):
        return jnp.swapaxes(a.reshape((a.shape[0], Bp, H, Tp0, hd)), 2, 3)

    def sample_state(a):
        return jnp.swapaxes(a, 2, 3)

    (fox_k, fox_v), (sb_k, sb_v) = kv_state[0], kv_state[1]
    y_prompt, y_sample = x
    return (y_prompt.reshape(Bp, Tp0, D), y_sample.reshape(Bs, Ts, D),
            prompt_state(fox_k[0]), prompt_state(fox_v[0]), jnp.stack(logf_p),
            prompt_state(sb_k[0]), prompt_state(sb_v[0]),
            jnp.stack(gla_p),
            mk_p.reshape(depth, Bp, Mm, MH, hd), mv_p.reshape(depth, Bp, Mm, MH, hd),
            sample_state(fox_k[1]), sample_state(fox_v[1]), jnp.stack(logf_s),
            sample_state(sb_k[1]), sample_state(sb_v[1]),
            jnp.stack(gla_s))
```

```python
import functools
import math

import jax
import jax.numpy as jnp
from jax import lax
from jax.experimental import pallas as pl
from jax.experimental.pallas import tpu as pltpu

f32 = jnp.float32
bf16 = jnp.bfloat16

RMS_EPS = 1e-6
GLA_GATE_TEMP = 16.0
GLA_CHUNK = 64
GLA_SUB = 8
GLA_STEP_CHUNKS = 2
LOG2E = 1.4426950408889634
MASKED_LOGIT = -1e30
SKIP_BITS = 160.0
LANES = 128
FOX_GATE_ROWS = 16
VMEM_LIMIT_BYTES = 56 * 1024 * 1024


def _params(n_axes):
    return pltpu.CompilerParams(dimension_semantics=("arbitrary",) * n_axes,
                                vmem_limit_bytes=VMEM_LIMIT_BYTES)


def _rms(x, g):
    return x * lax.rsqrt(jnp.mean(x * x, axis=-1, keepdims=True) + RMS_EPS) * g


def _log_sigmoid(z):
    return jnp.minimum(z, 0.0) - jnp.log1p(jnp.exp(-jnp.abs(z)))


def _split3(x):
    hi = x.astype(bf16)
    r = x - hi.astype(f32)
    mid = r.astype(bf16)
    lo = (r - mid.astype(f32)).astype(bf16)
    return hi, mid, lo


def _split2(x):
    hi = x.astype(bf16)
    lo = (x - hi.astype(f32)).astype(bf16)
    return hi, lo


def _tri(n, m, fn):
    r = lax.broadcasted_iota(jnp.int32, (n, m), 0)
    c = lax.broadcasted_iota(jnp.int32, (n, m), 1)
    return jnp.where(fn(r, c), 1.0, 0.0).astype(bf16)


def _dot(a, b):
    return jnp.dot(a, b, preferred_element_type=f32)


def _dot_nt(a, b):
    return lax.dot_general(a, b, (((1,), (1,)), ((), ())), preferred_element_type=f32)


def _dot_tn(a, b):
    return lax.dot_general(a, b, (((0,), (0,)), ((), ())), preferred_element_type=f32)


def _cumsum_lanes(x, carry, tri):
    n = x.shape[1]
    outs = []
    for b0 in range(0, n, LANES):
        w = min(LANES, n - b0)
        hi, mid, lo = _split3(x[:, b0:b0 + w])
        t = tri[:w, :w]
        c = _dot(hi, t) + _dot(mid, t) + _dot(lo, t) + carry
        carry = c[:, w - 1:w]
        outs.append(c)
    return outs, carry


def _proj_body(x_ref, g_ref, w_ref, *o_refs, cols, scales):
    h = _rms(x_ref[...], g_ref[...]).astype(bf16)
    y = _dot(h, w_ref[...].astype(bf16))
    for o_ref, (c0, c1), sc in zip(o_refs, cols, scales):
        part = y[:, c0:c1]
        if sc != 1.0:
            part = part * sc
        o_ref[...] = part.astype(o_ref.dtype)


def _stacked_weight_spec(D, width, layer, col_block):
    return pl.BlockSpec((None, D, width), lambda i: (layer, 0, col_block), pipeline_mode=pl.Buffered(1))


def _proj(x, g, w, outs, tm, w_block=None):
    M, D = x.shape
    if w_block is None:
        w_spec = pl.BlockSpec(w.shape, lambda i: (0, 0))
    else:
        layer, col_block, width = w_block
        w_spec = _stacked_weight_spec(D, width, layer, col_block)
    cols = tuple((c0, c1) for c0, c1, _, _ in outs)
    scales = tuple(float(s) for _, _, _, s in outs)
    return pl.pallas_call(
        functools.partial(_proj_body, cols=cols, scales=scales),
        grid=(M // tm,),
        in_specs=[pl.BlockSpec((tm, D), lambda i: (i, 0)),
                  pl.BlockSpec((1, D), lambda i: (0, 0)),
                  w_spec],
        out_specs=[pl.BlockSpec((tm, c1 - c0), lambda i: (i, 0)) for c0, c1 in cols],
        out_shape=[jax.ShapeDtypeStruct((M, c1 - c0), dt) for c0, c1, dt, _ in outs],
        compiler_params=_params(1),
        name="norm_proj",
    )(x, g.reshape(1, D), w)


def _proj_state_body(x_ref, g_ref, w_ref, *refs, n_prompt_blocks, n_heads, hd, ts):
    a_ref, sp_ref, ss_ref = refs[-3:]
    i = pl.program_id(0)
    h = _rms(x_ref[...], g_ref[...]).astype(bf16)
    y = _dot(h, w_ref[...].astype(bf16))
    a_ref[...] = y.astype(a_ref.dtype)

    @pl.when(i < n_prompt_blocks)
    def _():
        for hh in range(n_heads):
            sp_ref[hh] = y[:, hh * hd:(hh + 1) * hd]

    @pl.when(i >= n_prompt_blocks)
    def _():
        for b in range(ss_ref.shape[0]):
            for hh in range(n_heads):
                ss_ref[b, hh] = y[b * ts:(b + 1) * ts, hh * hd:(hh + 1) * hd]


def _proj_state(x, g, w, w_block, slot, prev, t_prompt, ts, tm, n_heads, hd):
    M, D = x.shape
    layer, col_block, width = w_block
    n_p = t_prompt // tm
    in_specs = [pl.BlockSpec((tm, D), lambda i: (i, 0)),
                pl.BlockSpec((1, D), lambda i: (0, 0)),
                _stacked_weight_spec(D, width, layer, col_block)]
    args = [x, g.reshape(1, D), w]
    if isinstance(prev, int):
        shapes = [jax.ShapeDtypeStruct((prev, n_heads, t_prompt, hd), f32),
                  jax.ShapeDtypeStruct((prev, (M - t_prompt) // ts, n_heads, ts, hd), f32)]
        aliases = {}
    else:
        shapes = [jax.ShapeDtypeStruct(p.shape, p.dtype) for p in prev]
        in_specs += [pl.BlockSpec(memory_space=pl.ANY)] * 2
        args += list(prev)
        aliases = {3: 1, 4: 2}
    return pl.pallas_call(
        functools.partial(_proj_state_body, n_prompt_blocks=n_p, n_heads=n_heads, hd=hd, ts=ts),
        grid=(M // tm,),
        in_specs=in_specs,
        out_specs=[pl.BlockSpec((tm, width), lambda i: (i, 0)),
                   pl.BlockSpec((None, n_heads, tm, hd), lambda i: (slot, 0, jnp.minimum(i, n_p - 1), 0)),
                   pl.BlockSpec((None, tm // ts, n_heads, ts, hd),
                                lambda i: (slot, jnp.maximum(i - n_p, 0), 0, 0, 0))],
        out_shape=[jax.ShapeDtypeStruct((M, width), bf16)] + shapes,
        input_output_aliases=aliases,
        compiler_params=_params(1),
        name="norm_proj_state",
    )(*args)


def _memkv_body(m_ref, g_ref, w_ref, k_ref, v_ref, *, mw):
    h = _rms(m_ref[...], g_ref[...]).astype(bf16)
    w = w_ref[...].astype(bf16)
    k_ref[...] = _dot(h, w[:, :mw])
    v_ref[...] = _dot(h, w[:, mw:])


def _memkv(mem, g, w):
    L, D, two_mw = w.shape
    Mm = mem.shape[0]
    mw = two_mw // 2
    return pl.pallas_call(
        functools.partial(_memkv_body, mw=mw),
        grid=(L,),
        in_specs=[pl.BlockSpec((Mm, D), lambda l: (0, 0)),
                  pl.BlockSpec((None, 1, D), lambda l: (l, 0, 0)),
                  pl.BlockSpec((None, D, two_mw), lambda l: (l, 0, 0))],
        out_specs=[pl.BlockSpec((None, Mm, mw), lambda l: (l, 0, 0))] * 2,
        out_shape=[jax.ShapeDtypeStruct((L, Mm, mw), f32)] * 2,
        compiler_params=_params(1),
        name="mem_kv",
    )(mem, g.reshape(L, 1, D), w)


def _gate_logits(x_ref, g_ref, wf_ref, bf_ref):
    h = _rms(x_ref[...], g_ref[...]).astype(bf16)
    return _log_sigmoid(_dot_nt(wf_ref[...], h) + bf_ref[...])


def _fox_gate_prompt_body(x_ref, g_ref, wf_ref, bf_ref, lf_ref, cf_ref, carry_ref):
    @pl.when(pl.program_id(0) == 0)
    def _():
        carry_ref[...] = jnp.zeros_like(carry_ref)

    logf = _gate_logits(x_ref, g_ref, wf_ref, bf_ref)
    lf_ref[...] = logf
    tri = _tri(LANES, LANES, lambda r, c: r <= c)
    outs, carry = _cumsum_lanes(logf, carry_ref[:, 0:1], tri)
    for b, c in enumerate(outs):
        cf_ref[:, b * LANES:(b + 1) * LANES] = c * LOG2E
    carry_ref[...] = jnp.broadcast_to(carry, carry_ref.shape)


def _fox_gate_prompt(x, g, wf_t, bf, t_prompt, tg):
    D = x.shape[1]
    R = wf_t.shape[0]
    return pl.pallas_call(
        _fox_gate_prompt_body,
        grid=(t_prompt // tg,),
        in_specs=[pl.BlockSpec((tg, D), lambda i: (i, 0)),
                  pl.BlockSpec((1, D), lambda i: (0, 0)),
                  pl.BlockSpec((R, D), lambda i: (0, 0)),
                  pl.BlockSpec((R, 1), lambda i: (0, 0))],
        out_specs=[pl.BlockSpec((R, tg), lambda i: (0, i))] * 2,
        out_shape=[jax.ShapeDtypeStruct((R, t_prompt), f32)] * 2,
        scratch_shapes=[pltpu.VMEM((R, LANES), f32)],
        compiler_params=_params(1),
        name="fox_gate_prompt",
    )(x, g.reshape(1, D), wf_t, bf)


def _fox_gate_sample_body(x_ref, g_ref, wf_ref, bf_ref, pl_ref, lf_ref, cf_ref, *, past, ts):
    logf = _gate_logits(x_ref, g_ref, wf_ref, bf_ref)
    lf_ref[...] = logf
    tri = _tri(LANES, LANES, lambda r, c: r <= c)
    zero = jnp.zeros((logf.shape[0], 1), f32)
    outs, carry = _cumsum_lanes(pl_ref[...], zero, tri)
    for b, c in enumerate(outs):
        cf_ref[:, b * LANES:(b + 1) * LANES] = c * LOG2E
    (new,), _ = _cumsum_lanes(logf, carry, tri)
    cf_ref[:, past:past + ts] = new * LOG2E
    cf_ref[:, past + ts:] = jnp.zeros((logf.shape[0], LANES - ts), f32)


def _fox_gate_sample(x, g, wf_t, bf, plogf_t, row0, ts):
    D = x.shape[1]
    R = wf_t.shape[0]
    B, _, past = plogf_t.shape
    blk0 = row0 // ts
    return pl.pallas_call(
        functools.partial(_fox_gate_sample_body, past=past, ts=ts),
        grid=(B,),
        in_specs=[pl.BlockSpec((ts, D), lambda b: (blk0 + b, 0)),
                  pl.BlockSpec((1, D), lambda b: (0, 0)),
                  pl.BlockSpec((R, D), lambda b: (0, 0)),
                  pl.BlockSpec((R, 1), lambda b: (0, 0)),
                  pl.BlockSpec((None, R, past), lambda b: (b, 0, 0))],
        out_specs=[pl.BlockSpec((None, R, ts), lambda b: (b, 0, 0)),
                   pl.BlockSpec((None, R, past + LANES), lambda b: (b, 0, 0))],
        out_shape=[jax.ShapeDtypeStruct((B, R, ts), f32),
                   jax.ShapeDtypeStruct((B, R, past + LANES), f32)],
        compiler_params=_params(1),
        name="fox_gate_sample",
    )(x, g.reshape(1, D), wf_t, bf, plogf_t)


def _softmax_tile_wide(s, m, acc, v):
    m_new = jnp.maximum(m, jnp.max(s, axis=-1, keepdims=True))
    p = jnp.exp2((s - m_new).astype(bf16))
    return m_new, jnp.exp2(m - m_new) * acc + _dot(p, v)


def _causal(n, strict):
    r = lax.broadcasted_iota(jnp.int32, (n, n), 0)
    c = lax.broadcasted_iota(jnp.int32, (n, n), 1)
    return c < r if strict else c <= r


def _fox_prompt_body(q_ref, k16, v16, f_ref, o_ref, knorm_ref, *, t):
    qi = pl.program_id(1)

    @pl.when(qi == 0)
    def _():
        kf = k16[...].astype(f32)
        k_sq = jnp.max(jnp.sum(kf * kf, axis=-1, keepdims=True), axis=0, keepdims=True)
        knorm_ref[...] = jnp.broadcast_to(jnp.sqrt(k_sq), knorm_ref.shape)

    q = q_ref[...]
    qf = q.astype(f32)
    qk_bound = jnp.sqrt(jnp.sum(qf * qf, axis=-1, keepdims=True)) * knorm_ref[0:1, 0:1]
    col_minus_row = (lax.broadcasted_iota(jnp.int32, (t, t), 1) - lax.broadcasted_iota(jnp.int32, (t, t), 0))

    def scores(kb, max_col_minus_row):
        rows = pl.ds(pl.multiple_of(kb * t, t), t)
        s = _dot_nt(q, k16[rows, :]) - f_ref[pl.ds(kb, 1), :]
        if max_col_minus_row is not None:
            s = jnp.where(col_minus_row <= max_col_minus_row, s, MASKED_LOGIT)
        return s, jnp.concatenate([v16[rows, :], ones], axis=1)

    def tile(kb, carry, max_col_minus_row):
        s, v = scores(kb, max_col_minus_row)
        return _softmax_tile_wide(s, *carry, v)

    def rest_is_zero(a, b, kb):
        bias_max = jnp.max(-f_ref[pl.ds(kb, 1), :])
        return (jnp.max(qk_bound + bias_max - jnp.maximum(a[0], b[0])) < -SKIP_BITS).astype(jnp.int32)

    hd = q.shape[1]
    ones = jnp.ones((t, hd), bf16)
    init = (jnp.full((t, 1), MASKED_LOGIT, f32), jnp.zeros((t, 2 * hd), f32))
    odd = qi % 2
    a = tile(qi, init, 0)
    b = lax.cond(odd == 1, lambda: tile(qi - 1, init, None), lambda: init)
    first = qi - 1 - odd
    n_pairs = qi // 2

    def pair(c):
        i, a, b, _ = c
        kb = first - 2 * i
        s_a, v_a = scores(kb, None)
        s_b, v_b = scores(kb - 1, None)
        a = _softmax_tile_wide(s_a, *a, v_a)
        b = _softmax_tile_wide(s_b, *b, v_b)
        return i + 1, a, b, rest_is_zero(a, b, jnp.maximum(kb - 2, 0))

    _, (m_a, acc_a), (m_b, acc_b), _ = lax.while_loop(
        lambda c: jnp.logical_and(c[0] < n_pairs, c[3] == 0), pair,
        (jnp.int32(0), a, b, rest_is_zero(a, b, jnp.maximum(first, 0))))
    m = jnp.maximum(m_a, m_b)
    acc = jnp.exp2(m_a - m) * acc_a + jnp.exp2(m_b - m) * acc_b
    o_ref[...] = (acc[:, :hd] / acc[:, hd:]).astype(o_ref.dtype)


def _fox_prompt(q, k, v, cf, t_prompt, n_heads, hd, t):
    M = q.shape[0]
    nq = t_prompt // t
    return pl.pallas_call(
        functools.partial(_fox_prompt_body, t=t),
        grid=(n_heads, nq),
        in_specs=[pl.BlockSpec((t, hd), lambda h, i: (i, h)),
                  pl.BlockSpec((t_prompt, hd), lambda h, i: (0, h)),
                  pl.BlockSpec((t_prompt, hd), lambda h, i: (0, h)),
                  pl.BlockSpec((None, nq, t), lambda h, i: (h, 0, 0))],
        out_specs=pl.BlockSpec((t, hd), lambda h, i: (i, h)),
        out_shape=jax.ShapeDtypeStruct((M, n_heads * hd), bf16),
        scratch_shapes=[pltpu.VMEM((8, LANES), f32)],
        compiler_params=_params(2),
        name="fox_prompt",
    )(q, k, v, cf.reshape(cf.shape[0], nq, t))


def _fox_sample_body(q_ref, k_ref, v_ref, pk_ref, pv_ref, f_ref, _, o_ref, *, past, ts, n_heads, hd):
    for h in range(n_heads):
        cols = slice(h * hd, (h + 1) * hd)
        q = q_ref[:, cols]
        f = f_ref[h:h + 1, :]
        s_past = _dot_nt(q, pk_ref[h].astype(bf16)) - f[:, :past]
        s_new = _dot_nt(q, k_ref[:, cols].astype(bf16)) - f[:, past:past + ts]
        s_new = jnp.where(_causal(ts, strict=False), s_new, -jnp.inf)
        m = jnp.maximum(jnp.max(s_past, axis=-1, keepdims=True), jnp.max(s_new, axis=-1, keepdims=True))
        p_past = jnp.exp2(s_past - m)
        p_new = jnp.exp2(s_new - m)
        l = jnp.sum(p_past, axis=-1, keepdims=True) + jnp.sum(p_new, axis=-1, keepdims=True)
        acc = (_dot(p_past.astype(bf16), pv_ref[h].astype(bf16))
               + _dot(p_new.astype(bf16), v_ref[:, cols].astype(bf16)))
        o_ref[:, cols] = (acc / l).astype(o_ref.dtype)


def _sample_attn_call(body, name, q, k, v, pk, pv, layer, extra, o_buf, row0, ts):
    _, B, n_heads, past, hd = pk.shape
    blk0 = row0 // ts
    new_spec = pl.BlockSpec((ts, n_heads * hd), lambda b: (blk0 + b, 0))
    past_spec = pl.BlockSpec((None, None, n_heads, past, hd), lambda b: (layer, b, 0, 0, 0))
    in_specs = [new_spec, new_spec, new_spec, past_spec, past_spec]
    args = [q, k, v, pk, pv]
    if extra is not None:
        in_specs.append(pl.BlockSpec((None,) + extra.shape[1:], lambda b: (b, 0, 0)))
        args.append(extra)
    in_specs.append(pl.BlockSpec(memory_space=pl.ANY))
    args.append(o_buf)
    return pl.pallas_call(
        functools.partial(body, past=past, ts=ts, n_heads=n_heads, hd=hd),
        grid=(B,),
        in_specs=in_specs,
        out_specs=new_spec,
        out_shape=jax.ShapeDtypeStruct(o_buf.shape, o_buf.dtype),
        input_output_aliases={len(args) - 1: 0},
        compiler_params=_params(1),
        name=name,
    )(*args)


def _fox_sample(q, k, v, pk, pv, layer, cf, o_buf, row0, ts):
    return _sample_attn_call(_fox_sample_body, "fox_sample", q, k, v, pk, pv, layer, cf, o_buf, row0, ts)


def _tri2(n):
    tri = _tri(n, n, lambda r, c: r >= c)
    return jnp.concatenate([tri, tri], axis=0)


def _sb_scores(q, k, tri2, valid):
    z = _dot_nt(q, k)
    sp = jnp.maximum(z, 0.0) + jnp.log(1.0 + jnp.exp2(-jnp.abs(z))) * LOG2E
    if valid is not None:
        sp = jnp.where(valid, sp, 0.0)
    cum = _dot(jnp.concatenate(_split2(sp), axis=1), tri2)
    return z - cum, cum[:, 0:1]


def _sb_out(zc, run, v, valid):
    a = jnp.exp2(zc)
    if valid is not None:
        a = jnp.where(valid, a, 0.0)
    return jnp.exp2(-run) * _dot(a.astype(bf16), v)


def _sb_pair(q, right, left, tri2, run, acc, valid_right=None, valid_left=None):
    (k_r, v_r), (k_l, v_l) = right, left
    zc_r, tot_r = _sb_scores(q, k_r, tri2, valid_right)
    zc_l, tot_l = _sb_scores(q, k_l, tri2, valid_left)
    acc = acc + _sb_out(zc_r, run, v_r, valid_right) + _sb_out(zc_l, run + tot_r, v_l, valid_left)
    return run + tot_r + tot_l, acc


def _sb_prompt_body(q_ref, k16, v16, o_ref, tri2_ref, *, tq, tk):
    qi = pl.program_id(1)

    @pl.when(qi == 0)
    def _():
        tri2_ref[...] = _tri2(tk)

    q = q_ref[...]
    tri2 = tri2_ref[...]

    def kv(kb):
        rows = pl.ds(pl.multiple_of(kb * tk, tk), tk)
        return k16[rows, :], v16[rows, :]

    r = lax.broadcasted_iota(jnp.int32, (tq, tk), 0)
    c = lax.broadcasted_iota(jnp.int32, (tq, tk), 1)
    init = (jnp.zeros((tq, 1), f32), jnp.zeros((tq, q.shape[1]), f32))
    carry = _sb_pair(q, kv(2 * qi + 1), kv(2 * qi), tri2, *init, valid_right=c + tk < r, valid_left=c < r)

    def all_rows_spent(run):
        return (jnp.min(run) >= SKIP_BITS).astype(jnp.int32)

    def below(c):
        i, run, acc, _ = c
        k_t, v_t = kv(2 * qi - 1 - i)
        zc, tot = _sb_scores(q, k_t, tri2, None)
        acc = acc + _sb_out(zc, run, v_t, None)
        return i + 1, run + tot, acc, all_rows_spent(run + tot)

    _, _, acc, _ = lax.while_loop(lambda c: jnp.logical_and(c[0] < 2 * qi, c[3] == 0), below,
                                  (jnp.int32(0), *carry, all_rows_spent(carry[0])))
    o_ref[...] = acc.astype(o_ref.dtype)


def _sb_prompt(q, k, v, t_prompt, n_heads, hd, tq):
    M = q.shape[0]
    tk = tq // 2
    return pl.pallas_call(
        functools.partial(_sb_prompt_body, tq=tq, tk=tk),
        grid=(n_heads, t_prompt // tq),
        in_specs=[pl.BlockSpec((tq, hd), lambda h, i: (i, h)),
                  pl.BlockSpec((t_prompt, hd), lambda h, i: (0, h)),
                  pl.BlockSpec((t_prompt, hd), lambda h, i: (0, h))],
        out_specs=pl.BlockSpec((tq, hd), lambda h, i: (i, h)),
        out_shape=jax.ShapeDtypeStruct((M, n_heads * hd), bf16),
        scratch_shapes=[pltpu.VMEM((2 * tk, tk), bf16)],
        compiler_params=_params(2),
        name="sb_prompt",
    )(q, k, v)


def _sb_sample_body(q_ref, k_ref, v_ref, pk_ref, pv_ref, _, o_ref, *, past, ts, n_heads, hd, tp):
    strict = _causal(ts, strict=True)
    cols = [slice(h * hd, (h + 1) * hd) for h in range(n_heads)]
    qs = [q_ref[:, c] for c in cols]
    runs = [jnp.zeros((ts, 1), f32)] * n_heads
    accs = [jnp.zeros((ts, hd), f32)] * n_heads

    def tile(keys, values, tri2, valid):
        zs = [_dot_nt(q, k) for q, k in zip(qs, keys)]
        sps = []
        for z in zs:
            sp = jnp.maximum(z, 0.0) + jnp.log(1.0 + jnp.exp2(-jnp.abs(z))) * LOG2E
            sps.append(sp if valid is None else jnp.where(valid, sp, 0.0))
        cum = _dot(jnp.concatenate(_split2(jnp.concatenate(sps, axis=0)), axis=1), tri2)
        for h in range(n_heads):
            cum_h = cum[h * ts:(h + 1) * ts]
            accs[h] = accs[h] + _sb_out(zs[h] - cum_h, runs[h], values[h], valid)
            runs[h] = runs[h] + cum_h[:, 0:1]

    tile([k_ref[:, c].astype(bf16) for c in cols], [v_ref[:, c].astype(bf16) for c in cols], _tri2(ts), strict)
    tri2 = _tri2(tp)
    for p0 in range(past - tp, -1, -tp):
        tile([pk_ref[h, p0:p0 + tp, :].astype(bf16) for h in range(n_heads)],
             [pv_ref[h, p0:p0 + tp, :].astype(bf16) for h in range(n_heads)], tri2, None)
    for c, acc in zip(cols, accs):
        o_ref[:, c] = acc.astype(o_ref.dtype)


def _sb_sample(q, k, v, pk, pv, layer, o_buf, row0, ts, tp):
    return _sample_attn_call(functools.partial(_sb_sample_body, tp=tp), "sb_sample",
                             q, k, v, pk, pv, layer, None, o_buf, row0, ts)


def _mem_attn_body(q_ref, k_ref, v_ref, *rest, n_heads, hd):
    o_ref = rest[-1]
    for h in range(n_heads):
        sl = slice(h * hd, (h + 1) * hd)
        head = (lambda ref: ref[:, h, :]) if len(k_ref.shape) == 3 else (lambda ref: ref[:, sl])
        s = _dot_nt(q_ref[:, sl], head(k_ref).astype(bf16))
        p = jnp.exp(s - jnp.max(s, axis=-1, keepdims=True))
        acc = _dot(p.astype(bf16), head(v_ref).astype(bf16))
        o_ref[:, sl] = (acc / jnp.sum(p, axis=-1, keepdims=True)).astype(o_ref.dtype)


def _mem_attn_prompt(qm, mk, mv, t_prompt, tq, n_heads, hd):
    M, W = qm.shape
    Mm = mk.shape[0]
    return pl.pallas_call(
        functools.partial(_mem_attn_body, n_heads=n_heads, hd=hd),
        grid=(t_prompt // tq,),
        in_specs=[pl.BlockSpec((tq, W), lambda i: (i, 0)),
                  pl.BlockSpec((Mm, W), lambda i: (0, 0)),
                  pl.BlockSpec((Mm, W), lambda i: (0, 0))],
        out_specs=pl.BlockSpec((tq, W), lambda i: (i, 0)),
        out_shape=jax.ShapeDtypeStruct((M, W), bf16),
        compiler_params=_params(1),
        name="mem_attn_prompt",
    )(qm, mk, mv)


def _mem_attn_sample(qm, mk, mv, layer, o_buf, row0, ts):
    M, W = qm.shape
    _, B, Mm, n_heads, hd = mk.shape
    blk0 = row0 // ts
    row_spec = pl.BlockSpec((ts, W), lambda b: (blk0 + b, 0))
    mem_spec = pl.BlockSpec((None, None, Mm, n_heads, hd), lambda b: (layer, b, 0, 0, 0))
    return pl.pallas_call(
        functools.partial(_mem_attn_body, n_heads=n_heads, hd=hd),
        grid=(B,),
        in_specs=[row_spec, mem_spec, mem_spec, pl.BlockSpec(memory_space=pl.ANY)],
        out_specs=row_spec,
        out_shape=jax.ShapeDtypeStruct((M, W), bf16),
        input_output_aliases={3: 0},
        compiler_params=_params(1),
        name="mem_attn_sample",
    )(qm, mk, mv, o_buf)


def _gla_chunk(rows, h, log_a, tril, q_ref, k_ref, v_ref, r_ref, br_ref, ng_ref, o_ref, S_h, *, dkp, dv):
    C, U = GLA_CHUNK, GLA_SUB
    ks = slice(h * dkp, (h + 1) * dkp)
    vs = slice(h * dv, (h + 1) * dv)
    row_in_sub = lax.broadcasted_iota(jnp.int32, (U, 1), 0)
    hi, mid, lo = _split3(log_a[rows, ks])
    G = _dot(tril, hi) + _dot(tril, mid) + _dot(tril, lo)
    q = q_ref[rows, ks]
    k = k_ref[rows, ks]
    v = v_ref[rows, vs]
    v16 = v.astype(bf16)
    o_inter = _dot((q * jnp.exp(G)).astype(bf16), S_h.astype(bf16))

    o_rows = []
    for b in range(C // U):
        r0 = b * U
        Gb, qb = G[r0:r0 + U], q[r0:r0 + U]
        o_b = o_inter[r0:r0 + U]
        if b > 0:
            ref = G[r0 - 1:r0]
            qg = qb * jnp.exp(Gb - ref)
            kg = k[:r0] * jnp.exp(ref - G[:r0])
            a = _dot_nt(qg.astype(bf16), kg.astype(bf16))
            o_b = o_b + _dot(a.astype(bf16), v16[:r0])
        for j in range(U):
            s = r0 + j
            e = jnp.exp(jnp.minimum(Gb - G[s:s + 1], 0.0))
            a = jnp.sum(qb * e * k[s:s + 1], axis=-1, keepdims=True)
            a = jnp.where(row_in_sub >= j, a, 0.0)
            o_b = o_b + a * v[s:s + 1]
        o_rows.append(o_b)
    o = jnp.concatenate(o_rows, axis=0)

    on = o * lax.rsqrt(jnp.mean(o * o, axis=-1, keepdims=True) + RMS_EPS) * ng_ref[:, vs]
    x = r_ref[rows, vs] + br_ref[:, vs]
    o_ref[rows, vs] = (on * (x / (1.0 + jnp.exp(-x)))).astype(o_ref.dtype)

    kl = k * jnp.exp(G[C - 1:C] - G)
    decay_col = jnp.exp(G.T[:, C - 1:C])
    return decay_col * S_h + _dot_tn(kl.astype(bf16), v16)


def _gla_body(q_ref, k_ref, v_ref, r_ref, gl_ref, w2_ref, bg_ref, br_ref, ng_ref, s0_ref,
              o_ref, sp_ref, ss_ref, S, *, n_prompt_steps, n_heads, dk, dkp, dv):
    c = pl.program_id(0)
    C = GLA_CHUNK
    is_sample = c >= n_prompt_steps

    @pl.when(c == 0)
    def _():
        S[...] = jnp.zeros_like(S)

    log_a = _log_sigmoid(_dot(gl_ref[...].astype(bf16), w2_ref[...]) + bg_ref[...]) * (1.0 / GLA_GATE_TEMP)
    tril = _tri(C, C, lambda r, cc: r >= cc)
    refs = (q_ref, k_ref, v_ref, r_ref, br_ref, ng_ref, o_ref)

    for h in range(n_heads):
        S_h = S_prompt = S[h]
        for ci in range(GLA_STEP_CHUNKS):
            s0 = jnp.concatenate([s0_ref[ci, h], jnp.zeros((dkp - dk, dv), f32)], axis=0)
            S_h = _gla_chunk(slice(ci * C, (ci + 1) * C), h, log_a, tril, *refs,
                             jnp.where(is_sample, s0, S_h), dkp=dkp, dv=dv)
            ss_ref[ci, h] = S_h[:dk]
        S_prompt = jnp.where(is_sample, S_prompt, S_h)
        S[h] = S_prompt
        sp_ref[h] = S_prompt[:dk]


def _gla(q, k, v, r, glow, w2, bg, br, ng, s0, t_prompt, n_heads, dk, dkp, dv):
    M = q.shape[0]
    rows = GLA_CHUNK * GLA_STEP_CHUNKS
    B = s0.shape[0]
    n_p = t_prompt // rows
    row = lambda width: pl.BlockSpec((rows, width), lambda c: (c, 0))
    const = lambda a: pl.BlockSpec(a.shape, lambda c: (0,) * a.ndim)
    sample_state = pl.BlockSpec((GLA_STEP_CHUNKS, n_heads, dk, dv), lambda c: (jnp.maximum(c - n_p, 0), 0, 0, 0))
    return pl.pallas_call(
        functools.partial(_gla_body, n_prompt_steps=n_p, n_heads=n_heads, dk=dk, dkp=dkp, dv=dv),
        grid=(M // rows,),
        in_specs=[row(n_heads * dkp), row(n_heads * dkp), row(n_heads * dv), row(n_heads * dv),
                  row(glow.shape[1]), const(w2), const(bg), const(br), const(ng), sample_state],
        out_specs=[row(n_heads * dv), pl.BlockSpec((n_heads, dk, dv), lambda c: (0, 0, 0)), sample_state],
        out_shape=[jax.ShapeDtypeStruct((M, n_heads * dv), bf16),
                   jax.ShapeDtypeStruct((n_heads, dk, dv), f32),
                   jax.ShapeDtypeStruct((B, n_heads, dk, dv), f32)],
        scratch_shapes=[pltpu.VMEM((n_heads, dkp, dv), f32)],
        compiler_params=_params(1),
        name="gla",
    )(q, k, v, r, glow, w2, bg, br, ng, s0)


def _out_proj_body(o_ref, om_ref, x_ref, w_ref, y_ref, *, wo):
    y_ref[...] = x_ref[...] + _dot(o_ref[...], w_ref[:wo, :]) + _dot(om_ref[...], w_ref[wo:, :])


def _out_proj(o, om, x, w, tm):
    M, D = x.shape
    wo, wm = o.shape[1], om.shape[1]
    return pl.pallas_call(
        functools.partial(_out_proj_body, wo=wo),
        grid=(M // tm,),
        in_specs=[pl.BlockSpec((tm, wo), lambda i: (i, 0)),
                  pl.BlockSpec((tm, wm), lambda i: (i, 0)),
                  pl.BlockSpec((tm, D), lambda i: (i, 0)),
                  pl.BlockSpec((wo + wm, D), lambda i: (0, 0))],
        out_specs=pl.BlockSpec((tm, D), lambda i: (i, 0)),
        out_shape=jax.ShapeDtypeStruct((M, D), f32),
        compiler_params=_params(1),
        name="out_proj",
    )(o, om, x, w)


def _mlp_body(x_ref, g_ref, wu_ref, wd_ref, gf_ref, y_ref, *rest, split_row):
    h_ref = rest[-1]
    j = pl.program_id(1)

    @pl.when(j == 0)
    def _():
        x = x_ref[...]
        h_ref[...] = _rms(x, g_ref[...]).astype(bf16)
        y_ref[...] = x

    u = jnp.maximum(_dot(h_ref[...], wu_ref[...].astype(bf16)), 0.0)
    y_ref[...] += _dot((u * u).astype(bf16), wd_ref[...].astype(bf16))

    if split_row is not None:
        @pl.when(j == pl.num_programs(1) - 1)
        def _():
            y_ref[...] = _rms(y_ref[...], gf_ref[...])

        @pl.when(jnp.logical_and(j == pl.num_programs(1) - 1, pl.program_id(0) == pl.num_programs(0) - 1))
        def _():
            rest[0][...] = y_ref[split_row:, :]


def _mlp(x, g, wu, wd, layer, gf, split, tm, tf):
    M, D = x.shape
    F = wu.shape[2]
    if split is None:
        out_specs = pl.BlockSpec((tm, D), lambda i, j: (i, 0))
        out_shape = jax.ShapeDtypeStruct((M, D), f32)
        split_row = None
    else:
        assert split // tm == M // tm - 1, "the second group must lie inside the last row block"
        out_specs = [pl.BlockSpec((tm, D), lambda i, j: (i, 0)), pl.BlockSpec((M - split, D), lambda i, j: (0, 0))]
        out_shape = [jax.ShapeDtypeStruct((split, D), f32), jax.ShapeDtypeStruct((M - split, D), f32)]
        split_row = split % tm
    return pl.pallas_call(
        functools.partial(_mlp_body, split_row=split_row),
        grid=(M // tm, F // tf),
        in_specs=[pl.BlockSpec((tm, D), lambda i, j: (i, 0), pipeline_mode=pl.Buffered(1)),
                  pl.BlockSpec((1, D), lambda i, j: (0, 0)),
                  pl.BlockSpec((None, D, tf), lambda i, j: (layer, 0, j)),
                  pl.BlockSpec((None, tf, D), lambda i, j: (layer, j, 0)),
                  pl.BlockSpec((1, D), lambda i, j: (0, 0))],
        out_specs=out_specs,
        out_shape=out_shape,
        scratch_shapes=[pltpu.VMEM((tm, D), bf16)],
        compiler_params=_params(2),
        name="mlp",
    )(x, g.reshape(1, D), wu, wd, gf.reshape(1, D))


def _largest_divisor(n, cap, mult):
    best = None
    for d in range(mult, min(n, cap) + 1, mult):
        if n % d == 0:
            best = d
    assert best is not None, (n, cap, mult)
    return best


def kernel(x_prompt, x_sample, cache_fox_k, cache_fox_v, cache_fox_logf, cache_sb_k, cache_sb_v, state_gla,
           cache_mem_k, cache_mem_v, mem_prompt, norm_mix_g, norm_mlp_g, norm_mem_g, norm_final_g, w_mem_kv,
           w_in_fox, b_forget, w_out_fox, w_in_sb, w_out_sb, w_in_gla, w_gate2_gla, b_gate_gla, b_outgate_gla,
           norm_gla_g, w_out_gla, w_up, w_down):
    Bp, Tp0, D = x_prompt.shape
    Bs, Ts, _ = x_sample.shape
    assert Bp == 1, "the prompt group is handled as one sequence"
    Tp = Bp * Tp0
    M = Tp + Bs * Ts
    depth = norm_mix_g.shape[0]
    H, hd = cache_fox_k.shape[-2:]
    aw = H * hd
    past = cache_fox_k.shape[2]
    Mm, MH = cache_mem_k.shape[2], cache_mem_k.shape[3]
    mw = MH * hd
    GH, dk, dv = state_gla.shape[-3:]
    dkp = -(-dk // LANES) * LANES
    rank = w_gate2_gla.shape[1]
    C = GLA_CHUNK
    assert Ts == C and Tp % (C * GLA_STEP_CHUNKS) == 0 and Bs % GLA_STEP_CHUNKS == 0 and M % 16 == 0

    tm = _largest_divisor(M, 512, 16)
    tm_st = _largest_divisor(math.gcd(Tp, M - Tp), 512, Ts)
    tm_mlp = _largest_divisor(M, 1152, 16)
    tf = _largest_divisor(w_up.shape[2], 512, LANES)
    ta = _largest_divisor(Tp, 512, 2 * LANES)
    tg = _largest_divisor(Tp, 512, LANES)
    tq_mem = _largest_divisor(Tp, 512, 16)
    tp_sb = _largest_divisor(past // 2, 256, LANES)
    att_scale = hd ** -0.5
    att_scale2 = att_scale * LOG2E

    x = jnp.concatenate([x_prompt.reshape(Tp, D), x_sample.reshape(Bs * Ts, D)], axis=0)
    mk_p, mv_p = _memkv(mem_prompt.reshape(Mm, D), norm_mem_g, w_mem_kv)

    w16_fox = w_in_fox.astype(bf16)
    n_fox, n_sb = w_in_fox.shape[0], w_in_sb.shape[0]
    kv_state = {0: None, 1: None}
    logf_p, logf_s, gla_p, gla_s = [], [], [], []
    for i in range(depth):
        kind, j = i % 3, i // 3
        g = norm_mix_g[i]
        if kind in (0, 1):
            w = (w_in_fox if kind == 0 else w_in_sb)[j]
            w_all = w16_fox if kind == 0 else w_in_sb
            n_gate = H if kind == 0 else 0
            (q,) = _proj(x, g, w_all, [(0, aw, bf16, att_scale2)], tm, w_block=(j, 0, aw))
            n_layers = n_fox if kind == 0 else n_sb
            st = kv_state[kind]
            k, *st_k = _proj_state(x, g, w_all, (j, 1, aw), j, st[0] if st else n_layers, Tp, Ts, tm_st, H, hd)
            v, *st_v = _proj_state(x, g, w_all, (j, 2, aw), j, st[1] if st else n_layers, Tp, Ts, tm_st, H, hd)
            kv_state[kind] = (st_k, st_v)
            (qm,) = _proj(x, g, w[:, 3 * aw + n_gate:].astype(bf16), [(0, mw, bf16, att_scale)], tm)
            if kind == 0:
                wf_t = jnp.pad(w[:, 3 * aw:3 * aw + H].T, ((0, FOX_GATE_ROWS - H), (0, 0))).astype(bf16)
                bf_col = jnp.pad(b_forget[j], (0, FOX_GATE_ROWS - H)).reshape(FOX_GATE_ROWS, 1)
                lf_p, cf_p = _fox_gate_prompt(x, g, wf_t, bf_col, Tp, tg)
                plogf_t = jnp.pad(jnp.swapaxes(cache_fox_logf[j], 1, 2), ((0, 0), (0, FOX_GATE_ROWS - H), (0, 0)))
                lf_s, cf_s = _fox_gate_sample(x, g, wf_t, bf_col, plogf_t, Tp, Ts)
                o = _fox_prompt(q, k, v, cf_p, Tp, H, hd, ta)
                o = _fox_sample(q, k, v, jnp.swapaxes(cache_fox_k, 2, 3), jnp.swapaxes(cache_fox_v, 2, 3), j,
                                cf_s, o, Tp, Ts)
                logf_p.append(lf_p[:H].T.reshape(Bp, Tp0, H))
                logf_s.append(jnp.swapaxes(lf_s[:, :H], 1, 2))
                w_out = w_out_fox[j]
            else:
                o = _sb_prompt(q, k, v, Tp, H, hd, ta)
                o = _sb_sample(q, k, v, jnp.swapaxes(cache_sb_k, 2, 3), jnp.swapaxes(cache_sb_v, 2, 3), j,
                               o, Tp, Ts, tp_sb)
                w_out = w_out_sb[j]
        else:
            w = w_in_gla[j]
            kw, vw = GH * dk, GH * dv

            def pad_heads(a):
                a = a.reshape(a.shape[:-1] + (GH, dk))
                a = jnp.pad(a, [(0, 0)] * (a.ndim - 1) + [(0, dkp - dk)])
                return a.reshape(a.shape[:-2] + (GH * dkp,))

            wqk = jnp.concatenate([pad_heads(w[:, :kw]), pad_heads(w[:, kw:2 * kw])], axis=1).astype(bf16)
            wv = w[:, 2 * kw:2 * kw + vw].astype(bf16)
            wr = w[:, 2 * kw + vw:2 * kw + 2 * vw].astype(bf16)
            c0 = 2 * kw + 2 * vw
            wmg = jnp.concatenate([w[:, c0 + rank:], jnp.pad(w[:, c0:c0 + rank], ((0, 0), (0, LANES - rank)))],
                                  axis=1).astype(bf16)
            q, k = _proj(x, g, wqk, [(0, GH * dkp, f32, dk ** -0.5), (GH * dkp, 2 * GH * dkp, f32, 1.0)], tm)
            (v,) = _proj(x, g, wv, [(0, vw, f32, 1.0)], tm)
            (r,) = _proj(x, g, wr, [(0, vw, f32, 1.0)], tm)
            qm, glow = _proj(x, g, wmg, [(0, mw, bf16, att_scale), (mw, mw + LANES, f32, 1.0)], tm)
            w2 = jnp.pad(pad_heads(w_gate2_gla[j]), ((0, LANES - rank), (0, 0))).astype(bf16)
            bg = pad_heads(b_gate_gla[j]).reshape(1, GH * dkp)
            o, st_p, st_s = _gla(q, k, v, r, glow, w2, bg, b_outgate_gla[j].reshape(1, vw),
                                 norm_gla_g[j].reshape(1, vw), state_gla[j], Tp, GH, dk, dkp, dv)
            gla_p.append(st_p[None])
            gla_s.append(st_s)
            w_out = w_out_gla[j]

        om = _mem_attn_prompt(qm, mk_p[i], mv_p[i], Tp, tq_mem, MH, hd)
        om = _mem_attn_sample(qm, cache_mem_k, cache_mem_v, i, om, Tp, Ts)
        x = _out_proj(o, om, x, w_out.astype(bf16), tm)
        x = _mlp(x, norm_mlp_g[i], w_up, w_down, i, norm_final_g, Tp if i == depth - 1 else None, tm_mlp, tf)

    def prompt_state(a):
        return jnp.swapaxes(a.reshape((a.shape[0], Bp, H, Tp0, hd)), 2, 3)

    def sample_state(a):
        return jnp.swapaxes(a, 2, 3)

    (fox_k, fox_v), (sb_k, sb_v) = kv_state[0], kv_state[1]
    y_prompt, y_sample = x
    return (y_prompt.reshape(Bp, Tp0, D), y_sample.reshape(Bs, Ts, D),
            prompt_state(fox_k[0]), prompt_state(fox_v[0]), jnp.stack(logf_p),
            prompt_state(sb_k[0]), prompt_state(sb_v[0]),
            jnp.stack(gla_p),
            mk_p.reshape(depth, Bp, Mm, MH, hd), mv_p.reshape(depth, Bp, Mm, MH, hd),
            sample_state(fox_k[1]), sample_state(fox_v[1]), jnp.stack(logf_s),
            sample_state(sb_k[1]), sample_state(sb_v[1]),
            jnp.stack(gla_s))
```
